```python
import math
import jax
import jax.numpy as jnp
from jax import lax
import numpy as np

D_MODEL = 2048
BATCH = 2
SEQ = 8192
DEPTH = 4

HEAD_DIM = 64
N_MIXERS = 4
GROUP_WIDTH = D_MODEL // N_MIXERS
ROPE_THETA = 10000.0
NORM_EPS = 1e-6
NEG_INF = -1e30

SWA_HEADS = GROUP_WIDTH // HEAD_DIM
SWA_KV_HEADS = 2
SWA_WINDOW = 128
SWA_BLOCK = 128
SWA_Q = SWA_HEADS * HEAD_DIM
SWA_KV = SWA_KV_HEADS * HEAD_DIM
SWA_COLS = SWA_Q + 2 * SWA_KV

MLA_HEADS = GROUP_WIDTH // HEAD_DIM
MLA_NOPE = 64
MLA_ROPE = 32
MLA_V = GROUP_WIDTH // MLA_HEADS
MLA_QK = MLA_NOPE + MLA_ROPE
MLA_Q_RANK = 3 * D_MODEL // 16
MLA_KV_RANK = D_MODEL // 16
MLA_Q_BLOCK = 128
MLA_COLS = MLA_Q_RANK + MLA_KV_RANK + MLA_ROPE

RWKV_HEAD_DIM = 64
RWKV_HEADS = GROUP_WIDTH // RWKV_HEAD_DIM
RWKV_DECAY_LORA = 64
RWKV_A_LORA = 64
RWKV_GATE_LORA = 128
RWKV_GN_EPS = 64e-5
RWKV_OFFSETS = (GROUP_WIDTH, 2 * GROUP_WIDTH, 3 * GROUP_WIDTH,
                3 * GROUP_WIDTH + 2 * RWKV_DECAY_LORA,
                3 * GROUP_WIDTH + 2 * RWKV_DECAY_LORA + 2 * RWKV_A_LORA)
RWKV_COLS = 3 * GROUP_WIDTH + 2 * RWKV_DECAY_LORA + 2 * RWKV_A_LORA + RWKV_GATE_LORA

S5_WIDTH = GROUP_WIDTH
S5_GROUP = 16
S5_GROUPS = S5_WIDTH // S5_GROUP
S5_STATE = 64
S5_DT_MIN = 1e-3
S5_DT_MAX = 1e-1

MIX_OFFSETS = (SWA_COLS, SWA_COLS + MLA_COLS, SWA_COLS + MLA_COLS + RWKV_COLS)
IN_COLS = SWA_COLS + MLA_COLS + RWKV_COLS + S5_WIDTH

N_EXPERTS = 16
EC_CAPACITY_FACTOR = 2
D_FF_EXPERT = D_MODEL // 2

kernel_name = 'hybrid_parallel_head_group_encoder'


def rms_norm(t, gain):
    tf = t.astype(jnp.float32)
    y = tf * lax.rsqrt(jnp.mean(tf * tf, axis=-1, keepdims=True) + NORM_EPS)
    return (y * gain.astype(jnp.float32)).astype(t.dtype)


def rope_tables(positions, dim):
    inv_freq = ROPE_THETA ** (-jnp.arange(0, dim, 2, dtype=jnp.float32) / dim)
    ang = positions.astype(jnp.float32)[..., None] * inv_freq
    return jnp.cos(ang), jnp.sin(ang)


def apply_rope(t, cos, sin):
    tf = t.astype(jnp.float32)
    t1, t2 = jnp.split(tf, 2, axis=-1)
    c = cos[:, :, None, :]
    s = sin[:, :, None, :]
    return jnp.concatenate([t1 * c - t2 * s, t2 * c + t1 * s], axis=-1).astype(t.dtype)


def swa_mixer(h, q_gain, k_gain, sink, cos, sin):
    B_, S_, _ = h.shape
    nb = S_ // SWA_BLOCK
    grp = SWA_HEADS // SWA_KV_HEADS
    q, k, v = jnp.split(h, [SWA_Q, SWA_Q + SWA_KV], axis=-1)
    q = apply_rope(rms_norm(q.reshape(B_, S_, SWA_HEADS, HEAD_DIM), q_gain), cos, sin)
    k = apply_rope(rms_norm(k.reshape(B_, S_, SWA_KV_HEADS, HEAD_DIM), k_gain), cos, sin)
    v = v.reshape(B_, S_, SWA_KV_HEADS, HEAD_DIM)
    qb = q.reshape(B_, nb, SWA_BLOCK, SWA_KV_HEADS, grp, HEAD_DIM)

    def band(t):
        tp = jnp.pad(t, ((0, 0), (SWA_BLOCK, SWA_BLOCK), (0, 0), (0, 0)))
        tp = tp.reshape(B_, nb + 2, SWA_BLOCK, SWA_KV_HEADS, HEAD_DIM)
        return jnp.concatenate([tp[:, :-2], tp[:, 1:-1], tp[:, 2:]], axis=2)

    kb, vb = band(k), band(v)
    s = jnp.einsum('bnqkgd,bnjkd->bnkgqj', qb, kb).astype(jnp.float32) * (HEAD_DIM ** -0.5)
    q_in = jnp.arange(SWA_BLOCK)[:, None] + SWA_BLOCK
    k_in = jnp.arange(3 * SWA_BLOCK)[None, :]
    key_abs = (jnp.arange(nb) * SWA_BLOCK - SWA_BLOCK)[:, None] + jnp.arange(3 * SWA_BLOCK)[None, :]
    mask = (jnp.abs(q_in - k_in) <= SWA_WINDOW)[None] & ((key_abs >= 0) & (key_abs < S_))[:, None, :]
    s = jnp.where(mask[None, :, None, None], s, NEG_INF)
    sink_l = sink.astype(jnp.float32).reshape(SWA_KV_HEADS, grp)[None, None, :, :, None]
    m = jnp.maximum(jnp.max(s, axis=-1), sink_l)
    p = jnp.exp(s - m[..., None])
    p = p / (jnp.sum(p, axis=-1) + jnp.exp(sink_l - m))[..., None]
    o = jnp.einsum('bnkgqj,bnjkd->bnqkgd', p.astype(v.dtype), vb)
    return o.reshape(B_, S_, SWA_Q)


def mla_mixer(h, q_a_gain, kv_a_gain, w_uq, w_ukv, q_gain, k_gain, cos, sin):
    B_, S_, _ = h.shape
    nb = S_ // MLA_Q_BLOCK
    cq, ckv, k_pe = jnp.split(h, [MLA_Q_RANK, MLA_Q_RANK + MLA_KV_RANK], axis=-1)
    cq = rms_norm(cq, q_a_gain)
    ckv = rms_norm(ckv, kv_a_gain)
    q = (cq @ w_uq).reshape(B_, S_, MLA_HEADS, MLA_QK)
    kv = (ckv @ w_ukv).reshape(B_, S_, MLA_HEADS, MLA_NOPE + MLA_V)
    k_nope, v = jnp.split(kv, [MLA_NOPE], axis=-1)
    k_pe = jnp.broadcast_to(k_pe[:, :, None, :], (B_, S_, MLA_HEADS, MLA_ROPE))
    k = jnp.concatenate([k_nope, k_pe], axis=-1)
    q = rms_norm(q, q_gain)
    k = rms_norm(k, k_gain)
    q = jnp.concatenate([q[..., :MLA_NOPE], apply_rope(q[..., MLA_NOPE:], cos, sin)], axis=-1)
    k = jnp.concatenate([k[..., :MLA_NOPE], apply_rope(k[..., MLA_NOPE:], cos, sin)], axis=-1)
    scale = MLA_QK ** -0.5
    qb = jnp.swapaxes(q.reshape(B_, nb, MLA_Q_BLOCK, MLA_HEADS, MLA_QK), 0, 1)

    def attend(q_blk):
        s = jnp.einsum('bqhd,bkhd->bhqk', q_blk, k).astype(jnp.float32) * scale
        p = jax.nn.softmax(s, axis=-1)
        return jnp.einsum('bhqk,bkhd->bqhd', p.astype(v.dtype), v)

    o = lax.map(attend, qb)
    return jnp.swapaxes(o, 0, 1).reshape(B_, S_, MLA_HEADS * MLA_V)


def token_shift_centered(t, mu):
    prev = jnp.pad(t, ((0, 0), (1, 0), (0, 0)))[:, :-1]
    nxt = jnp.pad(t, ((0, 0), (0, 1), (0, 0)))[:, 1:]
    return t + mu * (0.5 * (prev + nxt) - t)


def rwkv7_scan(r, w, k, v, a, b, reverse):
    B_, _, H_, N_ = r.shape
    xs = tuple(jnp.moveaxis(t, 1, 0) for t in (r, w, k, v, a, b))

    def step(state, inp):
        r_t, w_t, k_t, v_t, a_t, b_t = inp
        sa = jnp.einsum('bhij,bhj->bhi', state, a_t)
        state = (state * w_t[:, :, None, :] + sa[..., None] * b_t[:, :, None, :]
                 + v_t[..., None] * k_t[:, :, None, :])
        return state, jnp.einsum('bhij,bhj->bhi', state, r_t)

    state0 = jnp.zeros((B_, H_, N_, N_), jnp.float32)
    _, out = lax.scan(step, state0, xs, reverse=reverse)
    return jnp.moveaxis(out, 0, 1)


def rwkv7_mixer(h, mu, w0, w_up, a0, a_up, g_up, k_k, k_a, r_k, ln_w, ln_b):
    B_, S_, _ = h.shape
    f32 = jnp.float32
    h = token_shift_centered(h, mu)
    r, k, v, wd, ad, gd = jnp.split(h, RWKV_OFFSETS, axis=-1)
    wd = wd.reshape(B_, S_, 2, RWKV_DECAY_LORA)
    ad = ad.reshape(B_, S_, 2, RWKV_A_LORA)
    logw = -jax.nn.softplus(-(w0 + jnp.einsum('bsel,elc->bsec', jnp.tanh(wd), w_up)).astype(f32)) - 0.5
    decay = jnp.exp(-jnp.exp(logw))
    a = jax.nn.sigmoid((a0 + jnp.einsum('bsel,elc->bsec', ad, a_up)).astype(f32))
    g = jax.nn.sigmoid(gd) @ g_up

    def heads(t):
        return t.reshape(B_, S_, RWKV_HEADS, RWKV_HEAD_DIM).astype(f32)

    kk = heads(k * k_k)
    kk = kk * lax.rsqrt(jnp.sum(kk * kk, axis=-1, keepdims=True) + 1e-12)
    k_dir = k.astype(f32)[:, :, None, :] * (1.0 + (a - 1.0) * k_a.astype(f32))
    r_h, k_h, v_h = heads(r), heads(k), heads(v)
    outs = []
    for e in range(2):
        a_h = heads(a[:, :, e])
        outs.append(rwkv7_scan(r_h, heads(decay[:, :, e]), heads(k_dir[:, :, e]), v_h,
                               -kk, kk * a_h, reverse=(e == 1)))
    y = outs[0] + outs[1]
    mean = jnp.mean(y, axis=-1, keepdims=True)
    var = jnp.mean((y - mean) ** 2, axis=-1, keepdims=True)
    y = ((y - mean) * lax.rsqrt(var + RWKV_GN_EPS)).reshape(B_, S_, GROUP_WIDTH)
    y = y * ln_w.astype(f32) + ln_b.astype(f32)
    bonus = jnp.sum(r_h * k_h * r_k.astype(f32), axis=-1, keepdims=True) * v_h
    y = (y + bonus.reshape(B_, S_, GROUP_WIDTH)) * g.astype(f32)
    return y.astype(h.dtype)


def complex_affine_combine(e1, e2):
    a1r, a1i, b1r, b1i = e1
    a2r, a2i, b2r, b2i = e2
    return (a2r * a1r - a2i * a1i, a2r * a1i + a2i * a1r,
            a2r * b1r - a2i * b1i + b2r, a2r * b1i + a2i * b1r + b2i)


def s5_mixer(u, a_re, a_im, log_dt, b_re, b_im, c_re, c_im, d_skip, glu_w, glu_b):
    B_, S_, _ = u.shape
    f32 = jnp.float32
    uf = u.astype(f32)
    ug = uf.reshape(B_, S_, S5_GROUPS, S5_GROUP)
    b_re, b_im = b_re.astype(f32), b_im.astype(f32)
    x_re, x_im = [], []
    for e in range(2):
        lam_r, lam_i = a_re[e].astype(f32), a_im[e].astype(f32)
        dt = jnp.exp(log_dt[e].astype(f32))[:, None]
        mag = jnp.exp(lam_r * dt)
        abar_r, abar_i = mag * jnp.cos(lam_i * dt), mag * jnp.sin(lam_i * dt)
        den = lam_r * lam_r + lam_i * lam_i
        nr, ni = abar_r - 1.0, abar_i
        coef_r = (nr * lam_r + ni * lam_i) / den
        coef_i = (ni * lam_r - nr * lam_i) / den
        bb_r = coef_r[..., None] * b_re - coef_i[..., None] * b_im
        bb_i = coef_r[..., None] * b_im + coef_i[..., None] * b_re
        bu_r = jnp.einsum('bsgh,gph->bsgp', ug, bb_r)
        bu_i = jnp.einsum('bsgh,gph->bsgp', ug, bb_i)
        ar = jnp.broadcast_to(abar_r, bu_r.shape)
        ai = jnp.broadcast_to(abar_i, bu_r.shape)
        _, _, s_r, s_i = lax.associative_scan(complex_affine_combine, (ar, ai, bu_r, bu_i),
                                              reverse=(e == 1), axis=1)
        x_re.append(s_r)
        x_im.append(s_i)
    xr = x_re[0] + x_re[1]
    xi = x_im[0] + x_im[1]
    y = (jnp.einsum('bsgp,ghp->bsgh', xr, c_re.astype(f32))
         - jnp.einsum('bsgp,ghp->bsgh', xi, c_im.astype(f32)))
    y = y.reshape(B_, S_, S5_WIDTH) + d_skip.astype(f32) * uf
    y = jax.nn.gelu(y)
    y = y * jax.nn.sigmoid(y @ glu_w.astype(f32) + glu_b.astype(f32))
    return y.astype(u.dtype)


def expert_choice_moe(h, router_w, w1, w3, w2):
    B_, S_, _ = h.shape
    capacity = EC_CAPACITY_FACTOR * S_ // N_EXPERTS
    aff = jax.nn.softmax(jnp.einsum('bsd,de->bse', h, router_w).astype(jnp.float32), axis=-1)
    gate, idx = lax.top_k(jnp.swapaxes(aff, 1, 2), capacity)
    bidx = jnp.arange(B_)[:, None, None]
    xs = h[bidx, idx]
    hid = jax.nn.silu(jnp.einsum('becd,edf->becf', xs, w1)) * jnp.einsum('becd,edf->becf', xs, w3)
    ys = jnp.einsum('becf,efd->becd', hid, w2) * gate[..., None].astype(h.dtype)
    return jnp.zeros_like(h).at[bidx, idx].add(ys)


def setup_inputs(seed: int = 0) -> dict:
    key = jax.random.key(seed)
    keys = jax.random.split(key, 48)
    counter = [0]
    f32 = jnp.float32
    L = DEPTH

    def nk():
        counter[0] += 1
        return keys[counter[0] - 1]

    def nrm(shape, scale):
        return jax.random.normal(nk(), shape, f32) * scale

    def unif(shape, lo, hi):
        return jax.random.uniform(nk(), shape, f32, lo, hi)

    x = nrm((BATCH, SEQ, D_MODEL), 1.0)
    c = nrm((BATCH, D_MODEL), 1.0)
    offset = jax.random.randint(nk(), (BATCH, 1), 0, 4096, dtype=jnp.int32)
    positions = offset + jnp.arange(SEQ, dtype=jnp.int32)[None, :]
    inv_d = D_MODEL ** -0.5
    return {
        'x': x,
        'c': c,
        'positions': positions,
        'ada_w': nrm((L, D_MODEL, 6 * D_MODEL), inv_d),
        'ada_b': nrm((L, 6 * D_MODEL), 0.02),
        'norm1_g': 1.0 + nrm((L, D_MODEL), 0.02),
        'norm2_g': 1.0 + nrm((L, D_MODEL), 0.02),
        'w_in': nrm((L, D_MODEL, IN_COLS), inv_d),
        'w_out': nrm((L, N_MIXERS * GROUP_WIDTH, D_MODEL), (N_MIXERS * GROUP_WIDTH) ** -0.5),
        'swa_q_gain': 1.0 + nrm((L, HEAD_DIM), 0.02),
        'swa_k_gain': 1.0 + nrm((L, HEAD_DIM), 0.02),
        'swa_sink': nrm((L, SWA_HEADS), 0.5),
        'mla_q_a_gain': 1.0 + nrm((L, MLA_Q_RANK), 0.02),
        'mla_kv_a_gain': 1.0 + nrm((L, MLA_KV_RANK), 0.02),
        'mla_w_uq': nrm((L, MLA_Q_RANK, MLA_HEADS * MLA_QK), MLA_Q_RANK ** -0.5),
        'mla_w_ukv': nrm((L, MLA_KV_RANK, MLA_HEADS * (MLA_NOPE + MLA_V)), MLA_KV_RANK ** -0.5),
        'mla_q_gain': 1.0 + nrm((L, MLA_QK), 0.02),
        'mla_k_gain': 1.0 + nrm((L, MLA_QK), 0.02),
        'rwkv_mu': unif((L, RWKV_COLS), 0.0, 1.0),
        'rwkv_w0': unif((L, 2, GROUP_WIDTH), -7.0, -2.0),
        'rwkv_w_up': nrm((L, 2, RWKV_DECAY_LORA, GROUP_WIDTH), 0.5 * RWKV_DECAY_LORA ** -0.5),
        'rwkv_a0': nrm((L, 2, GROUP_WIDTH), 0.1),
        'rwkv_a_up': nrm((L, 2, RWKV_A_LORA, GROUP_WIDTH), 0.5 * RWKV_A_LORA ** -0.5),
        'rwkv_g_up': nrm((L, RWKV_GATE_LORA, GROUP_WIDTH), RWKV_GATE_LORA ** -0.5),
        'rwkv_k_k': 0.85 + nrm((L, GROUP_WIDTH), 0.02),
        'rwkv_k_a': 1.0 + nrm((L, GROUP_WIDTH), 0.02),
        'rwkv_r_k': nrm((L, RWKV_HEADS, RWKV_HEAD_DIM), 0.1),
        'rwkv_ln_w': 1.0 + nrm((L, GROUP_WIDTH), 0.02),
        'rwkv_ln_b': nrm((L, GROUP_WIDTH), 0.02),
        's5_a_re': -0.5 + nrm((L, 2, S5_GROUPS, S5_STATE), 0.01),
        's5_a_im': jnp.pi * jnp.arange(S5_STATE, dtype=f32) + nrm((L, 2, S5_GROUPS, S5_STATE), 0.01),
        's5_log_dt': unif((L, 2, S5_GROUPS), math.log(S5_DT_MIN), math.log(S5_DT_MAX)),
        's5_b_re': nrm((L, S5_GROUPS, S5_STATE, S5_GROUP), (2 * S5_GROUP) ** -0.5),
        's5_b_im': nrm((L, S5_GROUPS, S5_STATE, S5_GROUP), (2 * S5_GROUP) ** -0.5),
        's5_c_re': nrm((L, S5_GROUPS, S5_GROUP, S5_STATE), (2 * S5_STATE) ** -0.5),
        's5_c_im': nrm((L, S5_GROUPS, S5_GROUP, S5_STATE), (2 * S5_STATE) ** -0.5),
        's5_d': nrm((L, S5_WIDTH), 1.0),
        's5_glu_w': nrm((L, S5_WIDTH, S5_WIDTH), S5_WIDTH ** -0.5),
        's5_glu_b': nrm((L, S5_WIDTH), 0.02),
        'router_w': nrm((L, D_MODEL, N_EXPERTS), inv_d),
        'moe_w1': nrm((L, N_EXPERTS, D_MODEL, D_FF_EXPERT), inv_d),
        'moe_w3': nrm((L, N_EXPERTS, D_MODEL, D_FF_EXPERT), inv_d),
        'moe_w2': nrm((L, N_EXPERTS, D_FF_EXPERT, D_MODEL), D_FF_EXPERT ** -0.5),
    }


def reference(x, c, positions, ada_w, ada_b, norm1_g, norm2_g, w_in, w_out,
              swa_q_gain, swa_k_gain, swa_sink,
              mla_q_a_gain, mla_kv_a_gain, mla_w_uq, mla_w_ukv, mla_q_gain, mla_k_gain,
              rwkv_mu, rwkv_w0, rwkv_w_up, rwkv_a0, rwkv_a_up, rwkv_g_up, rwkv_k_k, rwkv_k_a,
              rwkv_r_k, rwkv_ln_w, rwkv_ln_b,
              s5_a_re, s5_a_im, s5_log_dt, s5_b_re, s5_b_im, s5_c_re, s5_c_im, s5_d,
              s5_glu_w, s5_glu_b,
              router_w, moe_w1, moe_w3, moe_w2):
    cos_h, sin_h = rope_tables(positions, HEAD_DIM)
    cos_r, sin_r = rope_tables(positions, MLA_ROPE)
    cond = jax.nn.silu(c)
    for l in range(DEPTH):
        mod = cond @ ada_w[l] + ada_b[l]
        sh1, sc1, g1, sh2, sc2, g2 = [m[:, None, :] for m in jnp.split(mod, 6, axis=-1)]
        xn = rms_norm(x, norm1_g[l]) * (1.0 + sc1) + sh1
        h = xn @ w_in[l]
        h_swa, h_mla, h_rwkv, h_s5 = jnp.split(h, MIX_OFFSETS, axis=-1)
        o = jnp.concatenate([
            swa_mixer(h_swa, swa_q_gain[l], swa_k_gain[l], swa_sink[l], cos_h, sin_h).astype(x.dtype),
            mla_mixer(h_mla, mla_q_a_gain[l], mla_kv_a_gain[l], mla_w_uq[l], mla_w_ukv[l],
                      mla_q_gain[l], mla_k_gain[l], cos_r, sin_r).astype(x.dtype),
            rwkv7_mixer(h_rwkv, rwkv_mu[l], rwkv_w0[l], rwkv_w_up[l], rwkv_a0[l], rwkv_a_up[l],
                        rwkv_g_up[l], rwkv_k_k[l], rwkv_k_a[l], rwkv_r_k[l],
                        rwkv_ln_w[l], rwkv_ln_b[l]).astype(x.dtype),
            s5_mixer(h_s5, s5_a_re[l], s5_a_im[l], s5_log_dt[l], s5_b_re[l], s5_b_im[l],
                     s5_c_re[l], s5_c_im[l], s5_d[l], s5_glu_w[l], s5_glu_b[l]).astype(x.dtype),
        ], axis=-1)
        x = x + g1 * (o @ w_out[l])
        xn = rms_norm(x, norm2_g[l]) * (1.0 + sc2) + sh2
        x = x + g2 * expert_choice_moe(xn, router_w[l], moe_w1[l], moe_w3[l], moe_w2[l])
    return x
```

```python
import functools
import math

import jax
import jax.numpy as jnp
from jax import lax
from jax.experimental import pallas as pl
from jax.experimental.pallas import tpu as pltpu

f32 = jnp.float32
bf16 = jnp.bfloat16

D_MODEL = 2048
HEAD_DIM = 64
GROUP_WIDTH = 512
ROPE_THETA = 10000.0
NORM_EPS = 1e-6
NEG_INF = -1e30

SWA_HEADS = 8
SWA_KV_HEADS = 2
SWA_WINDOW = 128
SWA_BLOCK = 128
SWA_Q = 512
SWA_KV = 128
SWA_COLS = 768

MLA_HEADS = 8
MLA_NOPE = 64
MLA_ROPE = 32
MLA_V = 64
MLA_QK = 96
MLA_Q_RANK = 384
MLA_KV_RANK = 128
MLA_COLS = 544
MLA_COLS_PAD = 640
MLA_HEAD_PAD = 128

RWKV_HEADS = 8
RWKV_LORA = 64
RWKV_GATE_LORA = 128
RWKV_GN_EPS = 64e-5
RWKV_COLS = 1920
RWKV_CHUNK = 64

S5_GROUP = 16
S5_GROUPS = 32
S5_STATE = 64
S5_CHUNK = 32

N_EXPERTS = 16
EC_CAPACITY_FACTOR = 2
D_FF_EXPERT = 1024

VMEM_LIMIT_BYTES = 52 * 1024 * 1024
LANES = 128


def _cparams(*sem):
    return pltpu.CompilerParams(dimension_semantics=sem, vmem_limit_bytes=VMEM_LIMIT_BYTES)


def _sigmoid(x):
    return 1.0 / (1.0 + jnp.exp(-x))


def _split_bf16(x):
    hi = x.astype(bf16)
    lo = (x - hi.astype(f32)).astype(bf16)
    return hi, lo


def _dot(a, b):
    return jnp.dot(a, b, preferred_element_type=f32)


def _dot_nt(a, b):
    return lax.dot_general(a, b, (((1,), (1,)), ((), ())), preferred_element_type=f32)


def _dot_tn(a, b):
    return lax.dot_general(a, b, (((0,), (0,)), ((), ())), preferred_element_type=f32)


def _ada_kernel(c_ref, w_ref, b_ref, o_ref):
    c = c_ref[...]
    cond = (c * _sigmoid(c)).astype(bf16)
    o_ref[0] = _dot(cond, w_ref[0].astype(bf16)) + b_ref[0]


def _ada_mod(c, ada_w, ada_b):
    depth, d, n = ada_w.shape
    b = c.shape[0]
    rows = 8
    c_pad = jnp.zeros((rows, d), f32).at[:b].set(c)
    tn = 512
    out = pl.pallas_call(
        _ada_kernel,
        grid=(depth, n // tn),
        in_specs=[
            pl.BlockSpec((rows, d), lambda l, j: (0, 0)),
            pl.BlockSpec((1, d, tn), lambda l, j: (l, 0, j)),
            pl.BlockSpec((1, 1, tn), lambda l, j: (l, 0, j)),
        ],
        out_specs=pl.BlockSpec((1, rows, tn), lambda l, j: (l, 0, j)),
        out_shape=jax.ShapeDtypeStruct((depth, rows, n), f32),
        compiler_params=_cparams("parallel", "parallel"),
        name="ada_mod",
    )(c_pad, ada_w, ada_b.reshape(depth, 1, n))
    return out[:, :b]


def _mm_kernel(x_ref, w_ref, o_ref):
    o_ref[...] = _dot(x_ref[...].astype(bf16), w_ref[...].astype(bf16)).astype(o_ref.dtype)


def _matmul(x, w, *, tm, tn, name):
    m, k = x.shape
    n = w.shape[1]
    tm = min(tm, m)
    return pl.pallas_call(
        _mm_kernel,
        grid=(m // tm, n // tn),
        in_specs=[pl.BlockSpec((tm, k), lambda i, j: (i, 0)),
                  pl.BlockSpec((k, tn), lambda i, j: (0, j))],
        out_specs=pl.BlockSpec((tm, tn), lambda i, j: (i, j)),
        out_shape=jax.ShapeDtypeStruct((m, n), f32),
        compiler_params=_cparams("parallel", "parallel"),
        name=name,
    )(x, w)


def _norm_mod(x_ref, g_ref, sc_ref, sh_ref):
    x = x_ref[...]
    ms = jnp.mean(x * x, axis=-1, keepdims=True)
    y = x * lax.rsqrt(ms + NORM_EPS) * g_ref[...]
    return y * (1.0 + sc_ref[0]) + sh_ref[0]


def _norm_kernel(x_ref, g_ref, sc_ref, sh_ref, o_ref):
    o_ref[...] = _norm_mod(x_ref, g_ref, sc_ref, sh_ref).astype(o_ref.dtype)


def _norm_router_kernel(x_ref, g_ref, sc_ref, sh_ref, whi_ref, wlo_ref, o_ref, logit_ref):
    y = _norm_mod(x_ref, g_ref, sc_ref, sh_ref)
    o_ref[...] = y.astype(o_ref.dtype)
    yhi, ylo = _split_bf16(y)
    whi = whi_ref[...]
    logit_ref[...] = _dot(yhi, whi) + _dot(ylo, whi) + _dot(yhi, wlo_ref[...])


def _norm_call(x, gain, scale, shift, seq, router_w=None):
    t, d = x.shape
    tm = min(512, seq)
    per_batch = seq // tm
    row = pl.BlockSpec((tm, d), lambda i: (i, 0))
    mod = pl.BlockSpec((1, 1, d), lambda i: (i // per_batch, 0, 0))
    in_specs = [row, pl.BlockSpec((1, d), lambda i: (0, 0)), mod, mod]
    args = [x, gain.reshape(1, d), scale, shift]
    if router_w is None:
        return pl.pallas_call(
            _norm_kernel, grid=(t // tm,), in_specs=in_specs, out_specs=row,
            out_shape=jax.ShapeDtypeStruct((t, d), bf16),
            compiler_params=_cparams("parallel"), name="norm_mod",
        )(*args)
    e = router_w.shape[1]
    w_pad = jnp.zeros((d, LANES), f32).at[:, :e].set(router_w)
    whi, wlo = _split_bf16(w_pad)
    wspec = pl.BlockSpec((d, LANES), lambda i: (0, 0))
    xn, logits = pl.pallas_call(
        _norm_router_kernel, grid=(t // tm,), in_specs=in_specs + [wspec, wspec],
        out_specs=[row, pl.BlockSpec((tm, LANES), lambda i: (i, 0))],
        out_shape=[jax.ShapeDtypeStruct((t, d), bf16), jax.ShapeDtypeStruct((t, LANES), f32)],
        compiler_params=_cparams("parallel"), name="norm_mod_router",
    )(*args, whi, wlo)
    return xn, logits[:, :e]


def _rope_lanes(x, cos_t, sin_t, half, first_mask):
    n = x.shape[-1]
    fwd = pltpu.roll(x, n - half, axis=1)
    bwd = pltpu.roll(x, half, axis=1)
    return x * cos_t + jnp.where(first_mask, fwd, bwd) * sin_t


def _rope_tables(positions):
    t = positions.size
    pos = positions.reshape(t, 1).astype(f32)

    def tables(dim):
        inv_freq = ROPE_THETA ** (-jnp.arange(0, dim, 2, dtype=f32) / dim)
        ang = pos * inv_freq
        return jnp.cos(ang), jnp.sin(ang)

    c, s = tables(HEAD_DIM)
    cos_h = jnp.concatenate([c, c, c, c], axis=-1)
    sin_h = jnp.concatenate([-s, s, -s, s], axis=-1)
    c, s = tables(MLA_ROPE)
    one = jnp.ones((t, MLA_NOPE), f32)
    zero = jnp.zeros((t, MLA_NOPE), f32)
    pad1 = jnp.ones((t, MLA_HEAD_PAD - MLA_QK), f32)
    pad0 = jnp.zeros((t, MLA_HEAD_PAD - MLA_QK), f32)
    cos_r = jnp.concatenate([one, c, c, pad1], axis=-1)
    sin_r = jnp.concatenate([zero, -s, s, pad0], axis=-1)
    return cos_h, sin_h, cos_r, sin_r


def _pair_rmsnorm(x, gain):
    lane = lax.broadcasted_iota(jnp.int32, x.shape, 1)
    lo = lane < HEAD_DIM
    x2 = x * x
    s_lo = jnp.sum(jnp.where(lo, x2, 0.0), axis=-1, keepdims=True)
    s_hi = jnp.sum(jnp.where(lo, 0.0, x2), axis=-1, keepdims=True)
    ms = jnp.where(lo, s_lo, s_hi) * (1.0 / HEAD_DIM)
    return x * lax.rsqrt(ms + NORM_EPS) * gain


def _swa_prep_kernel(h_ref, cos_ref, sin_ref, qg_ref, kg_ref, q_ref, k_ref, v_ref):
    cos_t = cos_ref[...]
    sin_t = sin_ref[...]
    lane = lax.broadcasted_iota(jnp.int32, cos_t.shape, 1)
    first = (lane % HEAD_DIM) < (HEAD_DIM // 2)
    scale = HEAD_DIM ** -0.5
    for p in range(SWA_Q // LANES):
        x = _pair_rmsnorm(h_ref[:, p * LANES:(p + 1) * LANES], qg_ref[...])
        x = _rope_lanes(x, cos_t, sin_t, HEAD_DIM // 2, first)
        q_ref[:, p * LANES:(p + 1) * LANES] = (x * scale).astype(bf16)
    x = _pair_rmsnorm(h_ref[:, SWA_Q:SWA_Q + SWA_KV], kg_ref[...])
    k_ref[...] = _rope_lanes(x, cos_t, sin_t, HEAD_DIM // 2, first).astype(bf16)
    v_ref[...] = h_ref[:, SWA_Q + SWA_KV:SWA_COLS].astype(bf16)


def _swa_attn_kernel(q_ref, k_ref, v_ref, sink_ref, o_ref, *, seq):
    n = pl.program_id(1)
    band = 3 * SWA_BLOCK
    start = pl.multiple_of(jnp.clip((n - 1) * SWA_BLOCK, 0, seq - band), SWA_BLOCK)
    kb = k_ref[pl.ds(start, band), :]
    vb = v_ref[pl.ds(start, band), :]
    q = q_ref[...]
    grp = SWA_HEADS // SWA_KV_HEADS
    rows = grp * SWA_BLOCK
    qpos = n * SWA_BLOCK + lax.broadcasted_iota(jnp.int32, (rows, band), 0) % SWA_BLOCK
    kpos = start + lax.broadcasted_iota(jnp.int32, (rows, band), 1)
    valid = jnp.abs(qpos - kpos) <= SWA_WINDOW
    outs = []
    for kh in range(SWA_KV_HEADS):
        qs = jnp.concatenate(
            [q[:, (kh * grp + g) * HEAD_DIM:(kh * grp + g + 1) * HEAD_DIM] for g in range(grp)], axis=0)
        k_h = kb[:, kh * HEAD_DIM:(kh + 1) * HEAD_DIM]
        v_h = vb[:, kh * HEAD_DIM:(kh + 1) * HEAD_DIM]
        s = jnp.where(valid, _dot_nt(qs, k_h), NEG_INF)
        sink = jnp.concatenate(
            [jnp.broadcast_to(sink_ref[kh * grp + g:kh * grp + g + 1, 0:1], (SWA_BLOCK, 1))
             for g in range(grp)], axis=0)
        m = jnp.maximum(jnp.max(s, axis=-1, keepdims=True), sink)
        p = jnp.exp(s - m)
        denom = jnp.sum(p, axis=-1, keepdims=True) + jnp.exp(sink - m)
        o = _dot(p.astype(bf16), v_h) / denom
        for g in range(grp):
            outs.append(o[g * SWA_BLOCK:(g + 1) * SWA_BLOCK])
    o_ref[...] = jnp.concatenate(outs, axis=-1)


def _swa_mixer(h_swa, cos_h, sin_h, q_gain, k_gain, sink, batch, seq):
    t = h_swa.shape[0]
    tm = min(512, seq)
    row = lambda w: pl.BlockSpec((tm, w), lambda i: (i, 0))
    const = pl.BlockSpec((1, LANES), lambda i: (0, 0))
    q, k, v = pl.pallas_call(
        _swa_prep_kernel, grid=(t // tm,),
        in_specs=[row(SWA_COLS), row(LANES), row(LANES), const, const],
        out_specs=[row(SWA_Q), row(SWA_KV), row(SWA_KV)],
        out_shape=[jax.ShapeDtypeStruct((t, SWA_Q), bf16),
                   jax.ShapeDtypeStruct((t, SWA_KV), bf16),
                   jax.ShapeDtypeStruct((t, SWA_KV), bf16)],
        compiler_params=_cparams("parallel"), name="swa_prep",
    )(h_swa, cos_h, sin_h, jnp.tile(q_gain, 2).reshape(1, LANES), jnp.tile(k_gain, 2).reshape(1, LANES))
    nb = seq // SWA_BLOCK
    sink_t = jnp.broadcast_to(sink.astype(f32).reshape(SWA_HEADS, 1), (SWA_HEADS, LANES))
    return pl.pallas_call(
        functools.partial(_swa_attn_kernel, seq=seq), grid=(batch, nb),
        in_specs=[pl.BlockSpec((SWA_BLOCK, SWA_Q), lambda b, n: (b * nb + n, 0)),
                  pl.BlockSpec((seq, SWA_KV), lambda b, n: (b, 0)),
                  pl.BlockSpec((seq, SWA_KV), lambda b, n: (b, 0)),
                  pl.BlockSpec((SWA_HEADS, LANES), lambda b, n: (0, 0))],
        out_specs=pl.BlockSpec((SWA_BLOCK, SWA_Q), lambda b, n: (b * nb + n, 0)),
        out_shape=jax.ShapeDtypeStruct((t, SWA_Q), f32),
        compiler_params=_cparams("parallel", "parallel"), name="swa_attn",
    )(q, k, v, sink_t)


def _mla_prep_kernel(h_ref, cos_ref, sin_ref, qag_ref, kvag_ref, wq_ref, wk_ref, wpe_ref, wv_ref,
                     qg_ref, kg_ref, q_ref, k_ref, v_ref):
    cos_t = cos_ref[...]
    sin_t = sin_ref[...]
    lane = lax.broadcasted_iota(jnp.int32, cos_t.shape, 1)
    first = lane < MLA_NOPE + MLA_ROPE // 2
    cq = h_ref[:, :MLA_Q_RANK]
    cq = cq * lax.rsqrt(jnp.mean(cq * cq, axis=-1, keepdims=True) + NORM_EPS) * qag_ref[...]
    ckv = h_ref[:, MLA_Q_RANK:MLA_Q_RANK + MLA_KV_RANK]
    ckv = (ckv * lax.rsqrt(jnp.mean(ckv * ckv, axis=-1, keepdims=True) + NORM_EPS) * kvag_ref[...]).astype(bf16)
    kpe = h_ref[:, MLA_Q_RANK + MLA_KV_RANK:MLA_COLS_PAD].astype(bf16)
    q_all = _dot(cq.astype(bf16), wq_ref[...])
    k_all = _dot(ckv, wk_ref[...]) + _dot(kpe, wpe_ref[...])
    v_all = _dot(ckv, wv_ref[...])
    one_lane = lane == MLA_V
    scale = MLA_QK ** -0.5

    def head_norm(x, gain):
        ms = jnp.sum(x * x, axis=-1, keepdims=True) * (1.0 / MLA_QK)
        x = x * lax.rsqrt(ms + NORM_EPS) * gain
        return _rope_lanes(x, cos_t, sin_t, MLA_ROPE // 2, first)

    for h in range(MLA_HEADS):
        sl = slice(h * MLA_HEAD_PAD, (h + 1) * MLA_HEAD_PAD)
        q_ref[0, h] = (head_norm(q_all[:, sl], qg_ref[...]) * scale).astype(bf16)
        k_ref[0, h] = head_norm(k_all[:, sl], kg_ref[...]).astype(bf16)
        v_ref[0, h] = jnp.where(one_lane, 1.0, v_all[:, sl]).astype(bf16)


def _mla_attn_kernel(q_ref, k_ref, v_ref, o_ref, *, tk, nk):
    outs = []
    for hh in range(2):
        q = q_ref[0, hh]
        tq = q.shape[0]

        def body(j, carry, hh=hh, q=q):
            m, acc = carry
            off = pl.multiple_of(j * tk, tk)
            kj = k_ref[0, hh, pl.ds(off, tk), :]
            vj = v_ref[0, hh, pl.ds(off, tk), :]
            s = _dot_nt(q, kj)
            m_new = jnp.maximum(m, jnp.max(s, axis=-1, keepdims=True))
            alpha = jnp.exp(m - m_new)
            p = jnp.exp(s - m_new)
            return m_new, acc * alpha + _dot(p.astype(bf16), vj)

        m0 = jnp.full((tq, 1), NEG_INF, f32)
        acc0 = jnp.zeros((tq, MLA_HEAD_PAD), f32)
        _, acc = lax.fori_loop(0, nk, body, (m0, acc0))
        outs.append(acc[:, :MLA_V] / acc[:, MLA_V:MLA_V + 1])
    o_ref[0] = jnp.concatenate(outs, axis=-1)


def _mla_weights(w_uq, w_ukv, q_gain, k_gain):
    hp = MLA_HEAD_PAD
    wq = jnp.zeros((MLA_Q_RANK, MLA_HEADS, hp), f32).at[:, :, :MLA_QK].set(
        w_uq.reshape(MLA_Q_RANK, MLA_HEADS, MLA_QK))
    kv = w_ukv.reshape(MLA_KV_RANK, MLA_HEADS, MLA_NOPE + MLA_V)
    wk = jnp.zeros((MLA_KV_RANK, MLA_HEADS, hp), f32).at[:, :, :MLA_NOPE].set(kv[:, :, :MLA_NOPE])
    wv = jnp.zeros((MLA_KV_RANK, MLA_HEADS, hp), f32).at[:, :, :MLA_V].set(kv[:, :, MLA_NOPE:])
    eye = jnp.eye(MLA_ROPE, dtype=f32)
    wpe = jnp.zeros((LANES, MLA_HEADS, hp), f32).at[:MLA_ROPE, :, MLA_NOPE:MLA_QK].set(
        jnp.broadcast_to(eye[:, None, :], (MLA_ROPE, MLA_HEADS, MLA_ROPE)))
    flat = lambda w: w.reshape(w.shape[0], MLA_HEADS * hp).astype(bf16)
    pad = lambda g: jnp.zeros((1, hp), f32).at[0, :MLA_QK].set(g)
    return flat(wq), flat(wk), flat(wpe), flat(wv), pad(q_gain), pad(k_gain)


def _mla_mixer(h_mla, cos_r, sin_r, q_a_gain, kv_a_gain, w_uq, w_ukv, q_gain, k_gain, batch, seq):
    t = h_mla.shape[0]
    tm = min(512, seq)
    per_batch = seq // tm
    hp = MLA_HEAD_PAD
    wq, wk, wpe, wv, qg, kg = _mla_weights(w_uq, w_ukv, q_gain, k_gain)
    row = lambda w: pl.BlockSpec((tm, w), lambda b, i: (b * per_batch + i, 0))
    const = lambda a: pl.BlockSpec(a.shape, lambda b, i: (0,) * a.ndim)
    head_out = pl.BlockSpec((1, MLA_HEADS, tm, hp), lambda b, i: (b, 0, i, 0))
    head_shape = jax.ShapeDtypeStruct((batch, MLA_HEADS, seq, hp), bf16)
    qag = q_a_gain.reshape(1, MLA_Q_RANK)
    kvag = kv_a_gain.reshape(1, MLA_KV_RANK)
    q, k, v = pl.pallas_call(
        _mla_prep_kernel, grid=(batch, per_batch),
        in_specs=[row(MLA_COLS_PAD), row(LANES), row(LANES), const(qag), const(kvag),
                  const(wq), const(wk), const(wpe), const(wv), const(qg), const(kg)],
        out_specs=[head_out, head_out, head_out],
        out_shape=[head_shape, head_shape, head_shape],
        compiler_params=_cparams("parallel", "parallel"), name="mla_prep",
    )(h_mla, cos_r, sin_r, qag, kvag, wq, wk, wpe, wv, qg, kg)
    tq = min(256, seq)
    tk = min(512, seq)
    kv_spec = pl.BlockSpec((1, 2, seq, hp), lambda b, h, i: (b, h, 0, 0))
    o = pl.pallas_call(
        functools.partial(_mla_attn_kernel, tk=tk, nk=seq // tk),
        grid=(batch, MLA_HEADS // 2, seq // tq),
        in_specs=[pl.BlockSpec((1, 2, tq, hp), lambda b, h, i: (b, h, i, 0)), kv_spec, kv_spec],
        out_specs=pl.BlockSpec((1, tq, 2 * MLA_V), lambda b, h, i: (b, i, h)),
        out_shape=jax.ShapeDtypeStruct((batch, seq, MLA_HEADS * MLA_V), f32),
        compiler_params=_cparams("parallel", "parallel", "parallel"), name="mla_attn",
    )(q, k, v)
    return o.reshape(t, MLA_HEADS * MLA_V)


def _head_sum(x, bd):
    hi, lo = _split_bf16(x)
    return _dot(hi, bd) + _dot(lo, bd)


def _rwkv_prep_kernel(h_ref, prev_ref, next_ref, mu_ref, w0_ref, wup_ref, a0_ref, aup_ref, gup_ref,
                      kk_ref, ka_ref, rk_ref, bd_ref,
                      r_ref, v_ref, an_ref, g_ref, bonus_ref, lw_ref, kd_ref, bdir_ref, *, per_batch):
    i = pl.program_id(0)
    x = h_ref[...]
    tm = x.shape[0]
    row = lax.broadcasted_iota(jnp.int32, x.shape, 0)
    prev_row = jnp.where(i % per_batch == 0, 0.0, prev_ref[7:8, :])
    next_row = jnp.where(i % per_batch == per_batch - 1, 0.0, next_ref[0:1, :])
    x_prev = jnp.where(row == 0, prev_row, pltpu.roll(x, 1, axis=0))
    x_next = jnp.where(row == tm - 1, next_row, pltpu.roll(x, tm - 1, axis=0))
    hs = x + mu_ref[...] * (0.5 * (x_prev + x_next) - x)
    gw = GROUP_WIDTH
    r = hs[:, 0:gw]
    k = hs[:, gw:2 * gw]
    v = hs[:, 2 * gw:3 * gw]
    wd = hs[:, 3 * gw:3 * gw + LANES]
    ad = hs[:, 3 * gw + LANES:3 * gw + 2 * LANES]
    gd = hs[:, 3 * gw + 2 * LANES:RWKV_COLS]
    bd = bd_ref[...]
    lora_w = _dot(jnp.tanh(wd).astype(bf16), wup_ref[...])
    lora_a = _dot(ad.astype(bf16), aup_ref[...])
    kk = k * kk_ref[...]
    kk = kk * lax.rsqrt(_head_sum(kk * kk, bd) + 1e-12)
    r_ref[...] = r
    v_ref[...] = v
    an_ref[...] = -kk
    g_ref[...] = _dot(_sigmoid(gd).astype(bf16), gup_ref[...])
    bonus_ref[...] = _head_sum(r * k * rk_ref[...], bd) * v
    for e in range(2):
        z = -(w0_ref[e:e + 1, :] + lora_w[:, e * gw:(e + 1) * gw])
        softplus = jnp.maximum(z, 0.0) + jnp.log(1.0 + jnp.exp(-jnp.abs(z)))
        lw_ref[e] = -jnp.exp(-softplus - 0.5)
        a = _sigmoid(a0_ref[e:e + 1, :] + lora_a[:, e * gw:(e + 1) * gw])
        kd_ref[e] = k * (1.0 + (a - 1.0) * ka_ref[...])
        bdir_ref[e] = kk * a


def _rwkv_chunk_kernel(r_ref, v_ref, an_ref, lw_ref, kd_ref, b_ref, o_ref, h_scr):
    e = pl.program_id(1)
    c = pl.program_id(2)
    L = RWKV_CHUNK
    N = HEAD_DIM

    @pl.when(c == 0)
    def _():
        h_scr[...] = jnp.zeros_like(h_scr)

    ti = lax.broadcasted_iota(jnp.int32, (L, L), 0)
    si = lax.broadcasted_iota(jnp.int32, (L, L), 1)
    sign = jnp.where(e == 0, 1, -1)
    ahead = (ti - si) * sign
    strict = ahead > 0
    incl = ahead >= 0
    lw = lw_ref[0]
    hi, lo = _split_bf16(lw)
    tri = jnp.where(incl, 1.0, 0.0).astype(bf16)
    cum = _dot(tri, hi) + _dot(tri, lo)
    p_incl = jnp.exp(cum)
    p_inv = jnp.exp(-cum)
    p_prev = jnp.exp(cum - lw)
    p_last = jnp.exp(jnp.sum(lw, axis=0, keepdims=True))
    a_t = an_ref[...] * p_prev
    b_t = b_ref[0] * p_inv
    k_t = kd_ref[0] * p_inv
    r_t = r_ref[...] * p_incl
    b_h = b_t * p_last
    k_h = k_t * p_last
    v = v_ref[...]
    eye = jnp.where(ti == si, 1.0, 0.0)
    for h in range(RWKV_HEADS):
        sl = slice(h * N, (h + 1) * N)
        ah, bh, kh, rh, vh = a_t[:, sl], b_t[:, sl], k_t[:, sl], r_t[:, sl], v[:, sl]
        lhs = jnp.concatenate([ah, rh], axis=0).astype(bf16)
        rhs = jnp.concatenate([bh, kh], axis=0).astype(bf16)
        prod = _dot_nt(lhs, rhs)
        a_ab = jnp.where(strict, prod[:L, :L], 0.0)
        a_ak = jnp.where(strict, prod[:L, L:], 0.0)
        a_rb = jnp.where(incl, prod[L:, :L], 0.0)
        a_rk = jnp.where(incl, prod[L:, L:], 0.0)
        t_inv = eye
        m = 1
        while m < L:
            same = (ti // (2 * m)) == (si // (2 * m))
            later_half = ((ti // m) % 2 - (si // m) % 2) * sign == 1
            a_off = jnp.where(same, jnp.where(later_half, a_ab, 0.0), 0.0)
            if m == 1:
                t_inv = eye + a_off
            else:
                tb = t_inv.astype(bf16)
                t_inv = t_inv + _dot(_dot(tb, a_off.astype(bf16)).astype(bf16), tb)
            m *= 2
        vb = vh.astype(bf16)
        tb = t_inv.astype(bf16)
        akv = _dot(a_ak.astype(bf16), vb)
        wz = _dot(tb, jnp.concatenate([ah, akv], axis=1).astype(bf16))
        qy = _dot(a_rb.astype(bf16), wz.astype(bf16))
        q_mat = rh + qy[:, :N]
        y_mat = qy[:, N:] + _dot(a_rk.astype(bf16), vb)
        h_state = h_scr[h]
        qw = jnp.concatenate([q_mat, wz[:, :N]], axis=0).astype(bf16)
        ou = _dot(qw, h_state.astype(bf16))
        o_ref[0, :, sl] = ou[:L] + y_mat
        u_mat = ou[L:] + wz[:, N:]
        upd = _dot_tn(jnp.concatenate([b_h[:, sl], k_h[:, sl]], axis=0).astype(bf16),
                      jnp.concatenate([u_mat, vh], axis=0).astype(bf16))
        p_col = jnp.sum(jnp.where(ti == si, jnp.broadcast_to(p_last[:, sl], (N, N)), 0.0),
                        axis=1, keepdims=True)
        h_scr[h] = p_col * h_state + upd


def _rwkv_post_kernel(o_ref, bonus_ref, g_ref, lnw_ref, lnb_ref, bd_ref, out_ref):
    y = o_ref[0] + o_ref[1]
    bd = bd_ref[...]
    mean = _head_sum(y, bd) * (1.0 / HEAD_DIM)
    yc = y - mean
    var = _head_sum(yc * yc, bd) * (1.0 / HEAD_DIM)
    yn = yc * lax.rsqrt(var + RWKV_GN_EPS) * lnw_ref[...] + lnb_ref[...]
    out_ref[...] = (yn + bonus_ref[...]) * g_ref[...]


def _rwkv_mixer(h_rwkv, mu, w0, w_up, a0, a_up, g_up, k_k, k_a, r_k, ln_w, ln_b, batch, seq):
    t = h_rwkv.shape[0]
    gw = GROUP_WIDTH
    tm = min(256, seq)
    per_batch = seq // tm
    n_halo = t // 8
    wup = jnp.zeros((2 * RWKV_LORA, 2 * gw), f32)
    aup = jnp.zeros((2 * RWKV_LORA, 2 * gw), f32)
    for e in range(2):
        wup = wup.at[e * RWKV_LORA:(e + 1) * RWKV_LORA, e * gw:(e + 1) * gw].set(w_up[e])
        aup = aup.at[e * RWKV_LORA:(e + 1) * RWKV_LORA, e * gw:(e + 1) * gw].set(a_up[e])
    head_id = jnp.arange(gw) // HEAD_DIM
    bd = (head_id[:, None] == head_id[None, :]).astype(bf16)
    vec = lambda a: a.reshape(1, -1).astype(f32)
    const = lambda a: pl.BlockSpec(a.shape, lambda i: (0,) * a.ndim)
    consts = [vec(mu), w0.astype(f32), wup.astype(bf16), a0.astype(f32), aup.astype(bf16),
              g_up.astype(bf16), vec(k_k), vec(k_a), vec(r_k), bd]
    row = pl.BlockSpec((tm, gw), lambda i: (i, 0))
    row2 = pl.BlockSpec((2, tm, gw), lambda i: (0, i, 0))
    one = jax.ShapeDtypeStruct((t, gw), f32)
    two = jax.ShapeDtypeStruct((2, t, gw), f32)
    r, v, an, g, bonus, lw, kd, bdir = pl.pallas_call(
        functools.partial(_rwkv_prep_kernel, per_batch=per_batch), grid=(t // tm,),
        in_specs=[pl.BlockSpec((tm, RWKV_COLS), lambda i: (i, 0)),
                  pl.BlockSpec((8, RWKV_COLS), lambda i: (jnp.maximum(i * (tm // 8) - 1, 0), 0)),
                  pl.BlockSpec((8, RWKV_COLS), lambda i: (jnp.minimum((i + 1) * (tm // 8), n_halo - 1), 0)),
                  ] + [const(a) for a in consts],
        out_specs=[row, row, row, row, row, row2, row2, row2],
        out_shape=[one, one, one, one, one, two, two, two],
        compiler_params=_cparams("parallel"), name="rwkv_prep",
    )(h_rwkv, h_rwkv, h_rwkv, *consts)

    L = RWKV_CHUNK
    nch = seq // L
    chunk_of = lambda b, e, c: b * nch + c + e * (nch - 1 - 2 * c)
    shared = pl.BlockSpec((L, gw), lambda b, e, c: (chunk_of(b, e, c), 0))
    per_dir = pl.BlockSpec((1, L, gw), lambda b, e, c: (e, chunk_of(b, e, c), 0))
    o = pl.pallas_call(
        _rwkv_chunk_kernel, grid=(batch, 2, nch),
        in_specs=[shared, shared, shared, per_dir, per_dir, per_dir],
        out_specs=per_dir,
        out_shape=two,
        scratch_shapes=[pltpu.VMEM((RWKV_HEADS, HEAD_DIM, HEAD_DIM), f32)],
        compiler_params=_cparams("parallel", "parallel", "arbitrary"), name="rwkv_chunk",
    )(r, v, an, lw, kd, bdir)

    tp = min(512, seq)
    rowp = pl.BlockSpec((tp, gw), lambda i: (i, 0))
    cvec = pl.BlockSpec((1, gw), lambda i: (0, 0))
    return pl.pallas_call(
        _rwkv_post_kernel, grid=(t // tp,),
        in_specs=[pl.BlockSpec((2, tp, gw), lambda i: (0, i, 0)), rowp, rowp, cvec, cvec,
                  pl.BlockSpec((gw, gw), lambda i: (0, 0))],
        out_specs=rowp, out_shape=one,
        compiler_params=_cparams("parallel"), name="rwkv_post",
    )(o, bonus, g, vec(ln_w), vec(ln_b), bd)


def _s5_operators(a_re, a_im, log_dt, b_re, b_im, c_re, c_im):
    L = S5_CHUNK
    lam_r, lam_i = a_re.astype(f32), a_im.astype(f32)
    dt = jnp.exp(log_dt.astype(f32))[..., None]
    mag = jnp.exp(lam_r * dt)
    abar_r, abar_i = mag * jnp.cos(lam_i * dt), mag * jnp.sin(lam_i * dt)
    den = lam_r * lam_r + lam_i * lam_i
    nr, ni = abar_r - 1.0, abar_i
    coef_r = (nr * lam_r + ni * lam_i) / den
    coef_i = (ni * lam_r - nr * lam_i) / den
    br, bi = b_re.astype(f32)[None], b_im.astype(f32)[None]
    bb_r = coef_r[..., None] * br - coef_i[..., None] * bi
    bb_i = coef_r[..., None] * bi + coef_i[..., None] * br
    steps = jnp.arange(L + 1, dtype=f32)[:, None, None, None]
    pmag = jnp.exp(steps * (lam_r * dt)[None])
    pw_r = pmag * jnp.cos(steps * (lam_i * dt)[None])
    pw_i = pmag * jnp.sin(steps * (lam_i * dt)[None])
    cr, ci = c_re.astype(f32)[None, None], c_im.astype(f32)[None, None]
    cp_r = cr * pw_r[:, :, :, None, :] - ci * pw_i[:, :, :, None, :]
    cp_i = cr * pw_i[:, :, :, None, :] + ci * pw_r[:, :, :, None, :]
    kern = (jnp.einsum('kegop,egpi->kegoi', cp_r, bb_r) - jnp.einsum('kegop,egpi->kegoi', cp_i, bb_i))
    idx = jnp.arange(L)
    lag = idx[None, :] - idx[:, None]
    kf = kern[jnp.clip(lag, 0, L), 0] * (lag >= 0)[:, :, None, None, None].astype(f32)
    kb = kern[jnp.clip(-lag, 0, L), 1] * (lag <= 0)[:, :, None, None, None].astype(f32)
    m_op = jnp.transpose(kf + kb, (2, 0, 4, 1, 3))
    g_, h_ = S5_GROUPS, S5_GROUP
    m_op = m_op.reshape(g_, L * h_, L * h_)

    def state_in(pr, pi, e):
        re = pr[..., None] * bb_r[e][None] - pi[..., None] * bb_i[e][None]
        im = pr[..., None] * bb_i[e][None] + pi[..., None] * bb_r[e][None]
        return jnp.transpose(re, (1, 0, 3, 2)), jnp.transpose(im, (1, 0, 3, 2))

    f_re, f_im = state_in(pw_r[L - 1 - idx, 0], pw_i[L - 1 - idx, 0], 0)
    b_re_, b_im_ = state_in(pw_r[idx, 1], pw_i[idx, 1], 1)
    b_sum = jnp.concatenate([f_re, f_im, b_re_, b_im_], axis=-1).reshape(g_, L * h_, 4 * S5_STATE)

    def state_out(k_idx, e):
        re = jnp.transpose(cp_r[k_idx, e], (1, 3, 0, 2))
        im = jnp.transpose(cp_i[k_idx, e], (1, 3, 0, 2))
        return re, -im

    fo_r, fo_i = state_out(idx + 1, 0)
    bo_r, bo_i = state_out(L - idx, 1)
    c_out = jnp.concatenate([fo_r, fo_i, bo_r, bo_i], axis=1).reshape(g_, 4 * S5_STATE, L * h_)
    a_pow = jnp.stack([jnp.stack([pw_r[L, e], pw_i[L, e]]) for e in range(2)])
    return m_op.astype(bf16), b_sum.astype(bf16), c_out.astype(bf16), a_pow


def _s5_state_in_kernel(u_ref, bsum_ref, s_ref):
    s_ref[0] = _dot(u_ref[0], bsum_ref[0])


def _s5_carry_kernel(sf_r, sf_i, sb_r, sb_i, af_r, af_i, ab_r, ab_i, xf_r, xf_i, xb_r, xb_i, *, nch):
    rows, p = af_r.shape
    zero = jnp.zeros((rows, p), f32)

    def body(i, carry):
        fr, fi, br, bi = carry
        j = nch - 1 - i
        xf_r[i] = fr
        xf_i[i] = fi
        xb_r[j] = br
        xb_i[j] = bi
        ar, ai = af_r[...], af_i[...]
        nfr = ar * fr - ai * fi + sf_r[i]
        nfi = ar * fi + ai * fr + sf_i[i]
        ar, ai = ab_r[...], ab_i[...]
        nbr = ar * br - ai * bi + sb_r[j]
        nbi = ar * bi + ai * br + sb_i[j]
        return nfr, nfi, nbr, nbi

    lax.fori_loop(0, nch, body, (zero, zero, zero, zero))


def _s5_out_kernel(u_ref, x_ref, m_ref, cout_ref, y_ref):
    y_ref[0] = _dot(u_ref[0], m_ref[0]) + _dot(x_ref[0], cout_ref[0])


def _s5_post_kernel(y_ref, u_ref, d_ref, w_ref, b_ref, o_ref):
    u = u_ref[...]
    z = y_ref[...] + d_ref[...] * u
    c0 = math.sqrt(2.0 / math.pi)
    z = 0.5 * z * (1.0 + jnp.tanh(c0 * (z + 0.044715 * (z * z * z))))
    gate = _dot(z.astype(bf16), w_ref[...]) + b_ref[...]
    o_ref[...] = z * _sigmoid(gate)


def _s5_mixer(u, a_re, a_im, log_dt, b_re, b_im, c_re, c_im, d_skip, glu_w, glu_b, batch, seq):
    t = u.shape[0]
    L, g_, h_, p = S5_CHUNK, S5_GROUPS, S5_GROUP, S5_STATE
    nch = seq // L
    rows = batch * nch
    m_op, b_sum, c_out, a_pow = _s5_operators(a_re, a_im, log_dt, b_re, b_im, c_re, c_im)
    ug = jnp.transpose(u.reshape(rows, L, g_, h_), (2, 0, 1, 3)).reshape(g_, rows, L * h_).astype(bf16)
    grp = lambda r, c: pl.BlockSpec((1, r, c), lambda g: (g, 0, 0))
    s_in = pl.pallas_call(
        _s5_state_in_kernel, grid=(g_,),
        in_specs=[grp(rows, L * h_), grp(L * h_, 4 * p)], out_specs=grp(rows, 4 * p),
        out_shape=jax.ShapeDtypeStruct((g_, rows, 4 * p), f32),
        compiler_params=_cparams("parallel"), name="s5_state_in",
    )(ug, b_sum)
    s_in = jnp.transpose(s_in.reshape(g_, batch, nch, 4, p), (3, 2, 1, 0, 4)).reshape(4, nch, batch * g_, p)
    coef = jnp.broadcast_to(a_pow[:, :, None], (2, 2, batch, g_, p)).reshape(2, 2, batch * g_, p)
    st = jax.ShapeDtypeStruct((nch, batch * g_, p), f32)
    rb = 8
    seq_spec = pl.BlockSpec((nch, rb, p), lambda i: (0, i, 0))
    coef_spec = pl.BlockSpec((rb, p), lambda i: (i, 0))
    xs = pl.pallas_call(
        functools.partial(_s5_carry_kernel, nch=nch), grid=(batch * g_ // rb,),
        in_specs=[seq_spec] * 4 + [coef_spec] * 4, out_specs=[seq_spec] * 4, out_shape=[st] * 4,
        compiler_params=_cparams("parallel"), name="s5_carry",
    )(s_in[0], s_in[1], s_in[2], s_in[3], coef[0, 0], coef[0, 1], coef[1, 0], coef[1, 1])
    x_cat = jnp.stack(xs).reshape(4, nch, batch, g_, p)
    x_cat = jnp.transpose(x_cat, (3, 2, 1, 0, 4)).reshape(g_, rows, 4 * p).astype(bf16)
    y = pl.pallas_call(
        _s5_out_kernel, grid=(g_,),
        in_specs=[grp(rows, L * h_), grp(rows, 4 * p), grp(L * h_, L * h_), grp(4 * p, L * h_)],
        out_specs=grp(rows, L * h_),
        out_shape=jax.ShapeDtypeStruct((g_, rows, L * h_), f32),
        compiler_params=_cparams("parallel"), name="s5_out",
    )(ug, x_cat, m_op, c_out)
    y = jnp.transpose(y.reshape(g_, rows, L, h_), (1, 2, 0, 3)).reshape(t, g_ * h_)
    tm = min(512, seq)
    w = g_ * h_
    row = pl.BlockSpec((tm, w), lambda i: (i, 0))
    cvec = pl.BlockSpec((1, w), lambda i: (0, 0))
    return pl.pallas_call(
        _s5_post_kernel, grid=(t // tm,),
        in_specs=[row, row, cvec, pl.BlockSpec((w, w), lambda i: (0, 0)), cvec],
        out_specs=row, out_shape=jax.ShapeDtypeStruct((t, w), f32),
        compiler_params=_cparams("parallel"), name="s5_post",
    )(y, u, d_skip.reshape(1, w).astype(f32), glu_w.astype(bf16), glu_b.reshape(1, w).astype(f32))


def _outproj_kernel(o0_ref, o1_ref, o2_ref, o3_ref, w_ref, x_ref, g_ref, out_ref):
    acc = None
    for i, o_ref in enumerate((o0_ref, o1_ref, o2_ref, o3_ref)):
        part = _dot(o_ref[...].astype(bf16), w_ref[i * GROUP_WIDTH:(i + 1) * GROUP_WIDTH, :])
        acc = part if acc is None else acc + part
    out_ref[...] = x_ref[...] + g_ref[0] * acc


def _outproj(outs, w_out, x, gate, seq):
    t, d = x.shape
    tm = min(512, seq)
    tn = 1024
    per_batch = seq // tm
    part = pl.BlockSpec((tm, GROUP_WIDTH), lambda i, j: (i, 0))
    return pl.pallas_call(
        _outproj_kernel, grid=(t // tm, d // tn),
        in_specs=[part, part, part, part,
                  pl.BlockSpec((4 * GROUP_WIDTH, tn), lambda i, j: (0, j)),
                  pl.BlockSpec((tm, tn), lambda i, j: (i, j)),
                  pl.BlockSpec((1, 1, tn), lambda i, j: (i // per_batch, 0, j))],
        out_specs=pl.BlockSpec((tm, tn), lambda i, j: (i, j)),
        out_shape=jax.ShapeDtypeStruct((t, d), f32),
        compiler_params=_cparams("parallel", "parallel"), name="out_proj",
    )(*outs, w_out.astype(bf16), x, gate)


def _moe_ffn_kernel(xs_ref, gate_ref, w1_ref, w3_ref, w2_ref, o_ref):
    f = pl.program_id(2)
    xs = xs_ref[0, 0]
    a = _dot(xs, w1_ref[0].astype(bf16))
    b = _dot(xs, w3_ref[0].astype(bf16))
    hid = (a * _sigmoid(a) * b).astype(bf16)
    part = _dot(hid, w2_ref[0].astype(bf16)) * gate_ref[0, 0]

    @pl.when(f == 0)
    def _():
        o_ref[0, 0] = part

    @pl.when(f > 0)
    def _():
        o_ref[0, 0] += part


def _moe_ffn(xs, gate, w1, w3, w2):
    batch, n_exp, cap, d = xs.shape
    ff = w1.shape[2]
    tf = 256
    return pl.pallas_call(
        _moe_ffn_kernel, grid=(n_exp, batch, ff // tf),
        in_specs=[pl.BlockSpec((1, 1, cap, d), lambda e, b, f: (b, e, 0, 0)),
                  pl.BlockSpec((1, 1, cap, 1), lambda e, b, f: (b, e, 0, 0)),
                  pl.BlockSpec((1, d, tf), lambda e, b, f: (e, 0, f)),
                  pl.BlockSpec((1, d, tf), lambda e, b, f: (e, 0, f)),
                  pl.BlockSpec((1, tf, d), lambda e, b, f: (e, f, 0))],
        out_specs=pl.BlockSpec((1, 1, cap, d), lambda e, b, f: (b, e, 0, 0)),
        out_shape=jax.ShapeDtypeStruct((batch, n_exp, cap, d), f32),
        compiler_params=_cparams("parallel", "parallel", "arbitrary"), name="moe_ffn",
    )(xs, gate, w1, w3, w2)


def _moe(xn, logits, w1, w3, w2, batch, seq):
    d = xn.shape[1]
    capacity = EC_CAPACITY_FACTOR * seq // N_EXPERTS
    aff = jax.nn.softmax(logits.reshape(batch, seq, N_EXPERTS), axis=-1)
    gate, idx = lax.top_k(jnp.swapaxes(aff, 1, 2), capacity)
    bidx = jnp.arange(batch)[:, None, None]
    xs = xn.reshape(batch, seq, d)[bidx, idx]
    ys = _moe_ffn(xs, gate[..., None], w1, w3, w2)
    return jnp.zeros((batch, seq, d), f32).at[bidx, idx].add(ys).reshape(batch * seq, d)


def _in_proj_weights(w_in_l):
    o1, o2, o3 = SWA_COLS, SWA_COLS + MLA_COLS, SWA_COLS + MLA_COLS + RWKV_COLS
    w_mla = jnp.zeros((w_in_l.shape[0], MLA_COLS_PAD), f32).at[:, :MLA_COLS].set(w_in_l[:, o1:o2])
    return (w_in_l[:, :o1].astype(bf16), w_mla.astype(bf16),
            w_in_l[:, o2:o3].astype(bf16), w_in_l[:, o3:].astype(bf16))


def kernel(x, c, positions, ada_w, ada_b, norm1_g, norm2_g, w_in, w_out, swa_q_gain, swa_k_gain, swa_sink, mla_q_a_gain, mla_kv_a_gain, mla_w_uq, mla_w_ukv, mla_q_gain, mla_k_gain, rwkv_mu, rwkv_w0, rwkv_w_up, rwkv_a0, rwkv_a_up, rwkv_g_up, rwkv_k_k, rwkv_k_a, rwkv_r_k, rwkv_ln_w, rwkv_ln_b, s5_a_re, s5_a_im, s5_log_dt, s5_b_re, s5_b_im, s5_c_re, s5_c_im, s5_d, s5_glu_w, s5_glu_b, router_w, moe_w1, moe_w3, moe_w2):
    batch, seq, d = x.shape
    depth = ada_w.shape[0]
    t = batch * seq
    cos_h, sin_h, cos_r, sin_r = _rope_tables(positions)
    mod = _ada_mod(c, ada_w, ada_b)
    xf = x.reshape(t, d)
    for l in range(depth):
        sh1, sc1, g1, sh2, sc2, g2 = [m[:, None, :] for m in jnp.split(mod[l], 6, axis=-1)]
        xn = _norm_call(xf, norm1_g[l], sc1, sh1, seq)
        w_swa, w_mla, w_rwkv, w_s5 = _in_proj_weights(w_in[l])
        h_swa = _matmul(xn, w_swa, tm=1024, tn=SWA_COLS, name="in_proj_swa")
        h_mla = _matmul(xn, w_mla, tm=1024, tn=MLA_COLS_PAD, name="in_proj_mla")
        h_rwkv = _matmul(xn, w_rwkv, tm=1024, tn=640, name="in_proj_rwkv")
        h_s5 = _matmul(xn, w_s5, tm=1024, tn=GROUP_WIDTH, name="in_proj_s5")
        o_swa = _swa_mixer(h_swa, cos_h, sin_h, swa_q_gain[l], swa_k_gain[l], swa_sink[l], batch, seq)
        o_mla = _mla_mixer(h_mla, cos_r, sin_r, mla_q_a_gain[l], mla_kv_a_gain[l], mla_w_uq[l],
                           mla_w_ukv[l], mla_q_gain[l], mla_k_gain[l], batch, seq)
        o_rwkv = _rwkv_mixer(h_rwkv, rwkv_mu[l], rwkv_w0[l], rwkv_w_up[l], rwkv_a0[l], rwkv_a_up[l],
                             rwkv_g_up[l], rwkv_k_k[l], rwkv_k_a[l], rwkv_r_k[l],
                             rwkv_ln_w[l], rwkv_ln_b[l], batch, seq)
        o_s5 = _s5_mixer(h_s5, s5_a_re[l], s5_a_im[l], s5_log_dt[l], s5_b_re[l], s5_b_im[l],
                         s5_c_re[l], s5_c_im[l], s5_d[l], s5_glu_w[l], s5_glu_b[l], batch, seq)
        xf = _outproj((o_swa, o_mla, o_rwkv, o_s5), w_out[l], xf, g1, seq)
        xn2, logits = _norm_call(xf, norm2_g[l], sc2, sh2, seq, router_w=router_w[l])
        moe = _moe(xn2, logits, moe_w1[l], moe_w3[l], moe_w2[l], batch, seq)
        xf = xf + jnp.broadcast_to(g2, (batch, seq, d)).reshape(t, d) * moe
    return xf.reshape(batch, seq, d)
```

```python
import functools
import math

import jax
import jax.numpy as jnp
from jax import lax
from jax.experimental import pallas as pl
from jax.experimental.pallas import tpu as pltpu

f32 = jnp.float32
bf16 = jnp.bfloat16

D_MODEL = 2048
HEAD_DIM = 64
GROUP_WIDTH = 512
ROPE_THETA = 10000.0
NORM_EPS = 1e-6
NEG_INF = -1e30

SWA_HEADS = 8
SWA_KV_HEADS = 2
SWA_WINDOW = 128
SWA_BLOCK = 128
SWA_Q = 512
SWA_KV = 128
SWA_COLS = 768

MLA_HEADS = 8
MLA_NOPE = 64
MLA_ROPE = 32
MLA_V = 64
MLA_QK = 96
MLA_Q_RANK = 384
MLA_KV_RANK = 128
MLA_COLS = 544
MLA_COLS_PAD = 640
MLA_HEAD_PAD = 128

RWKV_HEADS = 8
RWKV_LORA = 64
RWKV_GATE_LORA = 128
RWKV_GN_EPS = 64e-5
RWKV_COLS = 1920
RWKV_CHUNK = 64
RWKV_CHUNKS_PER_STEP = 2
RWKV_GROUP_LANES = 256

S5_GROUP = 16
S5_GROUPS = 32
S5_STATE = 64
S5_CHUNK = 32

N_EXPERTS = 16
EC_CAPACITY_FACTOR = 2
D_FF_EXPERT = 1024

VMEM_LIMIT_BYTES = 52 * 1024 * 1024
LANES = 128


def _cparams(*sem):
    return pltpu.CompilerParams(dimension_semantics=sem, vmem_limit_bytes=VMEM_LIMIT_BYTES)


def _sigmoid(x):
    return 1.0 / (1.0 + jnp.exp(-x))


def _split_bf16(x):
    hi = x.astype(bf16)
    lo = (x - hi.astype(f32)).astype(bf16)
    return hi, lo


def _dot(a, b):
    return jnp.dot(a, b, preferred_element_type=f32)


def _dot_nt(a, b):
    return lax.dot_general(a, b, (((1,), (1,)), ((), ())), preferred_element_type=f32)


def _dot_tn(a, b):
    return lax.dot_general(a, b, (((0,), (0,)), ((), ())), preferred_element_type=f32)


def _ada_kernel(c_ref, w_ref, b_ref, o_ref):
    c = c_ref[...]
    cond = (c * _sigmoid(c)).astype(bf16)
    o_ref[0] = _dot(cond, w_ref[0].astype(bf16)) + b_ref[0]


def _ada_mod(c, ada_w, ada_b):
    depth, d, n = ada_w.shape
    b = c.shape[0]
    rows = 8
    c_pad = jnp.zeros((rows, d), f32).at[:b].set(c)
    tn = 512
    out = pl.pallas_call(
        _ada_kernel,
        grid=(depth, n // tn),
        in_specs=[
            pl.BlockSpec((rows, d), lambda l, j: (0, 0)),
            pl.BlockSpec((1, d, tn), lambda l, j: (l, 0, j)),
            pl.BlockSpec((1, 1, tn), lambda l, j: (l, 0, j)),
        ],
        out_specs=pl.BlockSpec((1, rows, tn), lambda l, j: (l, 0, j)),
        out_shape=jax.ShapeDtypeStruct((depth, rows, n), f32),
        compiler_params=_cparams("parallel", "parallel"),
        name="ada_mod",
    )(c_pad, ada_w, ada_b.reshape(depth, 1, n))
    return out[:, :b]


def _mm_kernel(x_ref, w_ref, o_ref):
    o_ref[...] = _dot(x_ref[...].astype(bf16), w_ref[...].astype(bf16)).astype(o_ref.dtype)


def _matmul(x, w, *, tm, tn, name):
    m, k = x.shape
    n = w.shape[1]
    tm = min(tm, m)
    return pl.pallas_call(
        _mm_kernel,
        grid=(m // tm, n // tn),
        in_specs=[pl.BlockSpec((tm, k), lambda i, j: (i, 0)),
                  pl.BlockSpec((k, tn), lambda i, j: (0, j))],
        out_specs=pl.BlockSpec((tm, tn), lambda i, j: (i, j)),
        out_shape=jax.ShapeDtypeStruct((m, n), f32),
        compiler_params=_cparams("parallel", "parallel"),
        name=name,
    )(x, w)


def _norm_mod(x_ref, g_ref, sc_ref, sh_ref):
    x = x_ref[...]
    ms = jnp.mean(x * x, axis=-1, keepdims=True)
    y = x * lax.rsqrt(ms + NORM_EPS) * g_ref[...]
    return y * (1.0 + sc_ref[0]) + sh_ref[0]


def _norm_kernel(x_ref, g_ref, sc_ref, sh_ref, o_ref):
    o_ref[...] = _norm_mod(x_ref, g_ref, sc_ref, sh_ref).astype(o_ref.dtype)


def _norm_router_kernel(x_ref, g_ref, sc_ref, sh_ref, whi_ref, wlo_ref, o_ref, logit_ref):
    y = _norm_mod(x_ref, g_ref, sc_ref, sh_ref)
    o_ref[...] = y.astype(o_ref.dtype)
    yhi, ylo = _split_bf16(y)
    whi = whi_ref[...]
    logit_ref[...] = _dot(yhi, whi) + _dot(ylo, whi) + _dot(yhi, wlo_ref[...])


def _norm_call(x, gain, scale, shift, seq, router_w=None):
    t, d = x.shape
    tm = min(512, seq)
    per_batch = seq // tm
    row = pl.BlockSpec((tm, d), lambda i: (i, 0))
    mod = pl.BlockSpec((1, 1, d), lambda i: (i // per_batch, 0, 0))
    in_specs = [row, pl.BlockSpec((1, d), lambda i: (0, 0)), mod, mod]
    args = [x, gain.reshape(1, d), scale, shift]
    if router_w is None:
        return pl.pallas_call(
            _norm_kernel, grid=(t // tm,), in_specs=in_specs, out_specs=row,
            out_shape=jax.ShapeDtypeStruct((t, d), bf16),
            compiler_params=_cparams("parallel"), name="norm_mod",
        )(*args)
    e = router_w.shape[1]
    w_pad = jnp.zeros((d, LANES), f32).at[:, :e].set(router_w)
    whi, wlo = _split_bf16(w_pad)
    wspec = pl.BlockSpec((d, LANES), lambda i: (0, 0))
    xn, logits = pl.pallas_call(
        _norm_router_kernel, grid=(t // tm,), in_specs=in_specs + [wspec, wspec],
        out_specs=[row, pl.BlockSpec((tm, LANES), lambda i: (i, 0))],
        out_shape=[jax.ShapeDtypeStruct((t, d), bf16), jax.ShapeDtypeStruct((t, LANES), f32)],
        compiler_params=_cparams("parallel"), name="norm_mod_router",
    )(*args, whi, wlo)
    return xn, logits[:, :e]


def _rope_lanes(x, cos_t, sin_t, half, first_mask):
    n = x.shape[-1]
    fwd = pltpu.roll(x, n - half, axis=1)
    bwd = pltpu.roll(x, half, axis=1)
    return x * cos_t + jnp.where(first_mask, fwd, bwd) * sin_t


def _rope_tables(positions):
    t = positions.size
    pos = positions.reshape(t, 1).astype(f32)

    def tables(dim):
        inv_freq = ROPE_THETA ** (-jnp.arange(0, dim, 2, dtype=f32) / dim)
        ang = pos * inv_freq
        return jnp.cos(ang), jnp.sin(ang)

    c, s = tables(HEAD_DIM)
    cos_h = jnp.concatenate([c, c, c, c], axis=-1)
    sin_h = jnp.concatenate([-s, s, -s, s], axis=-1)
    c, s = tables(MLA_ROPE)
    one = jnp.ones((t, MLA_NOPE), f32)
    zero = jnp.zeros((t, MLA_NOPE), f32)
    pad1 = jnp.ones((t, MLA_HEAD_PAD - MLA_QK), f32)
    pad0 = jnp.zeros((t, MLA_HEAD_PAD - MLA_QK), f32)
    cos_r = jnp.concatenate([one, c, c, pad1], axis=-1)
    sin_r = jnp.concatenate([zero, -s, s, pad0], axis=-1)
    return cos_h, sin_h, cos_r, sin_r


def _pair_rmsnorm(x, gain):
    lane = lax.broadcasted_iota(jnp.int32, x.shape, 1)
    lo = lane < HEAD_DIM
    x2 = x * x
    s_lo = jnp.sum(jnp.where(lo, x2, 0.0), axis=-1, keepdims=True)
    s_hi = jnp.sum(jnp.where(lo, 0.0, x2), axis=-1, keepdims=True)
    ms = jnp.where(lo, s_lo, s_hi) * (1.0 / HEAD_DIM)
    return x * lax.rsqrt(ms + NORM_EPS) * gain


def _swa_prep_kernel(h_ref, cos_ref, sin_ref, qg_ref, kg_ref, q_ref, k_ref, v_ref):
    cos_t = cos_ref[...]
    sin_t = sin_ref[...]
    lane = lax.broadcasted_iota(jnp.int32, cos_t.shape, 1)
    first = (lane % HEAD_DIM) < (HEAD_DIM // 2)
    scale = HEAD_DIM ** -0.5
    for p in range(SWA_Q // LANES):
        x = _pair_rmsnorm(h_ref[:, p * LANES:(p + 1) * LANES], qg_ref[...])
        x = _rope_lanes(x, cos_t, sin_t, HEAD_DIM // 2, first)
        q_ref[:, p * LANES:(p + 1) * LANES] = (x * scale).astype(bf16)
    x = _pair_rmsnorm(h_ref[:, SWA_Q:SWA_Q + SWA_KV], kg_ref[...])
    k_ref[...] = _rope_lanes(x, cos_t, sin_t, HEAD_DIM // 2, first).astype(bf16)
    v_ref[...] = h_ref[:, SWA_Q + SWA_KV:SWA_COLS].astype(bf16)


def _swa_attn_kernel(q_ref, k_ref, v_ref, sink_ref, o_ref, *, seq):
    n = pl.program_id(1)
    band = 3 * SWA_BLOCK
    start = pl.multiple_of(jnp.clip((n - 1) * SWA_BLOCK, 0, seq - band), SWA_BLOCK)
    kb = k_ref[pl.ds(start, band), :]
    vb = v_ref[pl.ds(start, band), :]
    q = q_ref[...]
    grp = SWA_HEADS // SWA_KV_HEADS
    rows = grp * SWA_BLOCK
    qpos = n * SWA_BLOCK + lax.broadcasted_iota(jnp.int32, (rows, band), 0) % SWA_BLOCK
    kpos = start + lax.broadcasted_iota(jnp.int32, (rows, band), 1)
    valid = jnp.abs(qpos - kpos) <= SWA_WINDOW
    outs = []
    for kh in range(SWA_KV_HEADS):
        qs = jnp.concatenate(
            [q[:, (kh * grp + g) * HEAD_DIM:(kh * grp + g + 1) * HEAD_DIM] for g in range(grp)], axis=0)
        k_h = kb[:, kh * HEAD_DIM:(kh + 1) * HEAD_DIM]
        v_h = vb[:, kh * HEAD_DIM:(kh + 1) * HEAD_DIM]
        s = jnp.where(valid, _dot_nt(qs, k_h), NEG_INF)
        sink = jnp.concatenate(
            [jnp.broadcast_to(sink_ref[kh * grp + g:kh * grp + g + 1, 0:1], (SWA_BLOCK, 1))
             for g in range(grp)], axis=0)
        m = jnp.maximum(jnp.max(s, axis=-1, keepdims=True), sink)
        p = jnp.exp(s - m)
        denom = jnp.sum(p, axis=-1, keepdims=True) + jnp.exp(sink - m)
        o = _dot(p.astype(bf16), v_h) / denom
        for g in range(grp):
            outs.append(o[g * SWA_BLOCK:(g + 1) * SWA_BLOCK])
    o_ref[...] = jnp.concatenate(outs, axis=-1)


def _swa_mixer(h_swa, cos_h, sin_h, q_gain, k_gain, sink, batch, seq):
    t = h_swa.shape[0]
    tm = min(512, seq)
    row = lambda w: pl.BlockSpec((tm, w), lambda i: (i, 0))
    const = pl.BlockSpec((1, LANES), lambda i: (0, 0))
    q, k, v = pl.pallas_call(
        _swa_prep_kernel, grid=(t // tm,),
        in_specs=[row(SWA_COLS), row(LANES), row(LANES), const, const],
        out_specs=[row(SWA_Q), row(SWA_KV), row(SWA_KV)],
        out_shape=[jax.ShapeDtypeStruct((t, SWA_Q), bf16),
                   jax.ShapeDtypeStruct((t, SWA_KV), bf16),
                   jax.ShapeDtypeStruct((t, SWA_KV), bf16)],
        compiler_params=_cparams("parallel"), name="swa_prep",
    )(h_swa, cos_h, sin_h, jnp.tile(q_gain, 2).reshape(1, LANES), jnp.tile(k_gain, 2).reshape(1, LANES))
    nb = seq // SWA_BLOCK
    sink_t = jnp.broadcast_to(sink.astype(f32).reshape(SWA_HEADS, 1), (SWA_HEADS, LANES))
    return pl.pallas_call(
        functools.partial(_swa_attn_kernel, seq=seq), grid=(batch, nb),
        in_specs=[pl.BlockSpec((SWA_BLOCK, SWA_Q), lambda b, n: (b * nb + n, 0)),
                  pl.BlockSpec((seq, SWA_KV), lambda b, n: (b, 0)),
                  pl.BlockSpec((seq, SWA_KV), lambda b, n: (b, 0)),
                  pl.BlockSpec((SWA_HEADS, LANES), lambda b, n: (0, 0))],
        out_specs=pl.BlockSpec((SWA_BLOCK, SWA_Q), lambda b, n: (b * nb + n, 0)),
        out_shape=jax.ShapeDtypeStruct((t, SWA_Q), f32),
        compiler_params=_cparams("parallel", "parallel"), name="swa_attn",
    )(q, k, v, sink_t)


def _mla_prep_kernel(h_ref, cos_ref, sin_ref, qag_ref, kvag_ref, wq_ref, wk_ref, wpe_ref, wv_ref,
                     qg_ref, kg_ref, q_ref, k_ref, v_ref):
    cos_t = cos_ref[...]
    sin_t = sin_ref[...]
    lane = lax.broadcasted_iota(jnp.int32, cos_t.shape, 1)
    first = lane < MLA_NOPE + MLA_ROPE // 2
    cq = h_ref[:, :MLA_Q_RANK]
    cq = cq * lax.rsqrt(jnp.mean(cq * cq, axis=-1, keepdims=True) + NORM_EPS) * qag_ref[...]
    ckv = h_ref[:, MLA_Q_RANK:MLA_Q_RANK + MLA_KV_RANK]
    ckv = (ckv * lax.rsqrt(jnp.mean(ckv * ckv, axis=-1, keepdims=True) + NORM_EPS) * kvag_ref[...]).astype(bf16)
    kpe = h_ref[:, MLA_Q_RANK + MLA_KV_RANK:MLA_COLS_PAD].astype(bf16)
    q_all = _dot(cq.astype(bf16), wq_ref[...])
    k_all = _dot(ckv, wk_ref[...]) + _dot(kpe, wpe_ref[...])
    v_all = _dot(ckv, wv_ref[...])
    one_lane = lane == MLA_V
    scale = MLA_QK ** -0.5 * math.log2(math.e)

    def head_norm(x, gain):
        ms = jnp.sum(x * x, axis=-1, keepdims=True) * (1.0 / MLA_QK)
        x = x * lax.rsqrt(ms + NORM_EPS) * gain
        return _rope_lanes(x, cos_t, sin_t, MLA_ROPE // 2, first)

    for h in range(MLA_HEADS):
        sl = slice(h * MLA_HEAD_PAD, (h + 1) * MLA_HEAD_PAD)
        q_ref[0, h] = (head_norm(q_all[:, sl], qg_ref[...]) * scale).astype(bf16)
        k_ref[0, h] = head_norm(k_all[:, sl], kg_ref[...]).astype(bf16)
        v_ref[0, h] = jnp.where(one_lane, 1.0, v_all[:, sl]).astype(bf16)


def _mla_attn_kernel(q_ref, k_ref, v_ref, o_ref, *, tk, nk):
    outs = []
    for hh in range(2):
        q = q_ref[0, hh]
        tq = q.shape[0]

        def body(j, carry, hh=hh, q=q):
            m, acc = carry
            off = pl.multiple_of(j * tk, tk)
            kj = k_ref[0, hh, pl.ds(off, tk), :]
            vj = v_ref[0, hh, pl.ds(off, tk), :]
            s = _dot_nt(q, kj)
            m_new = jnp.maximum(m, jnp.max(s, axis=-1, keepdims=True))
            alpha = jnp.exp2(m - m_new)
            p = jnp.exp2(s - m_new)
            return m_new, acc * alpha + _dot(p.astype(bf16), vj)

        m0 = jnp.full((tq, 1), NEG_INF, f32)
        acc0 = jnp.zeros((tq, MLA_HEAD_PAD), f32)
        _, acc = lax.fori_loop(0, nk, body, (m0, acc0))
        outs.append(acc[:, :MLA_V] / acc[:, MLA_V:MLA_V + 1])
    o_ref[0] = jnp.concatenate(outs, axis=-1)


def _mla_weights(w_uq, w_ukv, q_gain, k_gain):
    hp = MLA_HEAD_PAD
    wq = jnp.zeros((MLA_Q_RANK, MLA_HEADS, hp), f32).at[:, :, :MLA_QK].set(
        w_uq.reshape(MLA_Q_RANK, MLA_HEADS, MLA_QK))
    kv = w_ukv.reshape(MLA_KV_RANK, MLA_HEADS, MLA_NOPE + MLA_V)
    wk = jnp.zeros((MLA_KV_RANK, MLA_HEADS, hp), f32).at[:, :, :MLA_NOPE].set(kv[:, :, :MLA_NOPE])
    wv = jnp.zeros((MLA_KV_RANK, MLA_HEADS, hp), f32).at[:, :, :MLA_V].set(kv[:, :, MLA_NOPE:])
    eye = jnp.eye(MLA_ROPE, dtype=f32)
    wpe = jnp.zeros((LANES, MLA_HEADS, hp), f32).at[:MLA_ROPE, :, MLA_NOPE:MLA_QK].set(
        jnp.broadcast_to(eye[:, None, :], (MLA_ROPE, MLA_HEADS, MLA_ROPE)))
    flat = lambda w: w.reshape(w.shape[0], MLA_HEADS * hp).astype(bf16)
    pad = lambda g: jnp.zeros((1, hp), f32).at[0, :MLA_QK].set(g)
    return flat(wq), flat(wk), flat(wpe), flat(wv), pad(q_gain), pad(k_gain)


def _mla_mixer(h_mla, cos_r, sin_r, q_a_gain, kv_a_gain, w_uq, w_ukv, q_gain, k_gain, batch, seq):
    t = h_mla.shape[0]
    tm = min(512, seq)
    per_batch = seq // tm
    hp = MLA_HEAD_PAD
    wq, wk, wpe, wv, qg, kg = _mla_weights(w_uq, w_ukv, q_gain, k_gain)
    row = lambda w: pl.BlockSpec((tm, w), lambda b, i: (b * per_batch + i, 0))
    const = lambda a: pl.BlockSpec(a.shape, lambda b, i: (0,) * a.ndim)
    head_out = pl.BlockSpec((1, MLA_HEADS, tm, hp), lambda b, i: (b, 0, i, 0))
    head_shape = jax.ShapeDtypeStruct((batch, MLA_HEADS, seq, hp), bf16)
    qag = q_a_gain.reshape(1, MLA_Q_RANK)
    kvag = kv_a_gain.reshape(1, MLA_KV_RANK)
    q, k, v = pl.pallas_call(
        _mla_prep_kernel, grid=(batch, per_batch),
        in_specs=[row(MLA_COLS_PAD), row(LANES), row(LANES), const(qag), const(kvag),
                  const(wq), const(wk), const(wpe), const(wv), const(qg), const(kg)],
        out_specs=[head_out, head_out, head_out],
        out_shape=[head_shape, head_shape, head_shape],
        compiler_params=_cparams("parallel", "parallel"), name="mla_prep",
    )(h_mla, cos_r, sin_r, qag, kvag, wq, wk, wpe, wv, qg, kg)
    tq = min(1024, seq)
    tk = min(1024, seq)
    kv_spec = pl.BlockSpec((1, 2, seq, hp), lambda b, h, i: (b, h, 0, 0))
    o = pl.pallas_call(
        functools.partial(_mla_attn_kernel, tk=tk, nk=seq // tk),
        grid=(batch, MLA_HEADS // 2, seq // tq),
        in_specs=[pl.BlockSpec((1, 2, tq, hp), lambda b, h, i: (b, h, i, 0)), kv_spec, kv_spec],
        out_specs=pl.BlockSpec((1, tq, 2 * MLA_V), lambda b, h, i: (b, i, h)),
        out_shape=jax.ShapeDtypeStruct((batch, seq, MLA_HEADS * MLA_V), f32),
        compiler_params=_cparams("parallel", "parallel", "parallel"), name="mla_attn",
    )(q, k, v)
    return o.reshape(t, MLA_HEADS * MLA_V)


def _head_sum(x, bd):
    hi, lo = _split_bf16(x)
    return _dot(hi, bd) + _dot(lo, bd)


def _rwkv_prep_kernel(h_ref, prev_ref, next_ref, mu_ref, w0_ref, wup_ref, a0_ref, aup_ref, gup_ref,
                      kk_ref, ka_ref, rk_ref, bd_ref,
                      r_ref, v_ref, an_ref, g_ref, bonus_ref, lw_ref, kd_ref, bdir_ref, *, per_batch):
    i = pl.program_id(0)
    x = h_ref[...]
    tm = x.shape[0]
    row = lax.broadcasted_iota(jnp.int32, x.shape, 0)
    prev_row = jnp.where(i % per_batch == 0, 0.0, prev_ref[7:8, :])
    next_row = jnp.where(i % per_batch == per_batch - 1, 0.0, next_ref[0:1, :])
    x_prev = jnp.where(row == 0, prev_row, pltpu.roll(x, 1, axis=0))
    x_next = jnp.where(row == tm - 1, next_row, pltpu.roll(x, tm - 1, axis=0))
    hs = x + mu_ref[...] * (0.5 * (x_prev + x_next) - x)
    gw = GROUP_WIDTH
    r = hs[:, 0:gw]
    k = hs[:, gw:2 * gw]
    v = hs[:, 2 * gw:3 * gw]
    wd = hs[:, 3 * gw:3 * gw + LANES]
    ad = hs[:, 3 * gw + LANES:3 * gw + 2 * LANES]
    gd = hs[:, 3 * gw + 2 * LANES:RWKV_COLS]
    bd = bd_ref[...]
    lora_w = _dot(jnp.tanh(wd).astype(bf16), wup_ref[...])
    lora_a = _dot(ad.astype(bf16), aup_ref[...])
    kk = k * kk_ref[...]
    kk = kk * lax.rsqrt(_head_sum(kk * kk, bd) + 1e-12)
    r_ref[...] = r
    v_ref[...] = v
    an_ref[...] = -kk
    g_ref[...] = _dot(_sigmoid(gd).astype(bf16), gup_ref[...])
    bonus_ref[...] = _head_sum(r * k * rk_ref[...], bd) * v
    for e in range(2):
        z = -(w0_ref[e:e + 1, :] + lora_w[:, e * gw:(e + 1) * gw])
        softplus = jnp.maximum(z, 0.0) + jnp.log(1.0 + jnp.exp(-jnp.abs(z)))
        lw_ref[e] = -jnp.exp(-softplus - 0.5)
        a = _sigmoid(a0_ref[e:e + 1, :] + lora_a[:, e * gw:(e + 1) * gw])
        kd_ref[e] = k * (1.0 + (a - 1.0) * ka_ref[...])
        bdir_ref[e] = kk * a


def _bmm(a, b):
    return lax.dot_general(a, b, (((2,), (1,)), ((0,), (0,))), preferred_element_type=f32)


def _bmm_nt(a, b):
    return lax.dot_general(a, b, (((2,), (2,)), ((0,), (0,))), preferred_element_type=f32)


def _bmm_tn(a, b):
    return lax.dot_general(a, b, (((1,), (1,)), ((0,), (0,))), preferred_element_type=f32)


def _rwkv_chunk_kernel(r_f, r_b, v_f, v_b, an_f, an_b, lw_f, lw_b, kd_f, kd_b, bd_f, bd_b,
                       o_f, o_b, h_scr, *, batch, ch):
    L = RWKV_CHUNK
    N = HEAD_DIM
    W = RWKV_GROUP_LANES
    hpg = W // N
    n_grp = GROUP_WIDTH // W

    @pl.when(pl.program_id(0) == 0)
    def _():
        h_scr[...] = jnp.zeros_like(h_scr)

    units = [(j, d, b, g) for j in range(ch) for d in range(2) for b in range(batch) for g in range(n_grp)]
    per_step = 2 * batch * n_grp
    n_units = len(units)

    def rows_of(j, d):
        return pl.ds((ch - 1 - j if d else j) * L, L)

    def load(ref_f, ref_b, lead):
        return jnp.stack([(ref_b if d else ref_f)[lead + (b, rows_of(j, d), pl.ds(g * W, W))]
                          for (j, d, b, g) in units])

    r = load(r_f, r_b, ())
    v = load(v_f, v_b, ())
    an = load(an_f, an_b, ())
    lw = load(lw_f, lw_b, (0,))
    kd = load(kd_f, kd_b, (0,))
    bdir = load(bd_f, bd_b, (0,))

    shape = (n_units, L, W)
    ui = lax.broadcasted_iota(jnp.int32, shape, 0)
    ti = lax.broadcasted_iota(jnp.int32, shape, 1)
    si = lax.broadcasted_iota(jnp.int32, shape, 2) % N
    sign = 1 - 2 * ((ui // (batch * n_grp)) % 2)
    ahead = (ti - si) * sign
    strict = ahead > 0
    incl = ahead >= 0
    bi = lax.broadcasted_iota(jnp.int32, (W, W), 0) // N
    bj = lax.broadcasted_iota(jnp.int32, (W, W), 1) // N
    diag_blocks = bi == bj

    def bdiag(x):
        return jnp.where(diag_blocks, jnp.concatenate([x] * hpg, axis=1), 0.0).astype(bf16)

    t2 = lax.broadcasted_iota(jnp.int32, (L, L), 0)
    s2 = lax.broadcasted_iota(jnp.int32, (L, L), 1)
    cum_parts = [None] * n_units
    for d in range(2):
        sel = [i for i, u in enumerate(units) if u[1] == d]
        tri = jnp.where((s2 >= t2) if d else (s2 <= t2), 1.0, 0.0).astype(bf16)
        hi, lo = _split_bf16(jnp.concatenate([lw[i] for i in sel], axis=1))
        cum_cat = _dot(tri, hi) + _dot(tri, lo)
        for k, i in enumerate(sel):
            cum_parts[i] = cum_cat[:, k * W:(k + 1) * W]
    cum = jnp.stack(cum_parts)
    p_incl = jnp.exp(cum)
    p_inv = jnp.exp(-cum)
    p_prev = jnp.exp(cum - lw)
    p_last = jnp.exp(jnp.sum(lw, axis=1, keepdims=True))
    a_t = an * p_prev
    b_t = bdir * p_inv
    k_t = kd * p_inv
    r_t = r * p_incl
    b_h = b_t * p_last
    k_h = k_t * p_last

    ar = jnp.concatenate([a_t, r_t], axis=1).astype(bf16)
    prod_b = _bmm_nt(ar, bdiag(b_t))
    prod_k = _bmm_nt(ar, bdiag(k_t))
    a_ab = jnp.where(strict, prod_b[:, :L], 0.0)
    a_rb = jnp.where(incl, prod_b[:, L:], 0.0).astype(bf16)
    a_ak = jnp.where(strict, prod_k[:, :L], 0.0).astype(bf16)
    a_rk = jnp.where(incl, prod_k[:, L:], 0.0).astype(bf16)

    m = 1
    t_inv = None
    while m < L:
        same = (ti // (2 * m)) == (si // (2 * m))
        later_half = ((ti // m) % 2 - (si // m) % 2) * sign == 1
        a_off = jnp.where(same, jnp.where(later_half, a_ab, 0.0), 0.0)
        if m == 1:
            t_inv = jnp.where(ti == si, 1.0, 0.0) + a_off
        else:
            left = _bmm(t_inv.astype(bf16), bdiag(a_off)).astype(bf16)
            t_inv = t_inv + _bmm(left, bdiag(t_inv))
        m *= 2

    v_bd = bdiag(v)
    akv = _bmm(a_ak, v_bd)
    tb = t_inv.astype(bf16)
    w_mat = _bmm(tb, bdiag(a_t))
    z_mat = _bmm(tb, bdiag(akv))
    q_mat = r_t + _bmm(a_rb, bdiag(w_mat))
    y_mat = _bmm(a_rb, bdiag(z_mat)) + _bmm(a_rk, v_bd)

    ones = jnp.ones((per_step, L, LANES), bf16)
    for j in range(ch):
        sl = slice(j * per_step, (j + 1) * per_step)
        h_state = h_scr[...]
        qw = jnp.concatenate([q_mat[sl], w_mat[sl]], axis=1).astype(bf16)
        ou = _bmm(qw, h_state.astype(bf16))
        out = ou[:, :L] + y_mat[sl]
        u_mat = ou[:, L:] + z_mat[sl]
        lhs = jnp.concatenate([b_h[sl], k_h[sl]], axis=1).astype(bf16)
        rhs = jnp.concatenate([u_mat, v[sl]], axis=1).astype(bf16)
        upd = _bmm_tn(lhs, rhs)
        hi, lo = _split_bf16(lw[sl])
        col = jnp.exp(_bmm_tn(hi, ones) + _bmm_tn(lo, ones))
        decay = jnp.concatenate([col] * (W // LANES), axis=2)
        h_scr[...] = jnp.where(diag_blocks, decay * h_state + upd, 0.0)
        for k, (_, d, b, g) in enumerate(units[sl]):
            (o_b if d else o_f)[b, rows_of(j, d), pl.ds(g * W, W)] = out[k]


def _rwkv_post_kernel(of_ref, ob_ref, bonus_ref, g_ref, lnw_ref, lnb_ref, bd_ref, out_ref):
    y = of_ref[...] + ob_ref[...]
    bd = bd_ref[...]
    mean = _head_sum(y, bd) * (1.0 / HEAD_DIM)
    yc = y - mean
    var = _head_sum(yc * yc, bd) * (1.0 / HEAD_DIM)
    yn = yc * lax.rsqrt(var + RWKV_GN_EPS) * lnw_ref[...] + lnb_ref[...]
    out_ref[...] = (yn + bonus_ref[...]) * g_ref[...]


def _rwkv_mixer(h_rwkv, mu, w0, w_up, a0, a_up, g_up, k_k, k_a, r_k, ln_w, ln_b, batch, seq):
    t = h_rwkv.shape[0]
    gw = GROUP_WIDTH
    tm = min(256, seq)
    per_batch = seq // tm
    n_halo = t // 8
    wup = jnp.zeros((2 * RWKV_LORA, 2 * gw), f32)
    aup = jnp.zeros((2 * RWKV_LORA, 2 * gw), f32)
    for e in range(2):
        wup = wup.at[e * RWKV_LORA:(e + 1) * RWKV_LORA, e * gw:(e + 1) * gw].set(w_up[e])
        aup = aup.at[e * RWKV_LORA:(e + 1) * RWKV_LORA, e * gw:(e + 1) * gw].set(a_up[e])
    head_id = jnp.arange(gw) // HEAD_DIM
    bd = (head_id[:, None] == head_id[None, :]).astype(bf16)
    vec = lambda a: a.reshape(1, -1).astype(f32)
    const = lambda a: pl.BlockSpec(a.shape, lambda i: (0,) * a.ndim)
    consts = [vec(mu), w0.astype(f32), wup.astype(bf16), a0.astype(f32), aup.astype(bf16),
              g_up.astype(bf16), vec(k_k), vec(k_a), vec(r_k), bd]
    row = pl.BlockSpec((tm, gw), lambda i: (i, 0))
    row2 = pl.BlockSpec((2, tm, gw), lambda i: (0, i, 0))
    one = jax.ShapeDtypeStruct((t, gw), f32)
    two = jax.ShapeDtypeStruct((2, t, gw), f32)
    r, v, an, g, bonus, lw, kd, bdir = pl.pallas_call(
        functools.partial(_rwkv_prep_kernel, per_batch=per_batch), grid=(t // tm,),
        in_specs=[pl.BlockSpec((tm, RWKV_COLS), lambda i: (i, 0)),
                  pl.BlockSpec((8, RWKV_COLS), lambda i: (jnp.maximum(i * (tm // 8) - 1, 0), 0)),
                  pl.BlockSpec((8, RWKV_COLS), lambda i: (jnp.minimum((i + 1) * (tm // 8), n_halo - 1), 0)),
                  ] + [const(a) for a in consts],
        out_specs=[row, row, row, row, row, row2, row2, row2],
        out_shape=[one, one, one, one, one, two, two, two],
        compiler_params=_cparams("parallel"), name="rwkv_prep",
    )(h_rwkv, h_rwkv, h_rwkv, *consts)

    L = RWKV_CHUNK
    ch = RWKV_CHUNKS_PER_STEP
    rows = ch * L
    n_steps = seq // rows
    n_grp = gw // RWKV_GROUP_LANES
    r3, v3, an3 = (a.reshape(batch, seq, gw) for a in (r, v, an))
    lw4, kd4, bd4 = (a.reshape(2, batch, seq, gw) for a in (lw, kd, bdir))
    fwd = pl.BlockSpec((batch, rows, gw), lambda c: (0, c, 0))
    bwd = pl.BlockSpec((batch, rows, gw), lambda c: (0, n_steps - 1 - c, 0))
    fwd_dir = pl.BlockSpec((1, batch, rows, gw), lambda c: (0, 0, c, 0))
    bwd_dir = pl.BlockSpec((1, batch, rows, gw), lambda c: (1, 0, n_steps - 1 - c, 0))
    o3 = jax.ShapeDtypeStruct((batch, seq, gw), f32)
    o_fwd, o_bwd = pl.pallas_call(
        functools.partial(_rwkv_chunk_kernel, batch=batch, ch=ch), grid=(n_steps,),
        in_specs=[fwd, bwd, fwd, bwd, fwd, bwd, fwd_dir, bwd_dir, fwd_dir, bwd_dir, fwd_dir, bwd_dir],
        out_specs=[fwd, bwd], out_shape=[o3, o3],
        scratch_shapes=[pltpu.VMEM((2 * batch * n_grp, RWKV_GROUP_LANES, RWKV_GROUP_LANES), f32)],
        compiler_params=_cparams("arbitrary"), name="rwkv_chunk",
    )(r3, r3, v3, v3, an3, an3, lw4, lw4, kd4, kd4, bd4, bd4)

    tp = min(512, seq)
    rowp = pl.BlockSpec((tp, gw), lambda i: (i, 0))
    cvec = pl.BlockSpec((1, gw), lambda i: (0, 0))
    return pl.pallas_call(
        _rwkv_post_kernel, grid=(t // tp,),
        in_specs=[rowp, rowp, rowp, rowp, cvec, cvec, pl.BlockSpec((gw, gw), lambda i: (0, 0))],
        out_specs=rowp, out_shape=one,
        compiler_params=_cparams("parallel"), name="rwkv_post",
    )(o_fwd.reshape(t, gw), o_bwd.reshape(t, gw), bonus, g, vec(ln_w), vec(ln_b), bd)


def _s5_operators(a_re, a_im, log_dt, b_re, b_im, c_re, c_im):
    L = S5_CHUNK
    lam_r, lam_i = a_re.astype(f32), a_im.astype(f32)
    dt = jnp.exp(log_dt.astype(f32))[..., None]
    mag = jnp.exp(lam_r * dt)
    abar_r, abar_i = mag * jnp.cos(lam_i * dt), mag * jnp.sin(lam_i * dt)
    den = lam_r * lam_r + lam_i * lam_i
    nr, ni = abar_r - 1.0, abar_i
    coef_r = (nr * lam_r + ni * lam_i) / den
    coef_i = (ni * lam_r - nr * lam_i) / den
    br, bi = b_re.astype(f32)[None], b_im.astype(f32)[None]
    bb_r = coef_r[..., None] * br - coef_i[..., None] * bi
    bb_i = coef_r[..., None] * bi + coef_i[..., None] * br
    steps = jnp.arange(L + 1, dtype=f32)[:, None, None, None]
    pmag = jnp.exp(steps * (lam_r * dt)[None])
    pw_r = pmag * jnp.cos(steps * (lam_i * dt)[None])
    pw_i = pmag * jnp.sin(steps * (lam_i * dt)[None])
    cr, ci = c_re.astype(f32)[None, None], c_im.astype(f32)[None, None]
    cp_r = cr * pw_r[:, :, :, None, :] - ci * pw_i[:, :, :, None, :]
    cp_i = cr * pw_i[:, :, :, None, :] + ci * pw_r[:, :, :, None, :]
    kern = (jnp.einsum('kegop,egpi->kegoi', cp_r, bb_r) - jnp.einsum('kegop,egpi->kegoi', cp_i, bb_i))
    idx = jnp.arange(L)
    lag = idx[None, :] - idx[:, None]
    lags = jnp.arange(L + 1)
    sel_f = (lag[:, :, None] == lags).astype(f32)
    sel_b = (-lag[:, :, None] == lags).astype(f32)
    toep = (jnp.einsum('stk,kgoi->stgoi', sel_f, kern[:, 0], precision=lax.Precision.HIGHEST)
            + jnp.einsum('stk,kgoi->stgoi', sel_b, kern[:, 1], precision=lax.Precision.HIGHEST))
    m_op = jnp.transpose(toep, (2, 0, 4, 1, 3))
    g_, h_ = S5_GROUPS, S5_GROUP
    m_op = m_op.reshape(g_, L * h_, L * h_)

    def state_in(pr, pi, e):
        re = pr[..., None] * bb_r[e][None] - pi[..., None] * bb_i[e][None]
        im = pr[..., None] * bb_i[e][None] + pi[..., None] * bb_r[e][None]
        return jnp.transpose(re, (1, 0, 3, 2)), jnp.transpose(im, (1, 0, 3, 2))

    f_re, f_im = state_in(pw_r[L - 1 - idx, 0], pw_i[L - 1 - idx, 0], 0)
    b_re_, b_im_ = state_in(pw_r[idx, 1], pw_i[idx, 1], 1)
    b_sum = jnp.concatenate([f_re, f_im, b_re_, b_im_], axis=-1).reshape(g_, L * h_, 4 * S5_STATE)

    def state_out(k_idx, e):
        re = jnp.transpose(cp_r[k_idx, e], (1, 3, 0, 2))
        im = jnp.transpose(cp_i[k_idx, e], (1, 3, 0, 2))
        return re, -im

    fo_r, fo_i = state_out(idx + 1, 0)
    bo_r, bo_i = state_out(L - idx, 1)
    c_out = jnp.concatenate([fo_r, fo_i, bo_r, bo_i], axis=1).reshape(g_, 4 * S5_STATE, L * h_)
    a_pow = jnp.stack([jnp.stack([pw_r[L, e], pw_i[L, e]]) for e in range(2)])
    return m_op.astype(bf16), b_sum.astype(bf16), c_out.astype(bf16), a_pow


def _s5_state_in_kernel(u_ref, bsum_ref, s_ref):
    s_ref[0] = _dot(u_ref[0], bsum_ref[0])


def _s5_carry_kernel(sf_r, sf_i, sb_r, sb_i, af_r, af_i, ab_r, ab_i, xf_r, xf_i, xb_r, xb_i, *, nch):
    rows, p = af_r.shape
    zero = jnp.zeros((rows, p), f32)

    def body(i, carry):
        fr, fi, br, bi = carry
        j = nch - 1 - i
        xf_r[i] = fr
        xf_i[i] = fi
        xb_r[j] = br
        xb_i[j] = bi
        ar, ai = af_r[...], af_i[...]
        nfr = ar * fr - ai * fi + sf_r[i]
        nfi = ar * fi + ai * fr + sf_i[i]
        ar, ai = ab_r[...], ab_i[...]
        nbr = ar * br - ai * bi + sb_r[j]
        nbi = ar * bi + ai * br + sb_i[j]
        return nfr, nfi, nbr, nbi

    lax.fori_loop(0, nch, body, (zero, zero, zero, zero))


def _s5_out_kernel(u_ref, x_ref, m_ref, cout_ref, y_ref):
    y_ref[0] = _dot(u_ref[0], m_ref[0]) + _dot(x_ref[0], cout_ref[0])


def _s5_post_kernel(y_ref, u_ref, d_ref, w_ref, b_ref, o_ref):
    u = u_ref[...]
    z = y_ref[...] + d_ref[...] * u
    c0 = math.sqrt(2.0 / math.pi)
    z = 0.5 * z * (1.0 + jnp.tanh(c0 * (z + 0.044715 * (z * z * z))))
    gate = _dot(z.astype(bf16), w_ref[...]) + b_ref[...]
    o_ref[...] = z * _sigmoid(gate)


def _s5_mixer(u, a_re, a_im, log_dt, b_re, b_im, c_re, c_im, d_skip, glu_w, glu_b, batch, seq):
    t = u.shape[0]
    L, g_, h_, p = S5_CHUNK, S5_GROUPS, S5_GROUP, S5_STATE
    nch = seq // L
    rows = batch * nch
    m_op, b_sum, c_out, a_pow = _s5_operators(a_re, a_im, log_dt, b_re, b_im, c_re, c_im)
    ug = jnp.transpose(u.reshape(rows, L, g_, h_), (2, 0, 1, 3)).reshape(g_, rows, L * h_).astype(bf16)
    grp = lambda r, c: pl.BlockSpec((1, r, c), lambda g: (g, 0, 0))
    s_in = pl.pallas_call(
        _s5_state_in_kernel, grid=(g_,),
        in_specs=[grp(rows, L * h_), grp(L * h_, 4 * p)], out_specs=grp(rows, 4 * p),
        out_shape=jax.ShapeDtypeStruct((g_, rows, 4 * p), f32),
        compiler_params=_cparams("parallel"), name="s5_state_in",
    )(ug, b_sum)
    s_in = jnp.transpose(s_in.reshape(g_, batch, nch, 4, p), (3, 2, 1, 0, 4)).reshape(4, nch, batch * g_, p)
    coef = jnp.broadcast_to(a_pow[:, :, None], (2, 2, batch, g_, p)).reshape(2, 2, batch * g_, p)
    st = jax.ShapeDtypeStruct((nch, batch * g_, p), f32)
    rb = 8
    seq_spec = pl.BlockSpec((nch, rb, p), lambda i: (0, i, 0))
    coef_spec = pl.BlockSpec((rb, p), lambda i: (i, 0))
    xs = pl.pallas_call(
        functools.partial(_s5_carry_kernel, nch=nch), grid=(batch * g_ // rb,),
        in_specs=[seq_spec] * 4 + [coef_spec] * 4, out_specs=[seq_spec] * 4, out_shape=[st] * 4,
        compiler_params=_cparams("parallel"), name="s5_carry",
    )(s_in[0], s_in[1], s_in[2], s_in[3], coef[0, 0], coef[0, 1], coef[1, 0], coef[1, 1])
    x_cat = jnp.stack(xs).reshape(4, nch, batch, g_, p)
    x_cat = jnp.transpose(x_cat, (3, 2, 1, 0, 4)).reshape(g_, rows, 4 * p).astype(bf16)
    y = pl.pallas_call(
        _s5_out_kernel, grid=(g_,),
        in_specs=[grp(rows, L * h_), grp(rows, 4 * p), grp(L * h_, L * h_), grp(4 * p, L * h_)],
        out_specs=grp(rows, L * h_),
        out_shape=jax.ShapeDtypeStruct((g_, rows, L * h_), f32),
        compiler_params=_cparams("parallel"), name="s5_out",
    )(ug, x_cat, m_op, c_out)
    y = jnp.transpose(y.reshape(g_, rows, L, h_), (1, 2, 0, 3)).reshape(t, g_ * h_)
    tm = min(512, seq)
    w = g_ * h_
    row = pl.BlockSpec((tm, w), lambda i: (i, 0))
    cvec = pl.BlockSpec((1, w), lambda i: (0, 0))
    return pl.pallas_call(
        _s5_post_kernel, grid=(t // tm,),
        in_specs=[row, row, cvec, pl.BlockSpec((w, w), lambda i: (0, 0)), cvec],
        out_specs=row, out_shape=jax.ShapeDtypeStruct((t, w), f32),
        compiler_params=_cparams("parallel"), name="s5_post",
    )(y, u, d_skip.reshape(1, w).astype(f32), glu_w.astype(bf16), glu_b.reshape(1, w).astype(f32))


def _outproj_kernel(o0_ref, o1_ref, o2_ref, o3_ref, w_ref, x_ref, g_ref, out_ref):
    acc = None
    for i, o_ref in enumerate((o0_ref, o1_ref, o2_ref, o3_ref)):
        part = _dot(o_ref[...].astype(bf16), w_ref[i * GROUP_WIDTH:(i + 1) * GROUP_WIDTH, :])
        acc = part if acc is None else acc + part
    out_ref[...] = x_ref[...] + g_ref[0] * acc


def _outproj(outs, w_out, x, gate, seq):
    t, d = x.shape
    tm = min(512, seq)
    tn = 1024
    per_batch = seq // tm
    part = pl.BlockSpec((tm, GROUP_WIDTH), lambda i, j: (i, 0))
    return pl.pallas_call(
        _outproj_kernel, grid=(t // tm, d // tn),
        in_specs=[part, part, part, part,
                  pl.BlockSpec((4 * GROUP_WIDTH, tn), lambda i, j: (0, j)),
                  pl.BlockSpec((tm, tn), lambda i, j: (i, j)),
                  pl.BlockSpec((1, 1, tn), lambda i, j: (i // per_batch, 0, j))],
        out_specs=pl.BlockSpec((tm, tn), lambda i, j: (i, j)),
        out_shape=jax.ShapeDtypeStruct((t, d), f32),
        compiler_params=_cparams("parallel", "parallel"), name="out_proj",
    )(*outs, w_out.astype(bf16), x, gate)


def _moe_ffn_kernel(xs_ref, gate_ref, g2_ref, w1_ref, w3_ref, w2_ref, o_ref):
    f = pl.program_id(2)
    xs = xs_ref[0, 0]
    a = _dot(xs, w1_ref[0, 0].astype(bf16))
    b = _dot(xs, w3_ref[0, 0].astype(bf16))
    hid = (a * _sigmoid(a) * b).astype(bf16)
    part = _dot(hid, w2_ref[0, 0].astype(bf16)) * gate_ref[0, 0] * g2_ref[0]

    @pl.when(f == 0)
    def _():
        o_ref[0, 0] = part

    @pl.when(f > 0)
    def _():
        o_ref[0, 0] += part


def _moe_ffn(xs, gate, g2, w1, w3, w2, layer):
    batch, n_exp, cap, d = xs.shape
    ff = w1.shape[3]
    tf = 256
    return pl.pallas_call(
        _moe_ffn_kernel, grid=(n_exp, batch, ff // tf),
        in_specs=[pl.BlockSpec((1, 1, cap, d), lambda e, b, f: (b, e, 0, 0)),
                  pl.BlockSpec((1, 1, cap, 1), lambda e, b, f: (b, e, 0, 0)),
                  pl.BlockSpec((1, 1, d), lambda e, b, f: (b, 0, 0)),
                  pl.BlockSpec((1, 1, d, tf), lambda e, b, f: (layer, e, 0, f)),
                  pl.BlockSpec((1, 1, d, tf), lambda e, b, f: (layer, e, 0, f)),
                  pl.BlockSpec((1, 1, tf, d), lambda e, b, f: (layer, e, f, 0))],
        out_specs=pl.BlockSpec((1, 1, cap, d), lambda e, b, f: (b, e, 0, 0)),
        out_shape=jax.ShapeDtypeStruct((batch, n_exp, cap, d), f32),
        compiler_params=_cparams("parallel", "parallel", "arbitrary"), name="moe_ffn",
    )(xs, gate, g2, w1, w3, w2)


def _moe_residual(x, xn, logits, g2, w1, w3, w2, layer, batch, seq):
    d = xn.shape[1]
    capacity = EC_CAPACITY_FACTOR * seq // N_EXPERTS
    aff = jax.nn.softmax(logits.reshape(batch, seq, N_EXPERTS), axis=-1)
    gate, idx = lax.top_k(jnp.swapaxes(aff, 1, 2), capacity)
    bidx = jnp.arange(batch)[:, None, None]
    xs = xn.reshape(batch, seq, d)[bidx, idx]
    ys = _moe_ffn(xs, gate[..., None], g2, w1, w3, w2, layer)
    return x.reshape(batch, seq, d).at[bidx, idx].add(ys).reshape(batch * seq, d)


def _in_proj_weights(w_in_l):
    o1, o2, o3 = SWA_COLS, SWA_COLS + MLA_COLS, SWA_COLS + MLA_COLS + RWKV_COLS
    w_mla = jnp.zeros((w_in_l.shape[0], MLA_COLS_PAD), f32).at[:, :MLA_COLS].set(w_in_l[:, o1:o2])
    return (w_in_l[:, :o1].astype(bf16), w_mla.astype(bf16),
            w_in_l[:, o2:o3].astype(bf16), w_in_l[:, o3:].astype(bf16))


def kernel(x, c, positions, ada_w, ada_b, norm1_g, norm2_g, w_in, w_out, swa_q_gain, swa_k_gain, swa_sink, mla_q_a_gain, mla_kv_a_gain, mla_w_uq, mla_w_ukv, mla_q_gain, mla_k_gain, rwkv_mu, rwkv_w0, rwkv_w_up, rwkv_a0, rwkv_a_up, rwkv_g_up, rwkv_k_k, rwkv_k_a, rwkv_r_k, rwkv_ln_w, rwkv_ln_b, s5_a_re, s5_a_im, s5_log_dt, s5_b_re, s5_b_im, s5_c_re, s5_c_im, s5_d, s5_glu_w, s5_glu_b, router_w, moe_w1, moe_w3, moe_w2):
    batch, seq, d = x.shape
    depth = ada_w.shape[0]
    t = batch * seq
    cos_h, sin_h, cos_r, sin_r = _rope_tables(positions)
    mod = _ada_mod(c, ada_w, ada_b)
    xf = x.reshape(t, d)
    for l in range(depth):
        sh1, sc1, g1, sh2, sc2, g2 = [m[:, None, :] for m in jnp.split(mod[l], 6, axis=-1)]
        xn = _norm_call(xf, norm1_g[l], sc1, sh1, seq)
        w_swa, w_mla, w_rwkv, w_s5 = _in_proj_weights(w_in[l])
        h_swa = _matmul(xn, w_swa, tm=1024, tn=SWA_COLS, name="in_proj_swa")
        h_mla = _matmul(xn, w_mla, tm=1024, tn=MLA_COLS_PAD, name="in_proj_mla")
        h_rwkv = _matmul(xn, w_rwkv, tm=1024, tn=640, name="in_proj_rwkv")
        h_s5 = _matmul(xn, w_s5, tm=1024, tn=GROUP_WIDTH, name="in_proj_s5")
        o_swa = _swa_mixer(h_swa, cos_h, sin_h, swa_q_gain[l], swa_k_gain[l], swa_sink[l], batch, seq)
        o_mla = _mla_mixer(h_mla, cos_r, sin_r, mla_q_a_gain[l], mla_kv_a_gain[l], mla_w_uq[l],
                           mla_w_ukv[l], mla_q_gain[l], mla_k_gain[l], batch, seq)
        o_rwkv = _rwkv_mixer(h_rwkv, rwkv_mu[l], rwkv_w0[l], rwkv_w_up[l], rwkv_a0[l], rwkv_a_up[l],
                             rwkv_g_up[l], rwkv_k_k[l], rwkv_k_a[l], rwkv_r_k[l],
                             rwkv_ln_w[l], rwkv_ln_b[l], batch, seq)
        o_s5 = _s5_mixer(h_s5, s5_a_re[l], s5_a_im[l], s5_log_dt[l], s5_b_re[l], s5_b_im[l],
                         s5_c_re[l], s5_c_im[l], s5_d[l], s5_glu_w[l], s5_glu_b[l], batch, seq)
        xf = _outproj((o_swa, o_mla, o_rwkv, o_s5), w_out[l], xf, g1, seq)
        xn2, logits = _norm_call(xf, norm2_g[l], sc2, sh2, seq, router_w=router_w[l])
        xf = _moe_residual(xf, xn2, logits, g2, moe_w1, moe_w3, moe_w2, l, batch, seq)
    return xf.reshape(batch, seq, d)
```

```python
import functools
import math

import jax
import jax.numpy as jnp
from jax import lax
from jax.experimental import pallas as pl
from jax.experimental.pallas import tpu as pltpu

f32 = jnp.float32
bf16 = jnp.bfloat16

D_MODEL = 2048
HEAD_DIM = 64
GROUP_WIDTH = 512
ROPE_THETA = 10000.0
NORM_EPS = 1e-6
NEG_INF = -1e30

SWA_HEADS = 8
SWA_KV_HEADS = 2
SWA_WINDOW = 128
SWA_BLOCK = 128
SWA_Q = 512
SWA_KV = 128
SWA_COLS = 768
SWA_BLOCKS_PER_STEP = 4

MLA_HEADS = 8
MLA_NOPE = 64
MLA_ROPE = 32
MLA_V = 64
MLA_QK = 96
MLA_Q_RANK = 384
MLA_KV_RANK = 128
MLA_COLS = 544
MLA_COLS_PAD = 640
MLA_HEAD_PAD = 128

RWKV_HEADS = 8
RWKV_LORA = 64
RWKV_GATE_LORA = 128
RWKV_GN_EPS = 64e-5
RWKV_COLS = 1920
RWKV_CHUNK = 64
RWKV_CHUNKS_PER_STEP = 2
RWKV_GROUP_LANES = 256

S5_GROUP = 16
S5_GROUPS = 32
S5_STATE = 64
S5_CHUNK = 32

N_EXPERTS = 16
EC_CAPACITY_FACTOR = 2
D_FF_EXPERT = 1024

VMEM_LIMIT_BYTES = 52 * 1024 * 1024
LANES = 128


def _cparams(*sem):
    return pltpu.CompilerParams(dimension_semantics=sem, vmem_limit_bytes=VMEM_LIMIT_BYTES)


def _sigmoid(x):
    return 1.0 / (1.0 + jnp.exp(-x))


def _split_bf16(x):
    hi = x.astype(bf16)
    lo = (x - hi.astype(f32)).astype(bf16)
    return hi, lo


def _dot(a, b):
    return jnp.dot(a, b, preferred_element_type=f32)


def _dot_nt(a, b):
    return lax.dot_general(a, b, (((1,), (1,)), ((), ())), preferred_element_type=f32)


def _dot_tn(a, b):
    return lax.dot_general(a, b, (((0,), (0,)), ((), ())), preferred_element_type=f32)


def _ada_kernel(c_ref, w_ref, b_ref, o_ref):
    c = c_ref[...]
    cond = (c * _sigmoid(c)).astype(bf16)
    o_ref[0] = _dot(cond, w_ref[0].astype(bf16)) + b_ref[0]


def _ada_mod(c, ada_w, ada_b):
    depth, d, n = ada_w.shape
    b = c.shape[0]
    rows = 8
    c_pad = jnp.zeros((rows, d), f32).at[:b].set(c)
    tn = 512
    out = pl.pallas_call(
        _ada_kernel,
        grid=(depth, n // tn),
        in_specs=[
            pl.BlockSpec((rows, d), lambda l, j: (0, 0)),
            pl.BlockSpec((1, d, tn), lambda l, j: (l, 0, j)),
            pl.BlockSpec((1, 1, tn), lambda l, j: (l, 0, j)),
        ],
        out_specs=pl.BlockSpec((1, rows, tn), lambda l, j: (l, 0, j)),
        out_shape=jax.ShapeDtypeStruct((depth, rows, n), f32),
        compiler_params=_cparams("parallel", "parallel"),
        name="ada_mod",
    )(c_pad, ada_w, ada_b.reshape(depth, 1, n))
    return out[:, :b]


def _mm_kernel(x_ref, w_ref, o_ref):
    o_ref[...] = _dot(x_ref[...].astype(bf16), w_ref[...].astype(bf16)).astype(o_ref.dtype)


def _matmul(x, w, *, tm, tn, name):
    m, k = x.shape
    n = w.shape[1]
    tm = min(tm, m)
    return pl.pallas_call(
        _mm_kernel,
        grid=(m // tm, n // tn),
        in_specs=[pl.BlockSpec((tm, k), lambda i, j: (i, 0)),
                  pl.BlockSpec((k, tn), lambda i, j: (0, j))],
        out_specs=pl.BlockSpec((tm, tn), lambda i, j: (i, j)),
        out_shape=jax.ShapeDtypeStruct((m, n), f32),
        compiler_params=_cparams("parallel", "parallel"),
        name=name,
    )(x, w)


def _norm_mod(x_ref, g_ref, sc_ref, sh_ref):
    x = x_ref[...]
    ms = jnp.mean(x * x, axis=-1, keepdims=True)
    y = x * lax.rsqrt(ms + NORM_EPS) * g_ref[...]
    return y * (1.0 + sc_ref[0]) + sh_ref[0]


def _norm_kernel(x_ref, g_ref, sc_ref, sh_ref, o_ref):
    o_ref[...] = _norm_mod(x_ref, g_ref, sc_ref, sh_ref).astype(o_ref.dtype)


def _norm_router_kernel(x_ref, g_ref, sc_ref, sh_ref, whi_ref, wlo_ref, o_ref, logit_ref):
    y = _norm_mod(x_ref, g_ref, sc_ref, sh_ref)
    o_ref[...] = y.astype(o_ref.dtype)
    yhi, ylo = _split_bf16(y)
    whi = whi_ref[...]
    logit_ref[...] = _dot(yhi, whi) + _dot(ylo, whi) + _dot(yhi, wlo_ref[...])


def _norm_call(x, gain, scale, shift, seq, router_w=None):
    t, d = x.shape
    tm = min(512, seq)
    per_batch = seq // tm
    row = pl.BlockSpec((tm, d), lambda i: (i, 0))
    mod = pl.BlockSpec((1, 1, d), lambda i: (i // per_batch, 0, 0))
    in_specs = [row, pl.BlockSpec((1, d), lambda i: (0, 0)), mod, mod]
    args = [x, gain.reshape(1, d), scale, shift]
    if router_w is None:
        return pl.pallas_call(
            _norm_kernel, grid=(t // tm,), in_specs=in_specs, out_specs=row,
            out_shape=jax.ShapeDtypeStruct((t, d), bf16),
            compiler_params=_cparams("parallel"), name="norm_mod",
        )(*args)
    e = router_w.shape[1]
    w_pad = jnp.zeros((d, LANES), f32).at[:, :e].set(router_w)
    whi, wlo = _split_bf16(w_pad)
    wspec = pl.BlockSpec((d, LANES), lambda i: (0, 0))
    xn, logits = pl.pallas_call(
        _norm_router_kernel, grid=(t // tm,), in_specs=in_specs + [wspec, wspec],
        out_specs=[row, pl.BlockSpec((tm, LANES), lambda i: (i, 0))],
        out_shape=[jax.ShapeDtypeStruct((t, d), bf16), jax.ShapeDtypeStruct((t, LANES), f32)],
        compiler_params=_cparams("parallel"), name="norm_mod_router",
    )(*args, whi, wlo)
    return xn, logits[:, :e]


def _rope_lanes(x, cos_t, sin_t, half, first_mask):
    n = x.shape[-1]
    fwd = pltpu.roll(x, n - half, axis=1)
    bwd = pltpu.roll(x, half, axis=1)
    return x * cos_t + jnp.where(first_mask, fwd, bwd) * sin_t


def _rope_tables(positions):
    t = positions.size
    pos = positions.reshape(t, 1).astype(f32)

    def tables(dim):
        inv_freq = ROPE_THETA ** (-jnp.arange(0, dim, 2, dtype=f32) / dim)
        ang = pos * inv_freq
        return jnp.cos(ang), jnp.sin(ang)

    c, s = tables(HEAD_DIM)
    cos_h = jnp.concatenate([c, c, c, c], axis=-1)
    sin_h = jnp.concatenate([-s, s, -s, s], axis=-1)
    c, s = tables(MLA_ROPE)
    one = jnp.ones((t, MLA_NOPE), f32)
    zero = jnp.zeros((t, MLA_NOPE), f32)
    pad1 = jnp.ones((t, MLA_HEAD_PAD - MLA_QK), f32)
    pad0 = jnp.zeros((t, MLA_HEAD_PAD - MLA_QK), f32)
    cos_r = jnp.concatenate([one, c, c, pad1], axis=-1)
    sin_r = jnp.concatenate([zero, -s, s, pad0], axis=-1)
    return cos_h, sin_h, cos_r, sin_r


def _pair_rmsnorm(x, gain):
    lane = lax.broadcasted_iota(jnp.int32, x.shape, 1)
    lo = lane < HEAD_DIM
    x2 = x * x
    s_lo = jnp.sum(jnp.where(lo, x2, 0.0), axis=-1, keepdims=True)
    s_hi = jnp.sum(jnp.where(lo, 0.0, x2), axis=-1, keepdims=True)
    ms = jnp.where(lo, s_lo, s_hi) * (1.0 / HEAD_DIM)
    return x * lax.rsqrt(ms + NORM_EPS) * gain


def _swa_prep_kernel(h_ref, cos_ref, sin_ref, qg_ref, kg_ref, q_ref, k_ref, v_ref):
    cos_t = cos_ref[...]
    sin_t = sin_ref[...]
    lane = lax.broadcasted_iota(jnp.int32, cos_t.shape, 1)
    first = (lane % HEAD_DIM) < (HEAD_DIM // 2)
    scale = HEAD_DIM ** -0.5
    for p in range(SWA_Q // LANES):
        x = _pair_rmsnorm(h_ref[:, p * LANES:(p + 1) * LANES], qg_ref[...])
        x = _rope_lanes(x, cos_t, sin_t, HEAD_DIM // 2, first)
        q_ref[:, p * LANES:(p + 1) * LANES] = (x * scale).astype(bf16)
    x = _pair_rmsnorm(h_ref[:, SWA_Q:SWA_Q + SWA_KV], kg_ref[...])
    k_ref[...] = _rope_lanes(x, cos_t, sin_t, HEAD_DIM // 2, first).astype(bf16)
    v_ref[...] = h_ref[:, SWA_Q + SWA_KV:SWA_COLS].astype(bf16)


def _swa_attn_kernel(q_ref, k_ref, v_ref, sink_ref, o_ref, *, seq, nblk):
    step = pl.program_id(1)
    band = 3 * SWA_BLOCK
    grp = SWA_HEADS // SWA_KV_HEADS
    rows = grp * SWA_BLOCK
    row_in_blk = lax.broadcasted_iota(jnp.int32, (rows, band), 0) % SWA_BLOCK
    col = lax.broadcasted_iota(jnp.int32, (rows, band), 1)
    sinks = [jnp.concatenate(
        [jnp.broadcast_to(sink_ref[kh * grp + g:kh * grp + g + 1, 0:1], (SWA_BLOCK, 1)) for g in range(grp)],
        axis=0) for kh in range(SWA_KV_HEADS)]
    qs, ks, vs, valid, sink = [], [], [], [], []
    for i in range(nblk):
        n = step * nblk + i
        start = pl.multiple_of(jnp.clip((n - 1) * SWA_BLOCK, 0, seq - band), SWA_BLOCK)
        kb = k_ref[pl.ds(start, band), :]
        vb = v_ref[pl.ds(start, band), :]
        q = q_ref[i * SWA_BLOCK:(i + 1) * SWA_BLOCK, :]
        ok = jnp.abs(n * SWA_BLOCK + row_in_blk - (start + col)) <= SWA_WINDOW
        for kh in range(SWA_KV_HEADS):
            qs.append(jnp.concatenate(
                [q[:, (kh * grp + g) * HEAD_DIM:(kh * grp + g + 1) * HEAD_DIM] for g in range(grp)], axis=0))
            ks.append(kb[:, kh * HEAD_DIM:(kh + 1) * HEAD_DIM])
            vs.append(vb[:, kh * HEAD_DIM:(kh + 1) * HEAD_DIM])
            valid.append(ok)
            sink.append(sinks[kh])
    sink = jnp.stack(sink)
    s = jnp.where(jnp.stack(valid), _bmm_nt(jnp.stack(qs), jnp.stack(ks)), NEG_INF)
    m = jnp.maximum(jnp.max(s, axis=-1, keepdims=True), sink)
    p = jnp.exp(s - m)
    denom = jnp.sum(p, axis=-1, keepdims=True) + jnp.exp(sink - m)
    o = _bmm(p.astype(bf16), jnp.stack(vs)) / denom
    for i in range(nblk):
        o_ref[i * SWA_BLOCK:(i + 1) * SWA_BLOCK, :] = jnp.concatenate(
            [o[i * SWA_KV_HEADS + kh, g * SWA_BLOCK:(g + 1) * SWA_BLOCK]
             for kh in range(SWA_KV_HEADS) for g in range(grp)], axis=-1)


def _swa_mixer(h_swa, cos_h, sin_h, q_gain, k_gain, sink, batch, seq):
    t = h_swa.shape[0]
    tm = min(512, seq)
    row = lambda w: pl.BlockSpec((tm, w), lambda i: (i, 0))
    const = pl.BlockSpec((1, LANES), lambda i: (0, 0))
    q, k, v = pl.pallas_call(
        _swa_prep_kernel, grid=(t // tm,),
        in_specs=[row(SWA_COLS), row(LANES), row(LANES), const, const],
        out_specs=[row(SWA_Q), row(SWA_KV), row(SWA_KV)],
        out_shape=[jax.ShapeDtypeStruct((t, SWA_Q), bf16),
                   jax.ShapeDtypeStruct((t, SWA_KV), bf16),
                   jax.ShapeDtypeStruct((t, SWA_KV), bf16)],
        compiler_params=_cparams("parallel"), name="swa_prep",
    )(h_swa, cos_h, sin_h, jnp.tile(q_gain, 2).reshape(1, LANES), jnp.tile(k_gain, 2).reshape(1, LANES))
    nblk = SWA_BLOCKS_PER_STEP
    tq = nblk * SWA_BLOCK
    nb = seq // tq
    sink_t = jnp.broadcast_to(sink.astype(f32).reshape(SWA_HEADS, 1), (SWA_HEADS, LANES))
    return pl.pallas_call(
        functools.partial(_swa_attn_kernel, seq=seq, nblk=nblk), grid=(batch, nb),
        in_specs=[pl.BlockSpec((tq, SWA_Q), lambda b, n: (b * nb + n, 0)),
                  pl.BlockSpec((seq, SWA_KV), lambda b, n: (b, 0)),
                  pl.BlockSpec((seq, SWA_KV), lambda b, n: (b, 0)),
                  pl.BlockSpec((SWA_HEADS, LANES), lambda b, n: (0, 0))],
        out_specs=pl.BlockSpec((tq, SWA_Q), lambda b, n: (b * nb + n, 0)),
        out_shape=jax.ShapeDtypeStruct((t, SWA_Q), f32),
        compiler_params=_cparams("parallel", "parallel"), name="swa_attn",
    )(q, k, v, sink_t)


def _mla_prep_kernel(h_ref, cos_ref, sin_ref, qag_ref, kvag_ref, wq_ref, wk_ref, wpe_ref, wv_ref,
                     qg_ref, kg_ref, q_ref, k_ref, v_ref):
    cos_t = cos_ref[...]
    sin_t = sin_ref[...]
    lane = lax.broadcasted_iota(jnp.int32, cos_t.shape, 1)
    first = lane < MLA_NOPE + MLA_ROPE // 2
    cq = h_ref[:, :MLA_Q_RANK]
    cq = cq * lax.rsqrt(jnp.mean(cq * cq, axis=-1, keepdims=True) + NORM_EPS) * qag_ref[...]
    ckv = h_ref[:, MLA_Q_RANK:MLA_Q_RANK + MLA_KV_RANK]
    ckv = (ckv * lax.rsqrt(jnp.mean(ckv * ckv, axis=-1, keepdims=True) + NORM_EPS) * kvag_ref[...]).astype(bf16)
    kpe = h_ref[:, MLA_Q_RANK + MLA_KV_RANK:MLA_COLS_PAD].astype(bf16)
    q_all = _dot(cq.astype(bf16), wq_ref[...])
    k_all = _dot(ckv, wk_ref[...]) + _dot(kpe, wpe_ref[...])
    v_all = _dot(ckv, wv_ref[...])
    one_lane = lane == MLA_V
    scale = MLA_QK ** -0.5 * math.log2(math.e)

    def head_norm(x, gain):
        ms = jnp.sum(x * x, axis=-1, keepdims=True) * (1.0 / MLA_QK)
        x = x * lax.rsqrt(ms + NORM_EPS) * gain
        return _rope_lanes(x, cos_t, sin_t, MLA_ROPE // 2, first)

    for h in range(MLA_HEADS):
        sl = slice(h * MLA_HEAD_PAD, (h + 1) * MLA_HEAD_PAD)
        q_ref[0, h] = (head_norm(q_all[:, sl], qg_ref[...]) * scale).astype(bf16)
        k_ref[0, h] = head_norm(k_all[:, sl], kg_ref[...]).astype(bf16)
        v_ref[0, h] = jnp.where(one_lane, 1.0, v_all[:, sl]).astype(bf16)


def _mla_attn_kernel(q_ref, k_ref, v_ref, o_ref, *, tk, nk):
    n_heads = q_ref.shape[1]
    tq = q_ref.shape[2]
    qs = [q_ref[0, hh] for hh in range(n_heads)]

    def body(j, carry):
        off = pl.multiple_of(j * tk, tk)
        new = []
        for hh in range(n_heads):
            m, acc = carry[hh]
            kj = k_ref[0, hh, pl.ds(off, tk), :]
            vj = v_ref[0, hh, pl.ds(off, tk), :]
            s = _dot_nt(qs[hh], kj)
            m_new = jnp.maximum(m, jnp.max(s, axis=-1, keepdims=True))
            alpha = jnp.exp2(m - m_new)
            p = jnp.exp2(s - m_new)
            new.append((m_new, acc * alpha + _dot(p.astype(bf16), vj)))
        return tuple(new)

    init = tuple((jnp.full((tq, 1), NEG_INF, f32), jnp.zeros((tq, MLA_HEAD_PAD), f32))
                 for _ in range(n_heads))
    final = lax.fori_loop(0, nk, body, init)
    o_ref[0] = jnp.concatenate([acc[:, :MLA_V] / acc[:, MLA_V:MLA_V + 1] for _, acc in final], axis=-1)


def _mla_weights(w_uq, w_ukv, q_gain, k_gain):
    hp = MLA_HEAD_PAD
    wq = jnp.zeros((MLA_Q_RANK, MLA_HEADS, hp), f32).at[:, :, :MLA_QK].set(
        w_uq.reshape(MLA_Q_RANK, MLA_HEADS, MLA_QK))
    kv = w_ukv.reshape(MLA_KV_RANK, MLA_HEADS, MLA_NOPE + MLA_V)
    wk = jnp.zeros((MLA_KV_RANK, MLA_HEADS, hp), f32).at[:, :, :MLA_NOPE].set(kv[:, :, :MLA_NOPE])
    wv = jnp.zeros((MLA_KV_RANK, MLA_HEADS, hp), f32).at[:, :, :MLA_V].set(kv[:, :, MLA_NOPE:])
    eye = jnp.eye(MLA_ROPE, dtype=f32)
    wpe = jnp.zeros((LANES, MLA_HEADS, hp), f32).at[:MLA_ROPE, :, MLA_NOPE:MLA_QK].set(
        jnp.broadcast_to(eye[:, None, :], (MLA_ROPE, MLA_HEADS, MLA_ROPE)))
    flat = lambda w: w.reshape(w.shape[0], MLA_HEADS * hp).astype(bf16)
    pad = lambda g: jnp.zeros((1, hp), f32).at[0, :MLA_QK].set(g)
    return flat(wq), flat(wk), flat(wpe), flat(wv), pad(q_gain), pad(k_gain)


def _mla_mixer(h_mla, cos_r, sin_r, q_a_gain, kv_a_gain, w_uq, w_ukv, q_gain, k_gain, batch, seq):
    t = h_mla.shape[0]
    tm = min(512, seq)
    per_batch = seq // tm
    hp = MLA_HEAD_PAD
    wq, wk, wpe, wv, qg, kg = _mla_weights(w_uq, w_ukv, q_gain, k_gain)
    row = lambda w: pl.BlockSpec((tm, w), lambda b, i: (b * per_batch + i, 0))
    const = lambda a: pl.BlockSpec(a.shape, lambda b, i: (0,) * a.ndim)
    head_out = pl.BlockSpec((1, MLA_HEADS, tm, hp), lambda b, i: (b, 0, i, 0))
    head_shape = jax.ShapeDtypeStruct((batch, MLA_HEADS, seq, hp), bf16)
    qag = q_a_gain.reshape(1, MLA_Q_RANK)
    kvag = kv_a_gain.reshape(1, MLA_KV_RANK)
    q, k, v = pl.pallas_call(
        _mla_prep_kernel, grid=(batch, per_batch),
        in_specs=[row(MLA_COLS_PAD), row(LANES), row(LANES), const(qag), const(kvag),
                  const(wq), const(wk), const(wpe), const(wv), const(qg), const(kg)],
        out_specs=[head_out, head_out, head_out],
        out_shape=[head_shape, head_shape, head_shape],
        compiler_params=_cparams("parallel", "parallel"), name="mla_prep",
    )(h_mla, cos_r, sin_r, qag, kvag, wq, wk, wpe, wv, qg, kg)
    tq = min(1024, seq)
    tk = min(1024, seq)
    kv_spec = pl.BlockSpec((1, 2, seq, hp), lambda b, h, i: (b, h, 0, 0))
    o = pl.pallas_call(
        functools.partial(_mla_attn_kernel, tk=tk, nk=seq // tk),
        grid=(batch, MLA_HEADS // 2, seq // tq),
        in_specs=[pl.BlockSpec((1, 2, tq, hp), lambda b, h, i: (b, h, i, 0)), kv_spec, kv_spec],
        out_specs=pl.BlockSpec((1, tq, 2 * MLA_V), lambda b, h, i: (b, i, h)),
        out_shape=jax.ShapeDtypeStruct((batch, seq, MLA_HEADS * MLA_V), f32),
        compiler_params=_cparams("parallel", "parallel", "parallel"), name="mla_attn",
    )(q, k, v)
    return o.reshape(t, MLA_HEADS * MLA_V)


def _head_sum(x, bd):
    hi, lo = _split_bf16(x)
    return _dot(hi, bd) + _dot(lo, bd)


def _rwkv_prep_kernel(h_ref, prev_ref, next_ref, mu_ref, w0_ref, wup_ref, a0_ref, aup_ref, gup_ref,
                      kk_ref, ka_ref, rk_ref, bd_ref,
                      r_ref, v_ref, an_ref, g_ref, bonus_ref, lw_ref, kd_ref, bdir_ref, *, per_batch):
    i = pl.program_id(0)
    x = h_ref[...]
    tm = x.shape[0]
    row = lax.broadcasted_iota(jnp.int32, x.shape, 0)
    prev_row = jnp.where(i % per_batch == 0, 0.0, prev_ref[7:8, :])
    next_row = jnp.where(i % per_batch == per_batch - 1, 0.0, next_ref[0:1, :])
    x_prev = jnp.where(row == 0, prev_row, pltpu.roll(x, 1, axis=0))
    x_next = jnp.where(row == tm - 1, next_row, pltpu.roll(x, tm - 1, axis=0))
    hs = x + mu_ref[...] * (0.5 * (x_prev + x_next) - x)
    gw = GROUP_WIDTH
    r = hs[:, 0:gw]
    k = hs[:, gw:2 * gw]
    v = hs[:, 2 * gw:3 * gw]
    wd = hs[:, 3 * gw:3 * gw + LANES]
    ad = hs[:, 3 * gw + LANES:3 * gw + 2 * LANES]
    gd = hs[:, 3 * gw + 2 * LANES:RWKV_COLS]
    bd = bd_ref[...]
    lora_w = _dot(jnp.tanh(wd).astype(bf16), wup_ref[...])
    lora_a = _dot(ad.astype(bf16), aup_ref[...])
    kk = k * kk_ref[...]
    kk = kk * lax.rsqrt(_head_sum(kk * kk, bd) + 1e-12)
    r_ref[...] = r
    v_ref[...] = v
    an_ref[...] = -kk
    g_ref[...] = _dot(_sigmoid(gd).astype(bf16), gup_ref[...])
    bonus_ref[...] = _head_sum(r * k * rk_ref[...], bd) * v
    for e in range(2):
        z = -(w0_ref[e:e + 1, :] + lora_w[:, e * gw:(e + 1) * gw])
        softplus = jnp.maximum(z, 0.0) + jnp.log(1.0 + jnp.exp(-jnp.abs(z)))
        lw_ref[e] = -jnp.exp(-softplus - 0.5)
        a = _sigmoid(a0_ref[e:e + 1, :] + lora_a[:, e * gw:(e + 1) * gw])
        kd_ref[e] = k * (1.0 + (a - 1.0) * ka_ref[...])
        bdir_ref[e] = kk * a


def _bmm(a, b):
    return lax.dot_general(a, b, (((2,), (1,)), ((0,), (0,))), preferred_element_type=f32)


def _bmm_nt(a, b):
    return lax.dot_general(a, b, (((2,), (2,)), ((0,), (0,))), preferred_element_type=f32)


def _bmm_tn(a, b):
    return lax.dot_general(a, b, (((1,), (1,)), ((0,), (0,))), preferred_element_type=f32)


def _rwkv_chunk_kernel(r_f, r_b, v_f, v_b, an_f, an_b, lw_f, lw_b, kd_f, kd_b, bd_f, bd_b,
                       o_f, o_b, h_scr, *, batch, ch):
    L = RWKV_CHUNK
    N = HEAD_DIM
    W = RWKV_GROUP_LANES
    hpg = W // N
    n_grp = GROUP_WIDTH // W

    @pl.when(pl.program_id(0) == 0)
    def _():
        h_scr[...] = jnp.zeros_like(h_scr)

    units = [(j, d, b, g) for j in range(ch) for d in range(2) for b in range(batch) for g in range(n_grp)]
    per_step = 2 * batch * n_grp
    n_units = len(units)

    def rows_of(j, d):
        return pl.ds((ch - 1 - j if d else j) * L, L)

    def load(ref_f, ref_b, lead):
        return jnp.stack([(ref_b if d else ref_f)[lead + (b, rows_of(j, d), pl.ds(g * W, W))]
                          for (j, d, b, g) in units])

    r = load(r_f, r_b, ())
    v = load(v_f, v_b, ())
    an = load(an_f, an_b, ())
    lw = load(lw_f, lw_b, (0,))
    kd = load(kd_f, kd_b, (0,))
    bdir = load(bd_f, bd_b, (0,))

    shape = (n_units, L, W)
    ui = lax.broadcasted_iota(jnp.int32, shape, 0)
    ti = lax.broadcasted_iota(jnp.int32, shape, 1)
    si = lax.broadcasted_iota(jnp.int32, shape, 2) % N
    sign = 1 - 2 * ((ui // (batch * n_grp)) % 2)
    ahead = (ti - si) * sign
    strict = ahead > 0
    incl = ahead >= 0
    bi = lax.broadcasted_iota(jnp.int32, (W, W), 0) // N
    bj = lax.broadcasted_iota(jnp.int32, (W, W), 1) // N
    diag_blocks = bi == bj

    def bdiag(x):
        return jnp.where(diag_blocks, jnp.concatenate([x] * hpg, axis=1), 0.0).astype(bf16)

    t2 = lax.broadcasted_iota(jnp.int32, (L, L), 0)
    s2 = lax.broadcasted_iota(jnp.int32, (L, L), 1)
    cum_parts = [None] * n_units
    for d in range(2):
        sel = [i for i, u in enumerate(units) if u[1] == d]
        tri = jnp.where((s2 >= t2) if d else (s2 <= t2), 1.0, 0.0).astype(bf16)
        hi, lo = _split_bf16(jnp.concatenate([lw[i] for i in sel], axis=1))
        cum_cat = _dot(tri, hi) + _dot(tri, lo)
        for k, i in enumerate(sel):
            cum_parts[i] = cum_cat[:, k * W:(k + 1) * W]
    cum = jnp.stack(cum_parts)
    p_incl = jnp.exp(cum)
    p_inv = jnp.exp(-cum)
    p_prev = jnp.exp(cum - lw)
    p_last = jnp.exp(jnp.sum(lw, axis=1, keepdims=True))
    a_t = an * p_prev
    b_t = bdir * p_inv
    k_t = kd * p_inv
    r_t = r * p_incl
    b_h = b_t * p_last
    k_h = k_t * p_last

    ar = jnp.concatenate([a_t, r_t], axis=1).astype(bf16)
    prod_b = _bmm_nt(ar, bdiag(b_t))
    prod_k = _bmm_nt(ar, bdiag(k_t))
    a_ab = jnp.where(strict, prod_b[:, :L], 0.0)
    a_rb = jnp.where(incl, prod_b[:, L:], 0.0).astype(bf16)
    a_ak = jnp.where(strict, prod_k[:, :L], 0.0).astype(bf16)
    a_rk = jnp.where(incl, prod_k[:, L:], 0.0).astype(bf16)

    m = 1
    t_inv = None
    while m < L:
        same = (ti // (2 * m)) == (si // (2 * m))
        later_half = ((ti // m) % 2 - (si // m) % 2) * sign == 1
        a_off = jnp.where(same, jnp.where(later_half, a_ab, 0.0), 0.0)
        if m == 1:
            t_inv = jnp.where(ti == si, 1.0, 0.0) + a_off
        else:
            left = _bmm(t_inv.astype(bf16), bdiag(a_off)).astype(bf16)
            t_inv = t_inv + _bmm(left, bdiag(t_inv))
        m *= 2

    v_bd = bdiag(v)
    akv = _bmm(a_ak, v_bd)
    tb = t_inv.astype(bf16)
    w_mat = _bmm(tb, bdiag(a_t))
    z_mat = _bmm(tb, bdiag(akv))
    q_mat = r_t + _bmm(a_rb, bdiag(w_mat))
    y_mat = _bmm(a_rb, bdiag(z_mat)) + _bmm(a_rk, v_bd)

    for j in range(ch):
        sl = slice(j * per_step, (j + 1) * per_step)
        h_t = h_scr[...]
        qw = jnp.concatenate([q_mat[sl], w_mat[sl]], axis=1).astype(bf16)
        ou = _bmm_nt(qw, h_t.astype(bf16))
        out = ou[:, :L] + y_mat[sl]
        u_mat = ou[:, L:] + z_mat[sl]
        keys = jnp.concatenate([b_h[sl], k_h[sl]], axis=1).astype(bf16)
        vals = jnp.concatenate([u_mat, v[sl]], axis=1).astype(bf16)
        upd = _bmm_tn(vals, keys)
        h_scr[...] = jnp.where(diag_blocks, p_last[sl] * h_t + upd, 0.0)
        for k, (_, d, b, g) in enumerate(units[sl]):
            (o_b if d else o_f)[b, rows_of(j, d), pl.ds(g * W, W)] = out[k]


def _rwkv_post_kernel(of_ref, ob_ref, bonus_ref, g_ref, lnw_ref, lnb_ref, bd_ref, out_ref):
    y = of_ref[...] + ob_ref[...]
    bd = bd_ref[...]
    mean = _head_sum(y, bd) * (1.0 / HEAD_DIM)
    yc = y - mean
    var = _head_sum(yc * yc, bd) * (1.0 / HEAD_DIM)
    yn = yc * lax.rsqrt(var + RWKV_GN_EPS) * lnw_ref[...] + lnb_ref[...]
    out_ref[...] = (yn + bonus_ref[...]) * g_ref[...]


def _rwkv_mixer(h_rwkv, mu, w0, w_up, a0, a_up, g_up, k_k, k_a, r_k, ln_w, ln_b, batch, seq):
    t = h_rwkv.shape[0]
    gw = GROUP_WIDTH
    tm = min(256, seq)
    per_batch = seq // tm
    n_halo = t // 8
    wup = jnp.zeros((2 * RWKV_LORA, 2 * gw), f32)
    aup = jnp.zeros((2 * RWKV_LORA, 2 * gw), f32)
    for e in range(2):
        wup = wup.at[e * RWKV_LORA:(e + 1) * RWKV_LORA, e * gw:(e + 1) * gw].set(w_up[e])
        aup = aup.at[e * RWKV_LORA:(e + 1) * RWKV_LORA, e * gw:(e + 1) * gw].set(a_up[e])
    head_id = jnp.arange(gw) // HEAD_DIM
    bd = (head_id[:, None] == head_id[None, :]).astype(bf16)
    vec = lambda a: a.reshape(1, -1).astype(f32)
    const = lambda a: pl.BlockSpec(a.shape, lambda i: (0,) * a.ndim)
    consts = [vec(mu), w0.astype(f32), wup.astype(bf16), a0.astype(f32), aup.astype(bf16),
              g_up.astype(bf16), vec(k_k), vec(k_a), vec(r_k), bd]
    row = pl.BlockSpec((tm, gw), lambda i: (i, 0))
    row2 = pl.BlockSpec((2, tm, gw), lambda i: (0, i, 0))
    one = jax.ShapeDtypeStruct((t, gw), f32)
    two = jax.ShapeDtypeStruct((2, t, gw), f32)
    r, v, an, g, bonus, lw, kd, bdir = pl.pallas_call(
        functools.partial(_rwkv_prep_kernel, per_batch=per_batch), grid=(t // tm,),
        in_specs=[pl.BlockSpec((tm, RWKV_COLS), lambda i: (i, 0)),
                  pl.BlockSpec((8, RWKV_COLS), lambda i: (jnp.maximum(i * (tm // 8) - 1, 0), 0)),
                  pl.BlockSpec((8, RWKV_COLS), lambda i: (jnp.minimum((i + 1) * (tm // 8), n_halo - 1), 0)),
                  ] + [const(a) for a in consts],
        out_specs=[row, row, row, row, row, row2, row2, row2],
        out_shape=[one, one, one, one, one, two, two, two],
        compiler_params=_cparams("parallel"), name="rwkv_prep",
    )(h_rwkv, h_rwkv, h_rwkv, *consts)

    L = RWKV_CHUNK
    ch = RWKV_CHUNKS_PER_STEP
    rows = ch * L
    n_steps = seq // rows
    n_grp = gw // RWKV_GROUP_LANES
    r3, v3, an3 = (a.reshape(batch, seq, gw) for a in (r, v, an))
    lw4, kd4, bd4 = (a.reshape(2, batch, seq, gw) for a in (lw, kd, bdir))
    fwd = pl.BlockSpec((batch, rows, gw), lambda c: (0, c, 0))
    bwd = pl.BlockSpec((batch, rows, gw), lambda c: (0, n_steps - 1 - c, 0))
    fwd_dir = pl.BlockSpec((1, batch, rows, gw), lambda c: (0, 0, c, 0))
    bwd_dir = pl.BlockSpec((1, batch, rows, gw), lambda c: (1, 0, n_steps - 1 - c, 0))
    o3 = jax.ShapeDtypeStruct((batch, seq, gw), f32)
    o_fwd, o_bwd = pl.pallas_call(
        functools.partial(_rwkv_chunk_kernel, batch=batch, ch=ch), grid=(n_steps,),
        in_specs=[fwd, bwd, fwd, bwd, fwd, bwd, fwd_dir, bwd_dir, fwd_dir, bwd_dir, fwd_dir, bwd_dir],
        out_specs=[fwd, bwd], out_shape=[o3, o3],
        scratch_shapes=[pltpu.VMEM((2 * batch * n_grp, RWKV_GROUP_LANES, RWKV_GROUP_LANES), f32)],
        compiler_params=_cparams("arbitrary"), name="rwkv_chunk",
    )(r3, r3, v3, v3, an3, an3, lw4, lw4, kd4, kd4, bd4, bd4)

    tp = min(512, seq)
    rowp = pl.BlockSpec((tp, gw), lambda i: (i, 0))
    cvec = pl.BlockSpec((1, gw), lambda i: (0, 0))
    return pl.pallas_call(
        _rwkv_post_kernel, grid=(t // tp,),
        in_specs=[rowp, rowp, rowp, rowp, cvec, cvec, pl.BlockSpec((gw, gw), lambda i: (0, 0))],
        out_specs=rowp, out_shape=one,
        compiler_params=_cparams("parallel"), name="rwkv_post",
    )(o_fwd.reshape(t, gw), o_bwd.reshape(t, gw), bonus, g, vec(ln_w), vec(ln_b), bd)


def _s5_operators(a_re, a_im, log_dt, b_re, b_im, c_re, c_im):
    L = S5_CHUNK
    lam_r, lam_i = a_re.astype(f32), a_im.astype(f32)
    dt = jnp.exp(log_dt.astype(f32))[..., None]
    mag = jnp.exp(lam_r * dt)
    abar_r, abar_i = mag * jnp.cos(lam_i * dt), mag * jnp.sin(lam_i * dt)
    den = lam_r * lam_r + lam_i * lam_i
    nr, ni = abar_r - 1.0, abar_i
    coef_r = (nr * lam_r + ni * lam_i) / den
    coef_i = (ni * lam_r - nr * lam_i) / den
    br, bi = b_re.astype(f32)[None], b_im.astype(f32)[None]
    bb_r = coef_r[..., None] * br - coef_i[..., None] * bi
    bb_i = coef_r[..., None] * bi + coef_i[..., None] * br
    steps = jnp.arange(L + 1, dtype=f32)[:, None, None, None]
    pmag = jnp.exp(steps * (lam_r * dt)[None])
    pw_r = pmag * jnp.cos(steps * (lam_i * dt)[None])
    pw_i = pmag * jnp.sin(steps * (lam_i * dt)[None])
    cr, ci = c_re.astype(f32)[None, None], c_im.astype(f32)[None, None]
    cp_r = cr * pw_r[:, :, :, None, :] - ci * pw_i[:, :, :, None, :]
    cp_i = cr * pw_i[:, :, :, None, :] + ci * pw_r[:, :, :, None, :]
    kern = (jnp.einsum('kegop,egpi->kegoi', cp_r, bb_r) - jnp.einsum('kegop,egpi->kegoi', cp_i, bb_i))
    idx = jnp.arange(L)
    kf, kb = kern[:L, 0], kern[:L, 1]
    by_lag = jnp.concatenate([kb[1:][::-1], kf[:1] + kb[:1], kf[1:], jnp.zeros_like(kf[:1])], axis=0)
    skew = jnp.tile(by_lag, (L, 1, 1, 1))[:L * (2 * L - 1)]
    toep = skew.reshape((L, 2 * L - 1) + by_lag.shape[1:])[:, L - 1:]
    m_op = jnp.transpose(toep, (2, 0, 4, 1, 3))
    g_, h_ = S5_GROUPS, S5_GROUP
    m_op = m_op.reshape(g_, L * h_, L * h_)

    def state_in(pr, pi, e):
        re = pr[..., None] * bb_r[e][None] - pi[..., None] * bb_i[e][None]
        im = pr[..., None] * bb_i[e][None] + pi[..., None] * bb_r[e][None]
        return jnp.transpose(re, (1, 0, 3, 2)), jnp.transpose(im, (1, 0, 3, 2))

    f_re, f_im = state_in(pw_r[L - 1 - idx, 0], pw_i[L - 1 - idx, 0], 0)
    b_re_, b_im_ = state_in(pw_r[idx, 1], pw_i[idx, 1], 1)
    b_sum = jnp.concatenate([f_re, f_im, b_re_, b_im_], axis=-1).reshape(g_, L * h_, 4 * S5_STATE)

    def state_out(k_idx, e):
        re = jnp.transpose(cp_r[k_idx, e], (1, 3, 0, 2))
        im = jnp.transpose(cp_i[k_idx, e], (1, 3, 0, 2))
        return re, -im

    fo_r, fo_i = state_out(idx + 1, 0)
    bo_r, bo_i = state_out(L - idx, 1)
    c_out = jnp.concatenate([fo_r, fo_i, bo_r, bo_i], axis=1).reshape(g_, 4 * S5_STATE, L * h_)
    a_pow = jnp.stack([jnp.stack([pw_r[L, e], pw_i[L, e]]) for e in range(2)])
    return m_op.astype(bf16), b_sum.astype(bf16), c_out.astype(bf16), a_pow


def _s5_state_in_kernel(u_ref, bsum_ref, s_ref):
    s_ref[0] = _dot(u_ref[0], bsum_ref[0])


def _s5_carry_kernel(sf_r, sf_i, sb_r, sb_i, af_r, af_i, ab_r, ab_i, xf_r, xf_i, xb_r, xb_i, *, nch):
    rows, p = af_r.shape
    zero = jnp.zeros((rows, p), f32)

    def body(i, carry):
        fr, fi, br, bi = carry
        j = nch - 1 - i
        xf_r[i] = fr
        xf_i[i] = fi
        xb_r[j] = br
        xb_i[j] = bi
        ar, ai = af_r[...], af_i[...]
        nfr = ar * fr - ai * fi + sf_r[i]
        nfi = ar * fi + ai * fr + sf_i[i]
        ar, ai = ab_r[...], ab_i[...]
        nbr = ar * br - ai * bi + sb_r[j]
        nbi = ar * bi + ai * br + sb_i[j]
        return nfr, nfi, nbr, nbi

    lax.fori_loop(0, nch, body, (zero, zero, zero, zero))


def _s5_out_kernel(u_ref, x_ref, m_ref, cout_ref, y_ref):
    y_ref[0] = _dot(u_ref[0], m_ref[0]) + _dot(x_ref[0], cout_ref[0])


def _s5_post_kernel(y_ref, u_ref, d_ref, w_ref, b_ref, o_ref):
    u = u_ref[...]
    z = y_ref[...] + d_ref[...] * u
    c0 = math.sqrt(2.0 / math.pi)
    z = 0.5 * z * (1.0 + jnp.tanh(c0 * (z + 0.044715 * (z * z * z))))
    gate = _dot(z.astype(bf16), w_ref[...]) + b_ref[...]
    o_ref[...] = z * _sigmoid(gate)


def _s5_mixer(u, a_re, a_im, log_dt, b_re, b_im, c_re, c_im, d_skip, glu_w, glu_b, batch, seq):
    t = u.shape[0]
    L, g_, h_, p = S5_CHUNK, S5_GROUPS, S5_GROUP, S5_STATE
    nch = seq // L
    rows = batch * nch
    m_op, b_sum, c_out, a_pow = _s5_operators(a_re, a_im, log_dt, b_re, b_im, c_re, c_im)
    ug = jnp.transpose(u.reshape(rows, L, g_, h_), (2, 0, 1, 3)).reshape(g_, rows, L * h_).astype(bf16)
    grp = lambda r, c: pl.BlockSpec((1, r, c), lambda g: (g, 0, 0))
    s_in = pl.pallas_call(
        _s5_state_in_kernel, grid=(g_,),
        in_specs=[grp(rows, L * h_), grp(L * h_, 4 * p)], out_specs=grp(rows, 4 * p),
        out_shape=jax.ShapeDtypeStruct((g_, rows, 4 * p), f32),
        compiler_params=_cparams("parallel"), name="s5_state_in",
    )(ug, b_sum)
    s_in = jnp.transpose(s_in.reshape(g_, batch, nch, 4, p), (3, 2, 1, 0, 4)).reshape(4, nch, batch * g_, p)
    coef = jnp.broadcast_to(a_pow[:, :, None], (2, 2, batch, g_, p)).reshape(2, 2, batch * g_, p)
    st = jax.ShapeDtypeStruct((nch, batch * g_, p), f32)
    rb = 8
    seq_spec = pl.BlockSpec((nch, rb, p), lambda i: (0, i, 0))
    coef_spec = pl.BlockSpec((rb, p), lambda i: (i, 0))
    xs = pl.pallas_call(
        functools.partial(_s5_carry_kernel, nch=nch), grid=(batch * g_ // rb,),
        in_specs=[seq_spec] * 4 + [coef_spec] * 4, out_specs=[seq_spec] * 4, out_shape=[st] * 4,
        compiler_params=_cparams("parallel"), name="s5_carry",
    )(s_in[0], s_in[1], s_in[2], s_in[3], coef[0, 0], coef[0, 1], coef[1, 0], coef[1, 1])
    x_cat = jnp.stack(xs).reshape(4, nch, batch, g_, p)
    x_cat = jnp.transpose(x_cat, (3, 2, 1, 0, 4)).reshape(g_, rows, 4 * p).astype(bf16)
    y = pl.pallas_call(
        _s5_out_kernel, grid=(g_,),
        in_specs=[grp(rows, L * h_), grp(rows, 4 * p), grp(L * h_, L * h_), grp(4 * p, L * h_)],
        out_specs=grp(rows, L * h_),
        out_shape=jax.ShapeDtypeStruct((g_, rows, L * h_), f32),
        compiler_params=_cparams("parallel"), name="s5_out",
    )(ug, x_cat, m_op, c_out)
    y = jnp.transpose(y.reshape(g_, rows, L, h_), (1, 2, 0, 3)).reshape(t, g_ * h_)
    tm = min(512, seq)
    w = g_ * h_
    row = pl.BlockSpec((tm, w), lambda i: (i, 0))
    cvec = pl.BlockSpec((1, w), lambda i: (0, 0))
    return pl.pallas_call(
        _s5_post_kernel, grid=(t // tm,),
        in_specs=[row, row, cvec, pl.BlockSpec((w, w), lambda i: (0, 0)), cvec],
        out_specs=row, out_shape=jax.ShapeDtypeStruct((t, w), f32),
        compiler_params=_cparams("parallel"), name="s5_post",
    )(y, u, d_skip.reshape(1, w).astype(f32), glu_w.astype(bf16), glu_b.reshape(1, w).astype(f32))


def _outproj_kernel(o0_ref, o1_ref, o2_ref, o3_ref, w_ref, x_ref, g_ref, out_ref):
    acc = None
    for i, o_ref in enumerate((o0_ref, o1_ref, o2_ref, o3_ref)):
        part = _dot(o_ref[...].astype(bf16), w_ref[i * GROUP_WIDTH:(i + 1) * GROUP_WIDTH, :])
        acc = part if acc is None else acc + part
    out_ref[...] = x_ref[...] + g_ref[0] * acc


def _outproj(outs, w_out, x, gate, seq):
    t, d = x.shape
    tm = min(512, seq)
    tn = d
    per_batch = seq // tm
    part = pl.BlockSpec((tm, GROUP_WIDTH), lambda i, j: (i, 0))
    return pl.pallas_call(
        _outproj_kernel, grid=(t // tm, d // tn),
        in_specs=[part, part, part, part,
                  pl.BlockSpec((4 * GROUP_WIDTH, tn), lambda i, j: (0, j)),
                  pl.BlockSpec((tm, tn), lambda i, j: (i, j)),
                  pl.BlockSpec((1, 1, tn), lambda i, j: (i // per_batch, 0, j))],
        out_specs=pl.BlockSpec((tm, tn), lambda i, j: (i, j)),
        out_shape=jax.ShapeDtypeStruct((t, d), f32),
        compiler_params=_cparams("parallel", "parallel"), name="out_proj",
    )(*outs, w_out.astype(bf16), x, gate)


def _moe_ffn_kernel(xs_ref, gate_ref, g2_ref, w1_ref, w3_ref, w2_ref, o_ref, hid_scr, *, n_up, tf):
    step = pl.program_id(2)

    @pl.when(step < n_up)
    def _():
        xs = xs_ref[0, 0]
        a = _dot(xs, w1_ref[0, 0].astype(bf16))
        b = _dot(xs, w3_ref[0, 0].astype(bf16))
        col = pl.multiple_of(step * tf, tf)
        hid_scr[:, pl.ds(col, tf)] = (a * _sigmoid(a) * b).astype(bf16)

    @pl.when(step >= n_up)
    def _():
        o_ref[0, 0] = _dot(hid_scr[...], w2_ref[0, 0].astype(bf16)) * gate_ref[0, 0] * g2_ref[0]


def _moe_ffn(xs, gate, g2, w1, w3, w2, layer):
    batch, n_exp, cap, d = xs.shape
    ff = w1.shape[3]
    tf = 256
    tn = 512
    n_up, n_down = ff // tf, d // tn
    up = lambda s: jnp.minimum(s, n_up - 1)
    down = lambda s: jnp.maximum(s - n_up, 0)
    return pl.pallas_call(
        functools.partial(_moe_ffn_kernel, n_up=n_up, tf=tf), grid=(n_exp, batch, n_up + n_down),
        in_specs=[pl.BlockSpec((1, 1, cap, d), lambda e, b, s: (b, e, 0, 0)),
                  pl.BlockSpec((1, 1, cap, 1), lambda e, b, s: (b, e, 0, 0)),
                  pl.BlockSpec((1, 1, tn), lambda e, b, s: (b, 0, down(s))),
                  pl.BlockSpec((1, 1, d, tf), lambda e, b, s: (layer, e, 0, up(s))),
                  pl.BlockSpec((1, 1, d, tf), lambda e, b, s: (layer, e, 0, up(s))),
                  pl.BlockSpec((1, 1, ff, tn), lambda e, b, s: (layer, e, 0, down(s)))],
        out_specs=pl.BlockSpec((1, 1, cap, tn), lambda e, b, s: (b, e, 0, down(s))),
        out_shape=jax.ShapeDtypeStruct((batch, n_exp, cap, d), f32),
        scratch_shapes=[pltpu.VMEM((cap, ff), bf16)],
        compiler_params=_cparams("parallel", "parallel", "arbitrary"), name="moe_ffn",
    )(xs, gate, g2, w1, w3, w2)


def _moe_combine_kernel(idx_ref, ys_ref, x_hbm, out_hbm, slab, sem, *, width, group):
    b, h, e = pl.program_id(0), pl.program_id(1), pl.program_id(2)
    cols = pl.ds(pl.multiple_of(h * width, width), width)

    def slab_copy(src, dst):
        return pltpu.make_async_copy(src, dst, sem)

    @pl.when(e == 0)
    def _():
        cp = slab_copy(x_hbm.at[b, :, cols], slab)
        cp.start()
        cp.wait()

    cap = ys_ref.shape[2]

    def body(g, carry):
        r0 = pl.multiple_of(g * group, group)
        tok = [idx_ref[0, 0, r0 + k] for k in range(group)]
        rows = [slab[pl.ds(tok[k], 1), :] + ys_ref[0, 0, pl.ds(r0 + k, 1), :] for k in range(group)]
        for k in range(group):
            slab[pl.ds(tok[k], 1), :] = rows[k]
        return carry

    lax.fori_loop(0, cap // group, body, 0)

    @pl.when(e == pl.num_programs(2) - 1)
    def _():
        cp = slab_copy(slab, out_hbm.at[b, :, cols])
        cp.start()
        cp.wait()


def _moe_combine(x, ys, idx):
    batch, seq, d = x.shape
    n_exp, cap = idx.shape[1], idx.shape[2]
    width = d // 2
    return pl.pallas_call(
        functools.partial(_moe_combine_kernel, width=width, group=8),
        grid=(batch, d // width, n_exp),
        in_specs=[pl.BlockSpec((1, 1, cap), lambda b, h, e: (b * n_exp + e, 0, 0), memory_space=pltpu.SMEM),
                  pl.BlockSpec((1, 1, cap, width), lambda b, h, e: (b, e, 0, h)),
                  pl.BlockSpec(memory_space=pl.ANY)],
        out_specs=pl.BlockSpec(memory_space=pl.ANY),
        out_shape=jax.ShapeDtypeStruct((batch, seq, d), f32),
        scratch_shapes=[pltpu.VMEM((seq, width), f32), pltpu.SemaphoreType.DMA(())],
        compiler_params=_cparams("arbitrary", "arbitrary", "arbitrary"), name="moe_combine",
    )(idx.reshape(batch * n_exp, 1, cap), ys, x)


def _moe_residual(x, xn, logits, g2, w1, w3, w2, layer, batch, seq):
    d = xn.shape[1]
    capacity = EC_CAPACITY_FACTOR * seq // N_EXPERTS
    aff = jax.nn.softmax(logits.reshape(batch, seq, N_EXPERTS), axis=-1)
    gate, idx = lax.top_k(jnp.swapaxes(aff, 1, 2), capacity)
    bidx = jnp.arange(batch)[:, None, None]
    xs = xn.reshape(batch, seq, d)[bidx, idx]
    ys = _moe_ffn(xs, gate[..., None], g2, w1, w3, w2, layer)
    return _moe_combine(x.reshape(batch, seq, d), ys, idx).reshape(batch * seq, d)


def _in_proj_weights(w_in_l):
    o1, o2, o3 = SWA_COLS, SWA_COLS + MLA_COLS, SWA_COLS + MLA_COLS + RWKV_COLS
    w_mla = jnp.zeros((w_in_l.shape[0], MLA_COLS_PAD), f32).at[:, :MLA_COLS].set(w_in_l[:, o1:o2])
    return (w_in_l[:, :o1].astype(bf16), w_mla.astype(bf16),
            w_in_l[:, o2:o3].astype(bf16), w_in_l[:, o3:].astype(bf16))


def kernel(x, c, positions, ada_w, ada_b, norm1_g, norm2_g, w_in, w_out, swa_q_gain, swa_k_gain, swa_sink, mla_q_a_gain, mla_kv_a_gain, mla_w_uq, mla_w_ukv, mla_q_gain, mla_k_gain, rwkv_mu, rwkv_w0, rwkv_w_up, rwkv_a0, rwkv_a_up, rwkv_g_up, rwkv_k_k, rwkv_k_a, rwkv_r_k, rwkv_ln_w, rwkv_ln_b, s5_a_re, s5_a_im, s5_log_dt, s5_b_re, s5_b_im, s5_c_re, s5_c_im, s5_d, s5_glu_w, s5_glu_b, router_w, moe_w1, moe_w3, moe_w2):
    batch, seq, d = x.shape
    depth = ada_w.shape[0]
    t = batch * seq
    cos_h, sin_h, cos_r, sin_r = _rope_tables(positions)
    mod = _ada_mod(c, ada_w, ada_b)
    xf = x.reshape(t, d)
    for l in range(depth):
        sh1, sc1, g1, sh2, sc2, g2 = [m[:, None, :] for m in jnp.split(mod[l], 6, axis=-1)]
        xn = _norm_call(xf, norm1_g[l], sc1, sh1, seq)
        w_swa, w_mla, w_rwkv, w_s5 = _in_proj_weights(w_in[l])
        h_swa = _matmul(xn, w_swa, tm=1024, tn=SWA_COLS, name="in_proj_swa")
        h_mla = _matmul(xn, w_mla, tm=1024, tn=MLA_COLS_PAD, name="in_proj_mla")
        h_rwkv = _matmul(xn, w_rwkv, tm=1024, tn=640, name="in_proj_rwkv")
        h_s5 = _matmul(xn, w_s5, tm=1024, tn=GROUP_WIDTH, name="in_proj_s5")
        o_swa = _swa_mixer(h_swa, cos_h, sin_h, swa_q_gain[l], swa_k_gain[l], swa_sink[l], batch, seq)
        o_mla = _mla_mixer(h_mla, cos_r, sin_r, mla_q_a_gain[l], mla_kv_a_gain[l], mla_w_uq[l],
                           mla_w_ukv[l], mla_q_gain[l], mla_k_gain[l], batch, seq)
        o_rwkv = _rwkv_mixer(h_rwkv, rwkv_mu[l], rwkv_w0[l], rwkv_w_up[l], rwkv_a0[l], rwkv_a_up[l],
                             rwkv_g_up[l], rwkv_k_k[l], rwkv_k_a[l], rwkv_r_k[l],
                             rwkv_ln_w[l], rwkv_ln_b[l], batch, seq)
        o_s5 = _s5_mixer(h_s5, s5_a_re[l], s5_a_im[l], s5_log_dt[l], s5_b_re[l], s5_b_im[l],
                         s5_c_re[l], s5_c_im[l], s5_d[l], s5_glu_w[l], s5_glu_b[l], batch, seq)
        xf = _outproj((o_swa, o_mla, o_rwkv, o_s5), w_out[l], xf, g1, seq)
        xn2, logits = _norm_call(xf, norm2_g[l], sc2, sh2, seq, router_w=router_w[l])
        xf = _moe_residual(xf, xn2, logits, g2, moe_w1, moe_w3, moe_w2, l, batch, seq)
    return xf.reshape(batch, seq, d)
```

```python
import functools
import math

import jax
import jax.numpy as jnp
from jax import lax
from jax.experimental import pallas as pl
from jax.experimental.pallas import tpu as pltpu

f32 = jnp.float32
bf16 = jnp.bfloat16

D_MODEL = 2048
HEAD_DIM = 64
GROUP_WIDTH = 512
ROPE_THETA = 10000.0
NORM_EPS = 1e-6
NEG_INF = -1e30

SWA_HEADS = 8
SWA_KV_HEADS = 2
SWA_WINDOW = 128
SWA_BLOCK = 128
SWA_Q = 512
SWA_KV = 128
SWA_COLS = 768
SWA_BLOCKS_PER_STEP = 4

MLA_HEADS = 8
MLA_NOPE = 64
MLA_ROPE = 32
MLA_V = 64
MLA_QK = 96
MLA_Q_RANK = 384
MLA_KV_RANK = 128
MLA_COLS = 544
MLA_COLS_PAD = 640
MLA_HEAD_PAD = 128

RWKV_HEADS = 8
RWKV_LORA = 64
RWKV_GATE_LORA = 128
RWKV_GN_EPS = 64e-5
RWKV_COLS = 1920
RWKV_CHUNK = 64
RWKV_CHUNKS_PER_STEP = 2
RWKV_GROUP_LANES = 256

S5_GROUP = 16
S5_GROUPS = 32
S5_STATE = 64
S5_CHUNK = 32

N_EXPERTS = 16
EC_CAPACITY_FACTOR = 2
D_FF_EXPERT = 1024

VMEM_LIMIT_BYTES = 52 * 1024 * 1024
LANES = 128


def _cparams(*sem):
    return pltpu.CompilerParams(dimension_semantics=sem, vmem_limit_bytes=VMEM_LIMIT_BYTES)


def _sigmoid(x):
    return 1.0 / (1.0 + jnp.exp(-x))


def _split_bf16(x):
    hi = x.astype(bf16)
    lo = (x - hi.astype(f32)).astype(bf16)
    return hi, lo


def _dot(a, b):
    return jnp.dot(a, b, preferred_element_type=f32)


def _dot_nt(a, b):
    return lax.dot_general(a, b, (((1,), (1,)), ((), ())), preferred_element_type=f32)


def _dot_tn(a, b):
    return lax.dot_general(a, b, (((0,), (0,)), ((), ())), preferred_element_type=f32)


def _ada_kernel(c_ref, w_ref, b_ref, o_ref):
    c = c_ref[...]
    cond = (c * _sigmoid(c)).astype(bf16)
    o_ref[0] = _dot(cond, w_ref[0].astype(bf16)) + b_ref[0]


def _ada_mod(c, ada_w, ada_b):
    depth, d, n = ada_w.shape
    b = c.shape[0]
    rows = 8
    c_pad = jnp.zeros((rows, d), f32).at[:b].set(c)
    tn = 512
    out = pl.pallas_call(
        _ada_kernel,
        grid=(depth, n // tn),
        in_specs=[
            pl.BlockSpec((rows, d), lambda l, j: (0, 0)),
            pl.BlockSpec((1, d, tn), lambda l, j: (l, 0, j)),
            pl.BlockSpec((1, 1, tn), lambda l, j: (l, 0, j)),
        ],
        out_specs=pl.BlockSpec((1, rows, tn), lambda l, j: (l, 0, j)),
        out_shape=jax.ShapeDtypeStruct((depth, rows, n), f32),
        compiler_params=_cparams("parallel", "parallel"),
        name="ada_mod",
    )(c_pad, ada_w, ada_b.reshape(depth, 1, n))
    return out[:, :b]


def _mm_kernel(x_ref, w_ref, o_ref):
    o_ref[...] = _dot(x_ref[...].astype(bf16), w_ref[...].astype(bf16)).astype(o_ref.dtype)


def _matmul(x, w, *, tm, tn, name):
    m, k = x.shape
    n = w.shape[1]
    tm = min(tm, m)
    return pl.pallas_call(
        _mm_kernel,
        grid=(m // tm, n // tn),
        in_specs=[pl.BlockSpec((tm, k), lambda i, j: (i, 0)),
                  pl.BlockSpec((k, tn), lambda i, j: (0, j))],
        out_specs=pl.BlockSpec((tm, tn), lambda i, j: (i, j)),
        out_shape=jax.ShapeDtypeStruct((m, n), f32),
        compiler_params=_cparams("parallel", "parallel"),
        name=name,
    )(x, w)


def _norm_mod(x, g_ref, sc_ref, sh_ref):
    ms = jnp.mean(x * x, axis=-1, keepdims=True)
    y = x * lax.rsqrt(ms + NORM_EPS) * g_ref[...]
    return y * (1.0 + sc_ref[0]) + sh_ref[0]


def _norm_kernel(x_ref, g_ref, sc_ref, sh_ref, o_ref):
    o_ref[...] = _norm_mod(x_ref[...], g_ref, sc_ref, sh_ref).astype(o_ref.dtype)


def _norm_call(x, gain, scale, shift, seq):
    t, d = x.shape
    tm = min(512, seq)
    per_batch = seq // tm
    row = pl.BlockSpec((tm, d), lambda i: (i, 0))
    mod = pl.BlockSpec((1, 1, d), lambda i: (i // per_batch, 0, 0))
    return pl.pallas_call(
        _norm_kernel, grid=(t // tm,),
        in_specs=[row, pl.BlockSpec((1, d), lambda i: (0, 0)), mod, mod], out_specs=row,
        out_shape=jax.ShapeDtypeStruct((t, d), bf16),
        compiler_params=_cparams("parallel"), name="norm_mod",
    )(x, gain.reshape(1, d), scale, shift)


def _rope_lanes(x, cos_t, sin_t, half, first_mask):
    n = x.shape[-1]
    fwd = pltpu.roll(x, n - half, axis=1)
    bwd = pltpu.roll(x, half, axis=1)
    return x * cos_t + jnp.where(first_mask, fwd, bwd) * sin_t


def _rope_tables(positions):
    t = positions.size
    pos = positions.reshape(t, 1).astype(f32)

    def tables(dim):
        inv_freq = ROPE_THETA ** (-jnp.arange(0, dim, 2, dtype=f32) / dim)
        ang = pos * inv_freq
        return jnp.cos(ang), jnp.sin(ang)

    c, s = tables(HEAD_DIM)
    cos_h = jnp.concatenate([c, c, c, c], axis=-1)
    sin_h = jnp.concatenate([-s, s, -s, s], axis=-1)
    c, s = tables(MLA_ROPE)
    one = jnp.ones((t, MLA_NOPE), f32)
    zero = jnp.zeros((t, MLA_NOPE), f32)
    pad1 = jnp.ones((t, MLA_HEAD_PAD - MLA_QK), f32)
    pad0 = jnp.zeros((t, MLA_HEAD_PAD - MLA_QK), f32)
    cos_r = jnp.concatenate([one, c, c, pad1], axis=-1)
    sin_r = jnp.concatenate([zero, -s, s, pad0], axis=-1)
    return cos_h, sin_h, cos_r, sin_r


def _pair_rmsnorm(x, gain):
    lane = lax.broadcasted_iota(jnp.int32, x.shape, 1)
    lo = lane < HEAD_DIM
    x2 = x * x
    s_lo = jnp.sum(jnp.where(lo, x2, 0.0), axis=-1, keepdims=True)
    s_hi = jnp.sum(jnp.where(lo, 0.0, x2), axis=-1, keepdims=True)
    ms = jnp.where(lo, s_lo, s_hi) * (1.0 / HEAD_DIM)
    return x * lax.rsqrt(ms + NORM_EPS) * gain


def _swa_prep_kernel(h_ref, cos_ref, sin_ref, qg_ref, kg_ref, q_ref, k_ref, v_ref):
    cos_t = cos_ref[...]
    sin_t = sin_ref[...]
    lane = lax.broadcasted_iota(jnp.int32, cos_t.shape, 1)
    first = (lane % HEAD_DIM) < (HEAD_DIM // 2)
    scale = HEAD_DIM ** -0.5
    for p in range(SWA_Q // LANES):
        x = _pair_rmsnorm(h_ref[:, p * LANES:(p + 1) * LANES], qg_ref[...])
        x = _rope_lanes(x, cos_t, sin_t, HEAD_DIM // 2, first)
        q_ref[:, p * LANES:(p + 1) * LANES] = (x * scale).astype(bf16)
    x = _pair_rmsnorm(h_ref[:, SWA_Q:SWA_Q + SWA_KV], kg_ref[...])
    k_ref[...] = _rope_lanes(x, cos_t, sin_t, HEAD_DIM // 2, first).astype(bf16)
    v_ref[...] = h_ref[:, SWA_Q + SWA_KV:SWA_COLS].astype(bf16)


def _swa_attn_kernel(q_ref, k_ref, v_ref, sink_ref, o_ref, *, seq, nblk):
    step = pl.program_id(1)
    band = 3 * SWA_BLOCK
    grp = SWA_HEADS // SWA_KV_HEADS
    rows = grp * SWA_BLOCK
    row_in_blk = lax.broadcasted_iota(jnp.int32, (rows, band), 0) % SWA_BLOCK
    col = lax.broadcasted_iota(jnp.int32, (rows, band), 1)
    sinks = [jnp.concatenate(
        [jnp.broadcast_to(sink_ref[kh * grp + g:kh * grp + g + 1, 0:1], (SWA_BLOCK, 1)) for g in range(grp)],
        axis=0) for kh in range(SWA_KV_HEADS)]
    qs, ks, vs, valid, sink = [], [], [], [], []
    for i in range(nblk):
        n = step * nblk + i
        start = pl.multiple_of(jnp.clip((n - 1) * SWA_BLOCK, 0, seq - band), SWA_BLOCK)
        kb = k_ref[pl.ds(start, band), :]
        vb = v_ref[pl.ds(start, band), :]
        q = q_ref[i * SWA_BLOCK:(i + 1) * SWA_BLOCK, :]
        ok = jnp.abs(n * SWA_BLOCK + row_in_blk - (start + col)) <= SWA_WINDOW
        for kh in range(SWA_KV_HEADS):
            qs.append(jnp.concatenate(
                [q[:, (kh * grp + g) * HEAD_DIM:(kh * grp + g + 1) * HEAD_DIM] for g in range(grp)], axis=0))
            ks.append(kb[:, kh * HEAD_DIM:(kh + 1) * HEAD_DIM])
            vs.append(vb[:, kh * HEAD_DIM:(kh + 1) * HEAD_DIM])
            valid.append(ok)
            sink.append(sinks[kh])
    sink = jnp.stack(sink)
    s = jnp.where(jnp.stack(valid), _bmm_nt(jnp.stack(qs), jnp.stack(ks)), NEG_INF)
    m = jnp.maximum(jnp.max(s, axis=-1, keepdims=True), sink)
    p = jnp.exp(s - m)
    denom = jnp.sum(p, axis=-1, keepdims=True) + jnp.exp(sink - m)
    o = _bmm(p.astype(bf16), jnp.stack(vs)) / denom
    for i in range(nblk):
        o_ref[i * SWA_BLOCK:(i + 1) * SWA_BLOCK, :] = jnp.concatenate(
            [o[i * SWA_KV_HEADS + kh, g * SWA_BLOCK:(g + 1) * SWA_BLOCK]
             for kh in range(SWA_KV_HEADS) for g in range(grp)], axis=-1)


def _swa_mixer(h_swa, cos_h, sin_h, q_gain, k_gain, sink, batch, seq):
    t = h_swa.shape[0]
    tm = min(512, seq)
    row = lambda w: pl.BlockSpec((tm, w), lambda i: (i, 0))
    const = pl.BlockSpec((1, LANES), lambda i: (0, 0))
    q, k, v = pl.pallas_call(
        _swa_prep_kernel, grid=(t // tm,),
        in_specs=[row(SWA_COLS), row(LANES), row(LANES), const, const],
        out_specs=[row(SWA_Q), row(SWA_KV), row(SWA_KV)],
        out_shape=[jax.ShapeDtypeStruct((t, SWA_Q), bf16),
                   jax.ShapeDtypeStruct((t, SWA_KV), bf16),
                   jax.ShapeDtypeStruct((t, SWA_KV), bf16)],
        compiler_params=_cparams("parallel"), name="swa_prep",
    )(h_swa, cos_h, sin_h, jnp.tile(q_gain, 2).reshape(1, LANES), jnp.tile(k_gain, 2).reshape(1, LANES))
    nblk = SWA_BLOCKS_PER_STEP
    tq = nblk * SWA_BLOCK
    nb = seq // tq
    sink_t = jnp.broadcast_to(sink.astype(f32).reshape(SWA_HEADS, 1), (SWA_HEADS, LANES))
    return pl.pallas_call(
        functools.partial(_swa_attn_kernel, seq=seq, nblk=nblk), grid=(batch, nb),
        in_specs=[pl.BlockSpec((tq, SWA_Q), lambda b, n: (b * nb + n, 0)),
                  pl.BlockSpec((seq, SWA_KV), lambda b, n: (b, 0)),
                  pl.BlockSpec((seq, SWA_KV), lambda b, n: (b, 0)),
                  pl.BlockSpec((SWA_HEADS, LANES), lambda b, n: (0, 0))],
        out_specs=pl.BlockSpec((tq, SWA_Q), lambda b, n: (b * nb + n, 0)),
        out_shape=jax.ShapeDtypeStruct((t, SWA_Q), f32),
        compiler_params=_cparams("parallel", "parallel"), name="swa_attn",
    )(q, k, v, sink_t)


def _mla_prep_kernel(h_ref, cos_ref, sin_ref, qag_ref, kvag_ref, wq_ref, wk_ref, wpe_ref, wv_ref,
                     qg_ref, kg_ref, q_ref, k_ref, v_ref):
    cos_t = cos_ref[...]
    sin_t = sin_ref[...]
    lane = lax.broadcasted_iota(jnp.int32, cos_t.shape, 1)
    first = lane < MLA_NOPE + MLA_ROPE // 2
    cq = h_ref[:, :MLA_Q_RANK]
    cq = cq * lax.rsqrt(jnp.mean(cq * cq, axis=-1, keepdims=True) + NORM_EPS) * qag_ref[...]
    ckv = h_ref[:, MLA_Q_RANK:MLA_Q_RANK + MLA_KV_RANK]
    ckv = (ckv * lax.rsqrt(jnp.mean(ckv * ckv, axis=-1, keepdims=True) + NORM_EPS) * kvag_ref[...]).astype(bf16)
    kpe = h_ref[:, MLA_Q_RANK + MLA_KV_RANK:MLA_COLS_PAD].astype(bf16)
    q_all = _dot(cq.astype(bf16), wq_ref[...])
    k_all = _dot(ckv, wk_ref[...]) + _dot(kpe, wpe_ref[...])
    v_all = _dot(ckv, wv_ref[...])
    one_lane = lane == MLA_V
    scale = MLA_QK ** -0.5 * math.log2(math.e)

    def head_norm(x, gain):
        ms = jnp.sum(x * x, axis=-1, keepdims=True) * (1.0 / MLA_QK)
        x = x * lax.rsqrt(ms + NORM_EPS) * gain
        return _rope_lanes(x, cos_t, sin_t, MLA_ROPE // 2, first)

    for h in range(MLA_HEADS):
        sl = slice(h * MLA_HEAD_PAD, (h + 1) * MLA_HEAD_PAD)
        q_ref[0, h] = (head_norm(q_all[:, sl], qg_ref[...]) * scale).astype(bf16)
        k_ref[0, h] = head_norm(k_all[:, sl], kg_ref[...]).astype(bf16)
        v_ref[0, h] = jnp.where(one_lane, 1.0, v_all[:, sl]).astype(bf16)


def _mla_attn_kernel(q_ref, k_ref, v_ref, o_ref, *, tk, nk):
    n_heads = q_ref.shape[1]
    tq = q_ref.shape[2]
    qs = [q_ref[0, hh] for hh in range(n_heads)]

    def body(j, carry):
        off = pl.multiple_of(j * tk, tk)
        new = []
        for hh in range(n_heads):
            m, acc = carry[hh]
            kj = k_ref[0, hh, pl.ds(off, tk), :]
            vj = v_ref[0, hh, pl.ds(off, tk), :]
            s = _dot_nt(qs[hh], kj)
            m_new = jnp.maximum(m, jnp.max(s, axis=-1, keepdims=True))
            alpha = jnp.exp2(m - m_new)
            p = jnp.exp2(s - m_new)
            new.append((m_new, acc * alpha + _dot(p.astype(bf16), vj)))
        return tuple(new)

    init = tuple((jnp.full((tq, 1), NEG_INF, f32), jnp.zeros((tq, MLA_HEAD_PAD), f32))
                 for _ in range(n_heads))
    final = lax.fori_loop(0, nk, body, init)
    o_ref[0] = jnp.concatenate([acc[:, :MLA_V] / acc[:, MLA_V:MLA_V + 1] for _, acc in final], axis=-1)


def _mla_weights(w_uq, w_ukv, q_gain, k_gain):
    hp = MLA_HEAD_PAD
    wq = jnp.zeros((MLA_Q_RANK, MLA_HEADS, hp), f32).at[:, :, :MLA_QK].set(
        w_uq.reshape(MLA_Q_RANK, MLA_HEADS, MLA_QK))
    kv = w_ukv.reshape(MLA_KV_RANK, MLA_HEADS, MLA_NOPE + MLA_V)
    wk = jnp.zeros((MLA_KV_RANK, MLA_HEADS, hp), f32).at[:, :, :MLA_NOPE].set(kv[:, :, :MLA_NOPE])
    wv = jnp.zeros((MLA_KV_RANK, MLA_HEADS, hp), f32).at[:, :, :MLA_V].set(kv[:, :, MLA_NOPE:])
    eye = jnp.eye(MLA_ROPE, dtype=f32)
    wpe = jnp.zeros((LANES, MLA_HEADS, hp), f32).at[:MLA_ROPE, :, MLA_NOPE:MLA_QK].set(
        jnp.broadcast_to(eye[:, None, :], (MLA_ROPE, MLA_HEADS, MLA_ROPE)))
    flat = lambda w: w.reshape(w.shape[0], MLA_HEADS * hp).astype(bf16)
    pad = lambda g: jnp.zeros((1, hp), f32).at[0, :MLA_QK].set(g)
    return flat(wq), flat(wk), flat(wpe), flat(wv), pad(q_gain), pad(k_gain)


def _mla_mixer(h_mla, cos_r, sin_r, q_a_gain, kv_a_gain, w_uq, w_ukv, q_gain, k_gain, batch, seq):
    t = h_mla.shape[0]
    tm = min(512, seq)
    per_batch = seq // tm
    hp = MLA_HEAD_PAD
    wq, wk, wpe, wv, qg, kg = _mla_weights(w_uq, w_ukv, q_gain, k_gain)
    row = lambda w: pl.BlockSpec((tm, w), lambda b, i: (b * per_batch + i, 0))
    const = lambda a: pl.BlockSpec(a.shape, lambda b, i: (0,) * a.ndim)
    head_out = pl.BlockSpec((1, MLA_HEADS, tm, hp), lambda b, i: (b, 0, i, 0))
    head_shape = jax.ShapeDtypeStruct((batch, MLA_HEADS, seq, hp), bf16)
    qag = q_a_gain.reshape(1, MLA_Q_RANK)
    kvag = kv_a_gain.reshape(1, MLA_KV_RANK)
    q, k, v = pl.pallas_call(
        _mla_prep_kernel, grid=(batch, per_batch),
        in_specs=[row(MLA_COLS_PAD), row(LANES), row(LANES), const(qag), const(kvag),
                  const(wq), const(wk), const(wpe), const(wv), const(qg), const(kg)],
        out_specs=[head_out, head_out, head_out],
        out_shape=[head_shape, head_shape, head_shape],
        compiler_params=_cparams("parallel", "parallel"), name="mla_prep",
    )(h_mla, cos_r, sin_r, qag, kvag, wq, wk, wpe, wv, qg, kg)
    tq = min(1024, seq)
    tk = min(2048, seq)
    kv_spec = pl.BlockSpec((1, 2, seq, hp), lambda b, h, i: (b, h, 0, 0))
    o = pl.pallas_call(
        functools.partial(_mla_attn_kernel, tk=tk, nk=seq // tk),
        grid=(batch, MLA_HEADS // 2, seq // tq),
        in_specs=[pl.BlockSpec((1, 2, tq, hp), lambda b, h, i: (b, h, i, 0)), kv_spec, kv_spec],
        out_specs=pl.BlockSpec((1, tq, 2 * MLA_V), lambda b, h, i: (b, i, h)),
        out_shape=jax.ShapeDtypeStruct((batch, seq, MLA_HEADS * MLA_V), f32),
        compiler_params=_cparams("parallel", "parallel", "parallel"), name="mla_attn",
    )(q, k, v)
    return o.reshape(t, MLA_HEADS * MLA_V)


def _head_sum(x, bd):
    hi, lo = _split_bf16(x)
    return _dot(hi, bd) + _dot(lo, bd)


def _rwkv_prep_kernel(h_ref, prev_ref, next_ref, mu_ref, w0_ref, wup_ref, a0_ref, aup_ref, gup_ref,
                      kk_ref, ka_ref, rk_ref, bd_ref,
                      r_ref, v_ref, an_ref, g_ref, bonus_ref, lw_ref, kd_ref, bdir_ref, *, per_batch):
    i = pl.program_id(0)
    x = h_ref[...]
    tm = x.shape[0]
    row = lax.broadcasted_iota(jnp.int32, x.shape, 0)
    prev_row = jnp.where(i % per_batch == 0, 0.0, prev_ref[7:8, :])
    next_row = jnp.where(i % per_batch == per_batch - 1, 0.0, next_ref[0:1, :])
    x_prev = jnp.where(row == 0, prev_row, pltpu.roll(x, 1, axis=0))
    x_next = jnp.where(row == tm - 1, next_row, pltpu.roll(x, tm - 1, axis=0))
    hs = x + mu_ref[...] * (0.5 * (x_prev + x_next) - x)
    gw = GROUP_WIDTH
    r = hs[:, 0:gw]
    k = hs[:, gw:2 * gw]
    v = hs[:, 2 * gw:3 * gw]
    wd = hs[:, 3 * gw:3 * gw + LANES]
    ad = hs[:, 3 * gw + LANES:3 * gw + 2 * LANES]
    gd = hs[:, 3 * gw + 2 * LANES:RWKV_COLS]
    bd = bd_ref[...]
    lora_w = _dot(jnp.tanh(wd).astype(bf16), wup_ref[...])
    lora_a = _dot(ad.astype(bf16), aup_ref[...])
    kk = k * kk_ref[...]
    kk = kk * lax.rsqrt(_head_sum(kk * kk, bd) + 1e-12)
    r_ref[...] = r
    v_ref[...] = v
    an_ref[...] = -kk
    g_ref[...] = _dot(_sigmoid(gd).astype(bf16), gup_ref[...])
    bonus_ref[...] = _head_sum(r * k * rk_ref[...], bd) * v
    for e in range(2):
        z = -(w0_ref[e:e + 1, :] + lora_w[:, e * gw:(e + 1) * gw])
        softplus = jnp.maximum(z, 0.0) + jnp.log(1.0 + jnp.exp(-jnp.abs(z)))
        lw_ref[e] = -jnp.exp(-softplus - 0.5)
        a = _sigmoid(a0_ref[e:e + 1, :] + lora_a[:, e * gw:(e + 1) * gw])
        kd_ref[e] = k * (1.0 + (a - 1.0) * ka_ref[...])
        bdir_ref[e] = kk * a


def _bmm(a, b):
    return lax.dot_general(a, b, (((2,), (1,)), ((0,), (0,))), preferred_element_type=f32)


def _bmm_nt(a, b):
    return lax.dot_general(a, b, (((2,), (2,)), ((0,), (0,))), preferred_element_type=f32)


def _bmm_tn(a, b):
    return lax.dot_general(a, b, (((1,), (1,)), ((0,), (0,))), preferred_element_type=f32)


def _rwkv_chunk_kernel(r_f, r_b, v_f, v_b, an_f, an_b, lw_f, lw_b, kd_f, kd_b, bd_f, bd_b,
                       o_f, o_b, h_scr, *, batch, ch):
    L = RWKV_CHUNK
    N = HEAD_DIM
    W = RWKV_GROUP_LANES
    hpg = W // N
    n_grp = GROUP_WIDTH // W

    @pl.when(pl.program_id(0) == 0)
    def _():
        h_scr[...] = jnp.zeros_like(h_scr)

    units = [(j, d, b, g) for j in range(ch) for d in range(2) for b in range(batch) for g in range(n_grp)]
    per_step = 2 * batch * n_grp
    n_units = len(units)

    def rows_of(j, d):
        return pl.ds((ch - 1 - j if d else j) * L, L)

    def load(ref_f, ref_b, lead):
        return jnp.stack([(ref_b if d else ref_f)[lead + (b, rows_of(j, d), pl.ds(g * W, W))]
                          for (j, d, b, g) in units])

    r = load(r_f, r_b, ())
    v = load(v_f, v_b, ())
    an = load(an_f, an_b, ())
    lw = load(lw_f, lw_b, (0,))
    kd = load(kd_f, kd_b, (0,))
    bdir = load(bd_f, bd_b, (0,))

    shape = (n_units, L, W)
    ui = lax.broadcasted_iota(jnp.int32, shape, 0)
    ti = lax.broadcasted_iota(jnp.int32, shape, 1)
    si = lax.broadcasted_iota(jnp.int32, shape, 2) % N
    sign = 1 - 2 * ((ui // (batch * n_grp)) % 2)
    ahead = (ti - si) * sign
    strict = ahead > 0
    incl = ahead >= 0
    bi = lax.broadcasted_iota(jnp.int32, (W, W), 0) // N
    bj = lax.broadcasted_iota(jnp.int32, (W, W), 1) // N
    diag_blocks = bi == bj

    def bdiag(x):
        return jnp.where(diag_blocks, jnp.concatenate([x] * hpg, axis=1), 0.0).astype(bf16)

    t2 = lax.broadcasted_iota(jnp.int32, (L, L), 0)
    s2 = lax.broadcasted_iota(jnp.int32, (L, L), 1)
    cum_parts = [None] * n_units
    for d in range(2):
        sel = [i for i, u in enumerate(units) if u[1] == d]
        tri = jnp.where((s2 >= t2) if d else (s2 <= t2), 1.0, 0.0).astype(bf16)
        hi, lo = _split_bf16(jnp.concatenate([lw[i] for i in sel], axis=1))
        cum_cat = _dot(tri, hi) + _dot(tri, lo)
        for k, i in enumerate(sel):
            cum_parts[i] = cum_cat[:, k * W:(k + 1) * W]
    cum = jnp.stack(cum_parts)
    p_incl = jnp.exp(cum)
    p_inv = jnp.exp(-cum)
    p_prev = jnp.exp(cum - lw)
    p_last = jnp.exp(jnp.sum(lw, axis=1, keepdims=True))
    a_t = an * p_prev
    b_t = bdir * p_inv
    k_t = kd * p_inv
    r_t = r * p_incl
    b_h = b_t * p_last
    k_h = k_t * p_last

    ar = jnp.concatenate([a_t, r_t], axis=1).astype(bf16)
    prod_b = _bmm_nt(ar, bdiag(b_t))
    prod_k = _bmm_nt(ar, bdiag(k_t))
    a_ab = jnp.where(strict, prod_b[:, :L], 0.0)
    a_rb = jnp.where(incl, prod_b[:, L:], 0.0).astype(bf16)
    a_ak = jnp.where(strict, prod_k[:, :L], 0.0).astype(bf16)
    a_rk = jnp.where(incl, prod_k[:, L:], 0.0).astype(bf16)

    m = 1
    t_inv = None
    while m < L:
        same = (ti // (2 * m)) == (si // (2 * m))
        later_half = ((ti // m) % 2 - (si // m) % 2) * sign == 1
        a_off = jnp.where(same, jnp.where(later_half, a_ab, 0.0), 0.0)
        if m == 1:
            t_inv = jnp.where(ti == si, 1.0, 0.0) + a_off
        else:
            left = _bmm(t_inv.astype(bf16), bdiag(a_off)).astype(bf16)
            t_inv = t_inv + _bmm(left, bdiag(t_inv))
        m *= 2

    v_bd = bdiag(v)
    akv = _bmm(a_ak, v_bd)
    tb = t_inv.astype(bf16)
    w_mat = _bmm(tb, bdiag(a_t))
    z_mat = _bmm(tb, bdiag(akv))
    q_mat = r_t + _bmm(a_rb, bdiag(w_mat))
    y_mat = _bmm(a_rb, bdiag(z_mat)) + _bmm(a_rk, v_bd)

    for j in range(ch):
        sl = slice(j * per_step, (j + 1) * per_step)
        h_t = h_scr[...]
        qw = jnp.concatenate([q_mat[sl], w_mat[sl]], axis=1).astype(bf16)
        ou = _bmm_nt(qw, h_t.astype(bf16))
        out = ou[:, :L] + y_mat[sl]
        u_mat = ou[:, L:] + z_mat[sl]
        keys = jnp.concatenate([b_h[sl], k_h[sl]], axis=1).astype(bf16)
        vals = jnp.concatenate([u_mat, v[sl]], axis=1).astype(bf16)
        upd = _bmm_tn(vals, keys)
        h_scr[...] = jnp.where(diag_blocks, p_last[sl] * h_t + upd, 0.0)
        for k, (_, d, b, g) in enumerate(units[sl]):
            (o_b if d else o_f)[b, rows_of(j, d), pl.ds(g * W, W)] = out[k]


def _rwkv_post_kernel(of_ref, ob_ref, bonus_ref, g_ref, lnw_ref, lnb_ref, bd_ref, out_ref):
    y = of_ref[...] + ob_ref[...]
    bd = bd_ref[...]
    mean = _head_sum(y, bd) * (1.0 / HEAD_DIM)
    yc = y - mean
    var = _head_sum(yc * yc, bd) * (1.0 / HEAD_DIM)
    yn = yc * lax.rsqrt(var + RWKV_GN_EPS) * lnw_ref[...] + lnb_ref[...]
    out_ref[...] = (yn + bonus_ref[...]) * g_ref[...]


def _rwkv_mixer(h_rwkv, mu, w0, w_up, a0, a_up, g_up, k_k, k_a, r_k, ln_w, ln_b, batch, seq):
    t = h_rwkv.shape[0]
    gw = GROUP_WIDTH
    tm = min(256, seq)
    per_batch = seq // tm
    n_halo = t // 8
    wup = jnp.zeros((2 * RWKV_LORA, 2 * gw), f32)
    aup = jnp.zeros((2 * RWKV_LORA, 2 * gw), f32)
    for e in range(2):
        wup = wup.at[e * RWKV_LORA:(e + 1) * RWKV_LORA, e * gw:(e + 1) * gw].set(w_up[e])
        aup = aup.at[e * RWKV_LORA:(e + 1) * RWKV_LORA, e * gw:(e + 1) * gw].set(a_up[e])
    head_id = jnp.arange(gw) // HEAD_DIM
    bd = (head_id[:, None] == head_id[None, :]).astype(bf16)
    vec = lambda a: a.reshape(1, -1).astype(f32)
    const = lambda a: pl.BlockSpec(a.shape, lambda i: (0,) * a.ndim)
    consts = [vec(mu), w0.astype(f32), wup.astype(bf16), a0.astype(f32), aup.astype(bf16),
              g_up.astype(bf16), vec(k_k), vec(k_a), vec(r_k), bd]
    row = pl.BlockSpec((tm, gw), lambda i: (i, 0))
    row2 = pl.BlockSpec((2, tm, gw), lambda i: (0, i, 0))
    one = jax.ShapeDtypeStruct((t, gw), f32)
    two = jax.ShapeDtypeStruct((2, t, gw), f32)
    r, v, an, g, bonus, lw, kd, bdir = pl.pallas_call(
        functools.partial(_rwkv_prep_kernel, per_batch=per_batch), grid=(t // tm,),
        in_specs=[pl.BlockSpec((tm, RWKV_COLS), lambda i: (i, 0)),
                  pl.BlockSpec((8, RWKV_COLS), lambda i: (jnp.maximum(i * (tm // 8) - 1, 0), 0)),
                  pl.BlockSpec((8, RWKV_COLS), lambda i: (jnp.minimum((i + 1) * (tm // 8), n_halo - 1), 0)),
                  ] + [const(a) for a in consts],
        out_specs=[row, row, row, row, row, row2, row2, row2],
        out_shape=[one, one, one, one, one, two, two, two],
        compiler_params=_cparams("parallel"), name="rwkv_prep",
    )(h_rwkv, h_rwkv, h_rwkv, *consts)

    L = RWKV_CHUNK
    ch = RWKV_CHUNKS_PER_STEP
    rows = ch * L
    n_steps = seq // rows
    n_grp = gw // RWKV_GROUP_LANES
    r3, v3, an3 = (a.reshape(batch, seq, gw) for a in (r, v, an))
    lw4, kd4, bd4 = (a.reshape(2, batch, seq, gw) for a in (lw, kd, bdir))
    fwd = pl.BlockSpec((batch, rows, gw), lambda c: (0, c, 0))
    bwd = pl.BlockSpec((batch, rows, gw), lambda c: (0, n_steps - 1 - c, 0))
    fwd_dir = pl.BlockSpec((1, batch, rows, gw), lambda c: (0, 0, c, 0))
    bwd_dir = pl.BlockSpec((1, batch, rows, gw), lambda c: (1, 0, n_steps - 1 - c, 0))
    o3 = jax.ShapeDtypeStruct((batch, seq, gw), f32)
    o_fwd, o_bwd = pl.pallas_call(
        functools.partial(_rwkv_chunk_kernel, batch=batch, ch=ch), grid=(n_steps,),
        in_specs=[fwd, bwd, fwd, bwd, fwd, bwd, fwd_dir, bwd_dir, fwd_dir, bwd_dir, fwd_dir, bwd_dir],
        out_specs=[fwd, bwd], out_shape=[o3, o3],
        scratch_shapes=[pltpu.VMEM((2 * batch * n_grp, RWKV_GROUP_LANES, RWKV_GROUP_LANES), f32)],
        compiler_params=_cparams("arbitrary"), name="rwkv_chunk",
    )(r3, r3, v3, v3, an3, an3, lw4, lw4, kd4, kd4, bd4, bd4)

    tp = min(512, seq)
    rowp = pl.BlockSpec((tp, gw), lambda i: (i, 0))
    cvec = pl.BlockSpec((1, gw), lambda i: (0, 0))
    return pl.pallas_call(
        _rwkv_post_kernel, grid=(t // tp,),
        in_specs=[rowp, rowp, rowp, rowp, cvec, cvec, pl.BlockSpec((gw, gw), lambda i: (0, 0))],
        out_specs=rowp, out_shape=one,
        compiler_params=_cparams("parallel"), name="rwkv_post",
    )(o_fwd.reshape(t, gw), o_bwd.reshape(t, gw), bonus, g, vec(ln_w), vec(ln_b), bd)


def _s5_operators(a_re, a_im, log_dt, b_re, b_im, c_re, c_im):
    L = S5_CHUNK
    lam_r, lam_i = a_re.astype(f32), a_im.astype(f32)
    dt = jnp.exp(log_dt.astype(f32))[..., None]
    mag = jnp.exp(lam_r * dt)
    abar_r, abar_i = mag * jnp.cos(lam_i * dt), mag * jnp.sin(lam_i * dt)
    den = lam_r * lam_r + lam_i * lam_i
    nr, ni = abar_r - 1.0, abar_i
    coef_r = (nr * lam_r + ni * lam_i) / den
    coef_i = (ni * lam_r - nr * lam_i) / den
    br, bi = b_re.astype(f32)[None], b_im.astype(f32)[None]
    bb_r = coef_r[..., None] * br - coef_i[..., None] * bi
    bb_i = coef_r[..., None] * bi + coef_i[..., None] * br
    steps = jnp.arange(L + 1, dtype=f32)[:, None, None, None]
    pmag = jnp.exp(steps * (lam_r * dt)[None])
    pw_r = pmag * jnp.cos(steps * (lam_i * dt)[None])
    pw_i = pmag * jnp.sin(steps * (lam_i * dt)[None])
    cr, ci = c_re.astype(f32)[None, None], c_im.astype(f32)[None, None]
    cp_r = cr * pw_r[:, :, :, None, :] - ci * pw_i[:, :, :, None, :]
    cp_i = cr * pw_i[:, :, :, None, :] + ci * pw_r[:, :, :, None, :]
    kern = (jnp.einsum('kegop,egpi->kegoi', cp_r, bb_r) - jnp.einsum('kegop,egpi->kegoi', cp_i, bb_i))
    idx = jnp.arange(L)
    kf, kb = kern[:L, 0], kern[:L, 1]
    by_lag = jnp.concatenate([kb[1:][::-1], kf[:1] + kb[:1], kf[1:], jnp.zeros_like(kf[:1])], axis=0)
    lag_tab = jnp.transpose(by_lag, (1, 3, 0, 2)).reshape(S5_GROUPS, S5_GROUP, 2 * L * S5_GROUP)
    g_, h_ = S5_GROUPS, S5_GROUP

    def state_in(pr, pi, e):
        re = pr[..., None] * bb_r[e][None] - pi[..., None] * bb_i[e][None]
        im = pr[..., None] * bb_i[e][None] + pi[..., None] * bb_r[e][None]
        return jnp.transpose(re, (1, 0, 3, 2)), jnp.transpose(im, (1, 0, 3, 2))

    f_re, f_im = state_in(pw_r[L - 1 - idx, 0], pw_i[L - 1 - idx, 0], 0)
    b_re_, b_im_ = state_in(pw_r[idx, 1], pw_i[idx, 1], 1)
    b_sum = jnp.concatenate([f_re, f_im, b_re_, b_im_], axis=-1).reshape(g_, L * h_, 4 * S5_STATE)

    def state_out(k_idx, e):
        re = jnp.transpose(cp_r[k_idx, e], (1, 3, 0, 2))
        im = jnp.transpose(cp_i[k_idx, e], (1, 3, 0, 2))
        return re, -im

    fo_r, fo_i = state_out(idx + 1, 0)
    bo_r, bo_i = state_out(L - idx, 1)
    c_out = jnp.concatenate([fo_r, fo_i, bo_r, bo_i], axis=1).reshape(g_, 4 * S5_STATE, L * h_)
    a_pow = jnp.stack([jnp.stack([pw_r[L, e], pw_i[L, e]]) for e in range(2)])
    return lag_tab, b_sum.astype(bf16), c_out.astype(bf16), a_pow


def _s5_state_in_kernel(u_ref, bsum_ref, s_ref):
    s_ref[0] = _dot(u_ref[0], bsum_ref[0])


def _s5_carry_kernel(sf_r, sf_i, sb_r, sb_i, af_r, af_i, ab_r, ab_i, xf_r, xf_i, xb_r, xb_i, *, nch):
    rows, p = af_r.shape
    zero = jnp.zeros((rows, p), f32)

    def body(i, carry):
        fr, fi, br, bi = carry
        j = nch - 1 - i
        xf_r[i] = fr
        xf_i[i] = fi
        xb_r[j] = br
        xb_i[j] = bi
        ar, ai = af_r[...], af_i[...]
        nfr = ar * fr - ai * fi + sf_r[i]
        nfi = ar * fi + ai * fr + sf_i[i]
        ar, ai = ab_r[...], ab_i[...]
        nbr = ar * br - ai * bi + sb_r[j]
        nbi = ar * bi + ai * br + sb_i[j]
        return nfr, nfi, nbr, nbi

    lax.fori_loop(0, nch, body, (zero, zero, zero, zero))


def _s5_out_kernel(u_ref, x_ref, lag_ref, cout_ref, y_ref):
    L, h_ = S5_CHUNK, S5_GROUP
    tab = lag_ref[0]
    width = tab.shape[1]
    blocks = []
    for s in range(L):
        start = (L - 1 - s) * h_
        win = pltpu.roll(tab, width - start, axis=1) if start else tab
        blocks.append(win[:, :L * h_])
    m_op = jnp.concatenate(blocks, axis=0).astype(bf16)
    y_ref[0] = _dot(u_ref[0], m_op) + _dot(x_ref[0], cout_ref[0])


def _s5_post_kernel(y_ref, u_ref, d_ref, w_ref, b_ref, o_ref):
    u = u_ref[...]
    z = y_ref[...] + d_ref[...] * u
    c0 = math.sqrt(2.0 / math.pi)
    z = 0.5 * z * (1.0 + jnp.tanh(c0 * (z + 0.044715 * (z * z * z))))
    gate = _dot(z.astype(bf16), w_ref[...]) + b_ref[...]
    o_ref[...] = z * _sigmoid(gate)


def _s5_mixer(u, a_re, a_im, log_dt, b_re, b_im, c_re, c_im, d_skip, glu_w, glu_b, batch, seq):
    t = u.shape[0]
    L, g_, h_, p = S5_CHUNK, S5_GROUPS, S5_GROUP, S5_STATE
    nch = seq // L
    rows = batch * nch
    lag_tab, b_sum, c_out, a_pow = _s5_operators(a_re, a_im, log_dt, b_re, b_im, c_re, c_im)
    ug = jnp.transpose(u.reshape(rows, L, g_, h_), (2, 0, 1, 3)).reshape(g_, rows, L * h_).astype(bf16)
    grp = lambda r, c: pl.BlockSpec((1, r, c), lambda g: (g, 0, 0))
    s_in = pl.pallas_call(
        _s5_state_in_kernel, grid=(g_,),
        in_specs=[grp(rows, L * h_), grp(L * h_, 4 * p)], out_specs=grp(rows, 4 * p),
        out_shape=jax.ShapeDtypeStruct((g_, rows, 4 * p), f32),
        compiler_params=_cparams("parallel"), name="s5_state_in",
    )(ug, b_sum)
    s_in = jnp.transpose(s_in.reshape(g_, batch, nch, 4, p), (3, 2, 1, 0, 4)).reshape(4, nch, batch * g_, p)
    coef = jnp.broadcast_to(a_pow[:, :, None], (2, 2, batch, g_, p)).reshape(2, 2, batch * g_, p)
    st = jax.ShapeDtypeStruct((nch, batch * g_, p), f32)
    rb = 8
    seq_spec = pl.BlockSpec((nch, rb, p), lambda i: (0, i, 0))
    coef_spec = pl.BlockSpec((rb, p), lambda i: (i, 0))
    xs = pl.pallas_call(
        functools.partial(_s5_carry_kernel, nch=nch), grid=(batch * g_ // rb,),
        in_specs=[seq_spec] * 4 + [coef_spec] * 4, out_specs=[seq_spec] * 4, out_shape=[st] * 4,
        compiler_params=_cparams("parallel"), name="s5_carry",
    )(s_in[0], s_in[1], s_in[2], s_in[3], coef[0, 0], coef[0, 1], coef[1, 0], coef[1, 1])
    x_cat = jnp.stack(xs).reshape(4, nch, batch, g_, p)
    x_cat = jnp.transpose(x_cat, (3, 2, 1, 0, 4)).reshape(g_, rows, 4 * p).astype(bf16)
    y = pl.pallas_call(
        _s5_out_kernel, grid=(g_,),
        in_specs=[grp(rows, L * h_), grp(rows, 4 * p), grp(h_, 2 * L * h_), grp(4 * p, L * h_)],
        out_specs=grp(rows, L * h_),
        out_shape=jax.ShapeDtypeStruct((g_, rows, L * h_), f32),
        compiler_params=_cparams("parallel"), name="s5_out",
    )(ug, x_cat, lag_tab, c_out)
    y = jnp.transpose(y.reshape(g_, rows, L, h_), (1, 2, 0, 3)).reshape(t, g_ * h_)
    tm = min(512, seq)
    w = g_ * h_
    row = pl.BlockSpec((tm, w), lambda i: (i, 0))
    cvec = pl.BlockSpec((1, w), lambda i: (0, 0))
    return pl.pallas_call(
        _s5_post_kernel, grid=(t // tm,),
        in_specs=[row, row, cvec, pl.BlockSpec((w, w), lambda i: (0, 0)), cvec],
        out_specs=row, out_shape=jax.ShapeDtypeStruct((t, w), f32),
        compiler_params=_cparams("parallel"), name="s5_post",
    )(y, u, d_skip.reshape(1, w).astype(f32), glu_w.astype(bf16), glu_b.reshape(1, w).astype(f32))


def _outproj_kernel(o0_ref, o1_ref, o2_ref, o3_ref, w_ref, x_ref, g_ref, n_ref, sc_ref, sh_ref,
                    whi_ref, wlo_ref, out_ref, xn_ref, logit_ref):
    acc = None
    for i, o_ref in enumerate((o0_ref, o1_ref, o2_ref, o3_ref)):
        part = _dot(o_ref[...].astype(bf16), w_ref[i * GROUP_WIDTH:(i + 1) * GROUP_WIDTH, :])
        acc = part if acc is None else acc + part
    x_new = x_ref[...] + g_ref[0] * acc
    out_ref[...] = x_new
    y = _norm_mod(x_new, n_ref, sc_ref, sh_ref)
    xn_ref[...] = y.astype(xn_ref.dtype)
    yhi, ylo = _split_bf16(y)
    whi = whi_ref[...]
    logit_ref[...] = _dot(yhi, whi) + _dot(ylo, whi) + _dot(yhi, wlo_ref[...])


def _outproj_norm(outs, w_out, x, gate, gain2, scale2, shift2, router_w, seq):
    t, d = x.shape
    tm = min(512, seq)
    per_batch = seq // tm
    e = router_w.shape[1]
    whi, wlo = _split_bf16(jnp.zeros((d, LANES), f32).at[:, :e].set(router_w))
    part = pl.BlockSpec((tm, GROUP_WIDTH), lambda i: (i, 0))
    row = pl.BlockSpec((tm, d), lambda i: (i, 0))
    mod = pl.BlockSpec((1, 1, d), lambda i: (i // per_batch, 0, 0))
    const = lambda r, c: pl.BlockSpec((r, c), lambda i: (0, 0))
    x_new, xn, logits = pl.pallas_call(
        _outproj_kernel, grid=(t // tm,),
        in_specs=[part, part, part, part, const(4 * GROUP_WIDTH, d), row, mod, const(1, d), mod, mod,
                  const(d, LANES), const(d, LANES)],
        out_specs=[row, row, pl.BlockSpec((tm, LANES), lambda i: (i, 0))],
        out_shape=[jax.ShapeDtypeStruct((t, d), f32), jax.ShapeDtypeStruct((t, d), bf16),
                   jax.ShapeDtypeStruct((t, LANES), f32)],
        compiler_params=_cparams("parallel"), name="out_proj",
    )(*outs, w_out.astype(bf16), x, gate, gain2.reshape(1, d), scale2, shift2, whi, wlo)
    return x_new, xn, logits[:, :e]


def _moe_ffn_kernel(xs_ref, gate_ref, g2_ref, w1_ref, w3_ref, w2_ref, o_ref, hid_scr, *, n_up, tf):
    step = pl.program_id(2)

    @pl.when(step < n_up)
    def _():
        xs = xs_ref[0, 0]
        a = _dot(xs, w1_ref[0, 0].astype(bf16))
        b = _dot(xs, w3_ref[0, 0].astype(bf16))
        col = pl.multiple_of(step * tf, tf)
        hid_scr[:, pl.ds(col, tf)] = (a * _sigmoid(a) * b).astype(bf16)

    @pl.when(step >= n_up)
    def _():
        o_ref[0, 0] = _dot(hid_scr[...], w2_ref[0, 0].astype(bf16)) * gate_ref[0, 0] * g2_ref[0]


def _moe_ffn(xs, gate, g2, w1, w3, w2, layer):
    batch, n_exp, cap, d = xs.shape
    ff = w1.shape[3]
    tf = 512
    tn = 1024
    n_up, n_down = ff // tf, d // tn
    up = lambda s: jnp.minimum(s, n_up - 1)
    down = lambda s: jnp.maximum(s - n_up, 0)
    return pl.pallas_call(
        functools.partial(_moe_ffn_kernel, n_up=n_up, tf=tf), grid=(n_exp, batch, n_up + n_down),
        in_specs=[pl.BlockSpec((1, 1, cap, d), lambda e, b, s: (b, e, 0, 0)),
                  pl.BlockSpec((1, 1, cap, 1), lambda e, b, s: (b, e, 0, 0)),
                  pl.BlockSpec((1, 1, tn), lambda e, b, s: (b, 0, down(s))),
                  pl.BlockSpec((1, 1, d, tf), lambda e, b, s: (layer, e, 0, up(s))),
                  pl.BlockSpec((1, 1, d, tf), lambda e, b, s: (layer, e, 0, up(s))),
                  pl.BlockSpec((1, 1, ff, tn), lambda e, b, s: (layer, e, 0, down(s)))],
        out_specs=pl.BlockSpec((1, 1, cap, tn), lambda e, b, s: (b, e, 0, down(s))),
        out_shape=jax.ShapeDtypeStruct((batch, n_exp, cap, d), f32),
        scratch_shapes=[pltpu.VMEM((cap, ff), bf16)],
        compiler_params=_cparams("parallel", "parallel", "arbitrary"), name="moe_ffn",
    )(xs, gate, g2, w1, w3, w2)


def _moe_combine_kernel(idx_ref, ys_ref, x_hbm, out_hbm, slab, sem, *, width, group):
    b, h, e = pl.program_id(0), pl.program_id(1), pl.program_id(2)
    cols = pl.ds(pl.multiple_of(h * width, width), width)

    def slab_copy(src, dst):
        return pltpu.make_async_copy(src, dst, sem)

    @pl.when(e == 0)
    def _():
        cp = slab_copy(x_hbm.at[b, :, cols], slab)
        cp.start()
        cp.wait()

    cap = ys_ref.shape[2]

    def body(g, carry):
        r0 = pl.multiple_of(g * group, group)
        tok = [idx_ref[0, 0, r0 + k] for k in range(group)]
        rows = [slab[pl.ds(tok[k], 1), :] + ys_ref[0, 0, pl.ds(r0 + k, 1), :] for k in range(group)]
        for k in range(group):
            slab[pl.ds(tok[k], 1), :] = rows[k]
        return carry

    lax.fori_loop(0, cap // group, body, 0)

    @pl.when(e == pl.num_programs(2) - 1)
    def _():
        cp = slab_copy(slab, out_hbm.at[b, :, cols])
        cp.start()
        cp.wait()


def _moe_combine(x, ys, idx):
    batch, seq, d = x.shape
    n_exp, cap = idx.shape[1], idx.shape[2]
    width = d // 2
    return pl.pallas_call(
        functools.partial(_moe_combine_kernel, width=width, group=8),
        grid=(batch, d // width, n_exp),
        in_specs=[pl.BlockSpec((1, 1, cap), lambda b, h, e: (b * n_exp + e, 0, 0), memory_space=pltpu.SMEM),
                  pl.BlockSpec((1, 1, cap, width), lambda b, h, e: (b, e, 0, h)),
                  pl.BlockSpec(memory_space=pl.ANY)],
        out_specs=pl.BlockSpec(memory_space=pl.ANY),
        out_shape=jax.ShapeDtypeStruct((batch, seq, d), f32),
        scratch_shapes=[pltpu.VMEM((seq, width), f32), pltpu.SemaphoreType.DMA(())],
        compiler_params=_cparams("arbitrary", "arbitrary", "arbitrary"), name="moe_combine",
    )(idx.reshape(batch * n_exp, 1, cap), ys, x)


def _moe_residual(x, xn, logits, g2, w1, w3, w2, layer, batch, seq):
    d = xn.shape[1]
    capacity = EC_CAPACITY_FACTOR * seq // N_EXPERTS
    aff = jax.nn.softmax(logits.reshape(batch, seq, N_EXPERTS), axis=-1)
    gate, idx = lax.top_k(jnp.swapaxes(aff, 1, 2), capacity)
    bidx = jnp.arange(batch)[:, None, None]
    xs = xn.reshape(batch, seq, d)[bidx, idx]
    ys = _moe_ffn(xs, gate[..., None], g2, w1, w3, w2, layer)
    return _moe_combine(x.reshape(batch, seq, d), ys, idx).reshape(batch * seq, d)


def _in_proj_weights(w_in_l):
    o1, o2, o3 = SWA_COLS, SWA_COLS + MLA_COLS, SWA_COLS + MLA_COLS + RWKV_COLS
    w_mla = jnp.zeros((w_in_l.shape[0], MLA_COLS_PAD), f32).at[:, :MLA_COLS].set(w_in_l[:, o1:o2])
    return (w_in_l[:, :o1].astype(bf16), w_mla.astype(bf16),
            w_in_l[:, o2:o3].astype(bf16), w_in_l[:, o3:].astype(bf16))


def kernel(x, c, positions, ada_w, ada_b, norm1_g, norm2_g, w_in, w_out, swa_q_gain, swa_k_gain, swa_sink, mla_q_a_gain, mla_kv_a_gain, mla_w_uq, mla_w_ukv, mla_q_gain, mla_k_gain, rwkv_mu, rwkv_w0, rwkv_w_up, rwkv_a0, rwkv_a_up, rwkv_g_up, rwkv_k_k, rwkv_k_a, rwkv_r_k, rwkv_ln_w, rwkv_ln_b, s5_a_re, s5_a_im, s5_log_dt, s5_b_re, s5_b_im, s5_c_re, s5_c_im, s5_d, s5_glu_w, s5_glu_b, router_w, moe_w1, moe_w3, moe_w2):
    batch, seq, d = x.shape
    depth = ada_w.shape[0]
    t = batch * seq
    cos_h, sin_h, cos_r, sin_r = _rope_tables(positions)
    mod = _ada_mod(c, ada_w, ada_b)
    xf = x.reshape(t, d)
    for l in range(depth):
        sh1, sc1, g1, sh2, sc2, g2 = [m[:, None, :] for m in jnp.split(mod[l], 6, axis=-1)]
        xn = _norm_call(xf, norm1_g[l], sc1, sh1, seq)
        w_swa, w_mla, w_rwkv, w_s5 = _in_proj_weights(w_in[l])
        h_swa = _matmul(xn, w_swa, tm=1024, tn=SWA_COLS, name="in_proj_swa")
        h_mla = _matmul(xn, w_mla, tm=1024, tn=MLA_COLS_PAD, name="in_proj_mla")
        h_rwkv = _matmul(xn, w_rwkv, tm=1024, tn=640, name="in_proj_rwkv")
        h_s5 = _matmul(xn, w_s5, tm=1024, tn=GROUP_WIDTH, name="in_proj_s5")
        o_swa = _swa_mixer(h_swa, cos_h, sin_h, swa_q_gain[l], swa_k_gain[l], swa_sink[l], batch, seq)
        o_mla = _mla_mixer(h_mla, cos_r, sin_r, mla_q_a_gain[l], mla_kv_a_gain[l], mla_w_uq[l],
                           mla_w_ukv[l], mla_q_gain[l], mla_k_gain[l], batch, seq)
        o_rwkv = _rwkv_mixer(h_rwkv, rwkv_mu[l], rwkv_w0[l], rwkv_w_up[l], rwkv_a0[l], rwkv_a_up[l],
                             rwkv_g_up[l], rwkv_k_k[l], rwkv_k_a[l], rwkv_r_k[l],
                             rwkv_ln_w[l], rwkv_ln_b[l], batch, seq)
        o_s5 = _s5_mixer(h_s5, s5_a_re[l], s5_a_im[l], s5_log_dt[l], s5_b_re[l], s5_b_im[l],
                         s5_c_re[l], s5_c_im[l], s5_d[l], s5_glu_w[l], s5_glu_b[l], batch, seq)
        xf, xn2, logits = _outproj_norm((o_swa, o_mla, o_rwkv, o_s5), w_out[l], xf, g1,
                                        norm2_g[l], sc2, sh2, router_w[l], seq)
        xf = _moe_residual(xf, xn2, logits, g2, moe_w1, moe_w3, moe_w2, l, batch, seq)
    return xf.reshape(batch, seq, d)
```

```python
import functools
import math

import jax
import jax.numpy as jnp
from jax import lax
from jax.experimental import pallas as pl
from jax.experimental.pallas import tpu as pltpu

f32 = jnp.float32
bf16 = jnp.bfloat16

D_MODEL = 2048
HEAD_DIM = 64
GROUP_WIDTH = 512
ROPE_THETA = 10000.0
NORM_EPS = 1e-6
NEG_INF = -1e30

SWA_HEADS = 8
SWA_KV_HEADS = 2
SWA_WINDOW = 128
SWA_BLOCK = 128
SWA_Q = 512
SWA_KV = 128
SWA_COLS = 768
SWA_BLOCKS_PER_STEP = 4

MLA_HEADS = 8
MLA_NOPE = 64
MLA_ROPE = 32
MLA_V = 64
MLA_QK = 96
MLA_Q_RANK = 384
MLA_KV_RANK = 128
MLA_COLS = 544
MLA_COLS_PAD = 640
MLA_HEAD_PAD = 128

RWKV_HEADS = 8
RWKV_LORA = 64
RWKV_GATE_LORA = 128
RWKV_GN_EPS = 64e-5
RWKV_COLS = 1920
RWKV_CHUNK = 64
RWKV_CHUNKS_PER_STEP = 2
RWKV_GROUP_LANES = 256

S5_GROUP = 16
S5_GROUPS = 32
S5_STATE = 64
S5_CHUNK = 32

N_EXPERTS = 16
EC_CAPACITY_FACTOR = 2
D_FF_EXPERT = 1024

VMEM_LIMIT_BYTES = 52 * 1024 * 1024
LANES = 128


def _cparams(*sem):
    return pltpu.CompilerParams(dimension_semantics=sem, vmem_limit_bytes=VMEM_LIMIT_BYTES)


def _sigmoid(x):
    return 1.0 / (1.0 + jnp.exp(-x))


def _split_bf16(x):
    hi = x.astype(bf16)
    lo = (x - hi.astype(f32)).astype(bf16)
    return hi, lo


def _dot(a, b):
    return jnp.dot(a, b, preferred_element_type=f32)


def _dot_nt(a, b):
    return lax.dot_general(a, b, (((1,), (1,)), ((), ())), preferred_element_type=f32)


def _dot_tn(a, b):
    return lax.dot_general(a, b, (((0,), (0,)), ((), ())), preferred_element_type=f32)


def _ada_kernel(c_ref, w_ref, b_ref, o_ref):
    c = c_ref[...]
    cond = (c * _sigmoid(c)).astype(bf16)
    o_ref[0] = _dot(cond, w_ref[0].astype(bf16)) + b_ref[0]


def _ada_mod(c, ada_w, ada_b):
    depth, d, n = ada_w.shape
    b = c.shape[0]
    rows = 8
    c_pad = jnp.zeros((rows, d), f32).at[:b].set(c)
    tn = 512
    out = pl.pallas_call(
        _ada_kernel,
        grid=(depth, n // tn),
        in_specs=[
            pl.BlockSpec((rows, d), lambda l, j: (0, 0)),
            pl.BlockSpec((1, d, tn), lambda l, j: (l, 0, j)),
            pl.BlockSpec((1, 1, tn), lambda l, j: (l, 0, j)),
        ],
        out_specs=pl.BlockSpec((1, rows, tn), lambda l, j: (l, 0, j)),
        out_shape=jax.ShapeDtypeStruct((depth, rows, n), f32),
        compiler_params=_cparams("parallel", "parallel"),
        name="ada_mod",
    )(c_pad, ada_w, ada_b.reshape(depth, 1, n))
    return out[:, :b]


def _mm_kernel(x_ref, w_ref, o_ref):
    o_ref[...] = _dot(x_ref[...].astype(bf16), w_ref[...].astype(bf16)).astype(o_ref.dtype)


def _matmul(x, w, *, tm, tn, name):
    m, k = x.shape
    n = w.shape[1]
    tm = min(tm, m)
    return pl.pallas_call(
        _mm_kernel,
        grid=(m // tm, n // tn),
        in_specs=[pl.BlockSpec((tm, k), lambda i, j: (i, 0)),
                  pl.BlockSpec((k, tn), lambda i, j: (0, j))],
        out_specs=pl.BlockSpec((tm, tn), lambda i, j: (i, j)),
        out_shape=jax.ShapeDtypeStruct((m, n), f32),
        compiler_params=_cparams("parallel", "parallel"),
        name=name,
    )(x, w)


def _norm_mod(x, g_ref, sc_ref, sh_ref):
    ms = jnp.mean(x * x, axis=-1, keepdims=True)
    y = x * lax.rsqrt(ms + NORM_EPS) * g_ref[...]
    return y * (1.0 + sc_ref[0]) + sh_ref[0]


def _norm_kernel(x_ref, g_ref, sc_ref, sh_ref, o_ref):
    o_ref[...] = _norm_mod(x_ref[...], g_ref, sc_ref, sh_ref).astype(o_ref.dtype)


def _norm_router_kernel(x_ref, g_ref, sc_ref, sh_ref, whi_ref, wlo_ref, o_ref, logit_ref):
    y = _norm_mod(x_ref[...], g_ref, sc_ref, sh_ref)
    o_ref[...] = y.astype(o_ref.dtype)
    yhi, ylo = _split_bf16(y)
    whi = whi_ref[...]
    logit_ref[...] = _dot(yhi, whi) + _dot(ylo, whi) + _dot(yhi, wlo_ref[...])


def _norm_call(x, gain, scale, shift, seq, router_w=None):
    t, d = x.shape
    tm = min(512, seq)
    per_batch = seq // tm
    row = pl.BlockSpec((tm, d), lambda i: (i, 0))
    mod = pl.BlockSpec((1, 1, d), lambda i: (i // per_batch, 0, 0))
    in_specs = [row, pl.BlockSpec((1, d), lambda i: (0, 0)), mod, mod]
    args = [x, gain.reshape(1, d), scale, shift]
    if router_w is None:
        return pl.pallas_call(
            _norm_kernel, grid=(t // tm,), in_specs=in_specs, out_specs=row,
            out_shape=jax.ShapeDtypeStruct((t, d), bf16),
            compiler_params=_cparams("parallel"), name="norm_mod",
        )(*args)
    e = router_w.shape[1]
    whi, wlo = _split_bf16(jnp.zeros((d, LANES), f32).at[:, :e].set(router_w))
    wspec = pl.BlockSpec((d, LANES), lambda i: (0, 0))
    xn, logits = pl.pallas_call(
        _norm_router_kernel, grid=(t // tm,), in_specs=in_specs + [wspec, wspec],
        out_specs=[row, pl.BlockSpec((tm, LANES), lambda i: (i, 0))],
        out_shape=[jax.ShapeDtypeStruct((t, d), bf16), jax.ShapeDtypeStruct((t, LANES), f32)],
        compiler_params=_cparams("parallel"), name="norm_mod_router",
    )(*args, whi, wlo)
    return xn, logits[:, :e]


def _rope_lanes(x, cos_t, sin_t, half, first_mask):
    n = x.shape[-1]
    fwd = pltpu.roll(x, n - half, axis=1)
    bwd = pltpu.roll(x, half, axis=1)
    return x * cos_t + jnp.where(first_mask, fwd, bwd) * sin_t


def _rope_tables(positions):
    t = positions.size
    pos = positions.reshape(t, 1).astype(f32)

    def tables(dim):
        inv_freq = ROPE_THETA ** (-jnp.arange(0, dim, 2, dtype=f32) / dim)
        ang = pos * inv_freq
        return jnp.cos(ang), jnp.sin(ang)

    c, s = tables(HEAD_DIM)
    cos_h = jnp.concatenate([c, c, c, c], axis=-1)
    sin_h = jnp.concatenate([-s, s, -s, s], axis=-1)
    c, s = tables(MLA_ROPE)
    one = jnp.ones((t, MLA_NOPE), f32)
    zero = jnp.zeros((t, MLA_NOPE), f32)
    pad1 = jnp.ones((t, MLA_HEAD_PAD - MLA_QK), f32)
    pad0 = jnp.zeros((t, MLA_HEAD_PAD - MLA_QK), f32)
    cos_r = jnp.concatenate([one, c, c, pad1], axis=-1)
    sin_r = jnp.concatenate([zero, -s, s, pad0], axis=-1)
    return cos_h, sin_h, cos_r, sin_r


def _pair_rmsnorm(x, gain):
    lane = lax.broadcasted_iota(jnp.int32, x.shape, 1)
    lo = lane < HEAD_DIM
    x2 = x * x
    s_lo = jnp.sum(jnp.where(lo, x2, 0.0), axis=-1, keepdims=True)
    s_hi = jnp.sum(jnp.where(lo, 0.0, x2), axis=-1, keepdims=True)
    ms = jnp.where(lo, s_lo, s_hi) * (1.0 / HEAD_DIM)
    return x * lax.rsqrt(ms + NORM_EPS) * gain


def _swa_prep_kernel(h_ref, cos_ref, sin_ref, qg_ref, kg_ref, q_ref, k_ref, v_ref):
    cos_t = cos_ref[...]
    sin_t = sin_ref[...]
    lane = lax.broadcasted_iota(jnp.int32, cos_t.shape, 1)
    first = (lane % HEAD_DIM) < (HEAD_DIM // 2)
    scale = HEAD_DIM ** -0.5
    for p in range(SWA_Q // LANES):
        x = _pair_rmsnorm(h_ref[:, p * LANES:(p + 1) * LANES], qg_ref[...])
        x = _rope_lanes(x, cos_t, sin_t, HEAD_DIM // 2, first)
        q_ref[:, p * LANES:(p + 1) * LANES] = (x * scale).astype(bf16)
    x = _pair_rmsnorm(h_ref[:, SWA_Q:SWA_Q + SWA_KV], kg_ref[...])
    k_ref[...] = _rope_lanes(x, cos_t, sin_t, HEAD_DIM // 2, first).astype(bf16)
    v_ref[...] = h_ref[:, SWA_Q + SWA_KV:SWA_COLS].astype(bf16)


def _swa_attn_kernel(q_ref, k_ref, v_ref, sink_ref, o_ref, *, seq, nblk):
    step = pl.program_id(1)
    band = 3 * SWA_BLOCK
    grp = SWA_HEADS // SWA_KV_HEADS
    rows = grp * SWA_BLOCK
    row_in_blk = lax.broadcasted_iota(jnp.int32, (rows, band), 0) % SWA_BLOCK
    col = lax.broadcasted_iota(jnp.int32, (rows, band), 1)
    sinks = [jnp.concatenate(
        [jnp.broadcast_to(sink_ref[kh * grp + g:kh * grp + g + 1, 0:1], (SWA_BLOCK, 1)) for g in range(grp)],
        axis=0) for kh in range(SWA_KV_HEADS)]
    qs, ks, vs, valid, sink = [], [], [], [], []
    for i in range(nblk):
        n = step * nblk + i
        start = pl.multiple_of(jnp.clip((n - 1) * SWA_BLOCK, 0, seq - band), SWA_BLOCK)
        kb = k_ref[pl.ds(start, band), :]
        vb = v_ref[pl.ds(start, band), :]
        q = q_ref[i * SWA_BLOCK:(i + 1) * SWA_BLOCK, :]
        ok = jnp.abs(n * SWA_BLOCK + row_in_blk - (start + col)) <= SWA_WINDOW
        for kh in range(SWA_KV_HEADS):
            qs.append(jnp.concatenate(
                [q[:, (kh * grp + g) * HEAD_DIM:(kh * grp + g + 1) * HEAD_DIM] for g in range(grp)], axis=0))
            ks.append(kb[:, kh * HEAD_DIM:(kh + 1) * HEAD_DIM])
            vs.append(vb[:, kh * HEAD_DIM:(kh + 1) * HEAD_DIM])
            valid.append(ok)
            sink.append(sinks[kh])
    sink = jnp.stack(sink)
    s = jnp.where(jnp.stack(valid), _bmm_nt(jnp.stack(qs), jnp.stack(ks)), NEG_INF)
    m = jnp.maximum(jnp.max(s, axis=-1, keepdims=True), sink)
    p = jnp.exp(s - m)
    denom = jnp.sum(p, axis=-1, keepdims=True) + jnp.exp(sink - m)
    o = _bmm(p.astype(bf16), jnp.stack(vs)) / denom
    for i in range(nblk):
        o_ref[i * SWA_BLOCK:(i + 1) * SWA_BLOCK, :] = jnp.concatenate(
            [o[i * SWA_KV_HEADS + kh, g * SWA_BLOCK:(g + 1) * SWA_BLOCK]
             for kh in range(SWA_KV_HEADS) for g in range(grp)], axis=-1)


def _swa_mixer(h_swa, cos_h, sin_h, q_gain, k_gain, sink, batch, seq):
    t = h_swa.shape[0]
    tm = min(512, seq)
    row = lambda w: pl.BlockSpec((tm, w), lambda i: (i, 0))
    const = pl.BlockSpec((1, LANES), lambda i: (0, 0))
    q, k, v = pl.pallas_call(
        _swa_prep_kernel, grid=(t // tm,),
        in_specs=[row(SWA_COLS), row(LANES), row(LANES), const, const],
        out_specs=[row(SWA_Q), row(SWA_KV), row(SWA_KV)],
        out_shape=[jax.ShapeDtypeStruct((t, SWA_Q), bf16),
                   jax.ShapeDtypeStruct((t, SWA_KV), bf16),
                   jax.ShapeDtypeStruct((t, SWA_KV), bf16)],
        compiler_params=_cparams("parallel"), name="swa_prep",
    )(h_swa, cos_h, sin_h, jnp.tile(q_gain, 2).reshape(1, LANES), jnp.tile(k_gain, 2).reshape(1, LANES))
    nblk = SWA_BLOCKS_PER_STEP
    tq = nblk * SWA_BLOCK
    nb = seq // tq
    sink_t = jnp.broadcast_to(sink.astype(f32).reshape(SWA_HEADS, 1), (SWA_HEADS, LANES))
    return pl.pallas_call(
        functools.partial(_swa_attn_kernel, seq=seq, nblk=nblk), grid=(batch, nb),
        in_specs=[pl.BlockSpec((tq, SWA_Q), lambda b, n: (b * nb + n, 0)),
                  pl.BlockSpec((seq, SWA_KV), lambda b, n: (b, 0)),
                  pl.BlockSpec((seq, SWA_KV), lambda b, n: (b, 0)),
                  pl.BlockSpec((SWA_HEADS, LANES), lambda b, n: (0, 0))],
        out_specs=pl.BlockSpec((tq, SWA_Q), lambda b, n: (b * nb + n, 0)),
        out_shape=jax.ShapeDtypeStruct((t, SWA_Q), f32),
        compiler_params=_cparams("parallel", "parallel"), name="swa_attn",
    )(q, k, v, sink_t)


def _mla_prep_kernel(h_ref, cos_ref, sin_ref, qag_ref, kvag_ref, wq_ref, wk_ref, wpe_ref, wv_ref,
                     qg_ref, kg_ref, q_ref, k_ref, v_ref):
    cos_t = cos_ref[...]
    sin_t = sin_ref[...]
    lane = lax.broadcasted_iota(jnp.int32, cos_t.shape, 1)
    first = lane < MLA_NOPE + MLA_ROPE // 2
    cq = h_ref[:, :MLA_Q_RANK]
    cq = cq * lax.rsqrt(jnp.mean(cq * cq, axis=-1, keepdims=True) + NORM_EPS) * qag_ref[...]
    ckv = h_ref[:, MLA_Q_RANK:MLA_Q_RANK + MLA_KV_RANK]
    ckv = (ckv * lax.rsqrt(jnp.mean(ckv * ckv, axis=-1, keepdims=True) + NORM_EPS) * kvag_ref[...]).astype(bf16)
    kpe = h_ref[:, MLA_Q_RANK + MLA_KV_RANK:MLA_COLS_PAD].astype(bf16)
    q_all = _dot(cq.astype(bf16), wq_ref[...])
    k_all = _dot(ckv, wk_ref[...]) + _dot(kpe, wpe_ref[...])
    v_all = _dot(ckv, wv_ref[...])
    one_lane = lane == MLA_V
    scale = MLA_QK ** -0.5 * math.log2(math.e)

    def head_norm(x, gain):
        ms = jnp.sum(x * x, axis=-1, keepdims=True) * (1.0 / MLA_QK)
        x = x * lax.rsqrt(ms + NORM_EPS) * gain
        return _rope_lanes(x, cos_t, sin_t, MLA_ROPE // 2, first)

    for h in range(MLA_HEADS):
        sl = slice(h * MLA_HEAD_PAD, (h + 1) * MLA_HEAD_PAD)
        q_ref[0, h] = (head_norm(q_all[:, sl], qg_ref[...]) * scale).astype(bf16)
        k_ref[0, h] = head_norm(k_all[:, sl], kg_ref[...]).astype(bf16)
        v_ref[0, h] = jnp.where(one_lane, 1.0, v_all[:, sl]).astype(bf16)


def _mla_attn_kernel(q_ref, k_ref, v_ref, o_ref, *, tk, nk):
    n_heads = q_ref.shape[1]
    tq = q_ref.shape[2]
    qs = [q_ref[0, hh] for hh in range(n_heads)]

    def body(j, carry):
        off = pl.multiple_of(j * tk, tk)
        new = []
        for hh in range(n_heads):
            m, acc = carry[hh]
            kj = k_ref[0, hh, pl.ds(off, tk), :]
            vj = v_ref[0, hh, pl.ds(off, tk), :]
            s = _dot_nt(qs[hh], kj)
            m_new = jnp.maximum(m, jnp.max(s, axis=-1, keepdims=True))
            alpha = jnp.exp2(m - m_new)
            p = jnp.exp2(s - m_new)
            new.append((m_new, acc * alpha + _dot(p.astype(bf16), vj)))
        return tuple(new)

    init = tuple((jnp.full((tq, 1), NEG_INF, f32), jnp.zeros((tq, MLA_HEAD_PAD), f32))
                 for _ in range(n_heads))
    final = lax.fori_loop(0, nk, body, init)
    o_ref[0] = jnp.concatenate([acc[:, :MLA_V] / acc[:, MLA_V:MLA_V + 1] for _, acc in final], axis=-1)


def _mla_weights(w_uq, w_ukv, q_gain, k_gain):
    hp = MLA_HEAD_PAD
    wq = jnp.zeros((MLA_Q_RANK, MLA_HEADS, hp), f32).at[:, :, :MLA_QK].set(
        w_uq.reshape(MLA_Q_RANK, MLA_HEADS, MLA_QK))
    kv = w_ukv.reshape(MLA_KV_RANK, MLA_HEADS, MLA_NOPE + MLA_V)
    wk = jnp.zeros((MLA_KV_RANK, MLA_HEADS, hp), f32).at[:, :, :MLA_NOPE].set(kv[:, :, :MLA_NOPE])
    wv = jnp.zeros((MLA_KV_RANK, MLA_HEADS, hp), f32).at[:, :, :MLA_V].set(kv[:, :, MLA_NOPE:])
    eye = jnp.eye(MLA_ROPE, dtype=f32)
    wpe = jnp.zeros((LANES, MLA_HEADS, hp), f32).at[:MLA_ROPE, :, MLA_NOPE:MLA_QK].set(
        jnp.broadcast_to(eye[:, None, :], (MLA_ROPE, MLA_HEADS, MLA_ROPE)))
    flat = lambda w: w.reshape(w.shape[0], MLA_HEADS * hp).astype(bf16)
    pad = lambda g: jnp.zeros((1, hp), f32).at[0, :MLA_QK].set(g)
    return flat(wq), flat(wk), flat(wpe), flat(wv), pad(q_gain), pad(k_gain)


def _mla_mixer(h_mla, cos_r, sin_r, q_a_gain, kv_a_gain, w_uq, w_ukv, q_gain, k_gain, batch, seq):
    t = h_mla.shape[0]
    tm = min(512, seq)
    per_batch = seq // tm
    hp = MLA_HEAD_PAD
    wq, wk, wpe, wv, qg, kg = _mla_weights(w_uq, w_ukv, q_gain, k_gain)
    row = lambda w: pl.BlockSpec((tm, w), lambda b, i: (b * per_batch + i, 0))
    const = lambda a: pl.BlockSpec(a.shape, lambda b, i: (0,) * a.ndim)
    head_out = pl.BlockSpec((1, MLA_HEADS, tm, hp), lambda b, i: (b, 0, i, 0))
    head_shape = jax.ShapeDtypeStruct((batch, MLA_HEADS, seq, hp), bf16)
    qag = q_a_gain.reshape(1, MLA_Q_RANK)
    kvag = kv_a_gain.reshape(1, MLA_KV_RANK)
    q, k, v = pl.pallas_call(
        _mla_prep_kernel, grid=(batch, per_batch),
        in_specs=[row(MLA_COLS_PAD), row(LANES), row(LANES), const(qag), const(kvag),
                  const(wq), const(wk), const(wpe), const(wv), const(qg), const(kg)],
        out_specs=[head_out, head_out, head_out],
        out_shape=[head_shape, head_shape, head_shape],
        compiler_params=_cparams("parallel", "parallel"), name="mla_prep",
    )(h_mla, cos_r, sin_r, qag, kvag, wq, wk, wpe, wv, qg, kg)
    tq = min(1024, seq)
    tk = min(2048, seq)
    kv_spec = pl.BlockSpec((1, 2, seq, hp), lambda b, h, i: (b, h, 0, 0))
    o = pl.pallas_call(
        functools.partial(_mla_attn_kernel, tk=tk, nk=seq // tk),
        grid=(batch, MLA_HEADS // 2, seq // tq),
        in_specs=[pl.BlockSpec((1, 2, tq, hp), lambda b, h, i: (b, h, i, 0)), kv_spec, kv_spec],
        out_specs=pl.BlockSpec((1, tq, 2 * MLA_V), lambda b, h, i: (b, i, h)),
        out_shape=jax.ShapeDtypeStruct((batch, seq, MLA_HEADS * MLA_V), f32),
        compiler_params=_cparams("parallel", "parallel", "parallel"), name="mla_attn",
    )(q, k, v)
    return o.reshape(t, MLA_HEADS * MLA_V)


def _head_sum(x, bd):
    hi, lo = _split_bf16(x)
    return _dot(hi, bd) + _dot(lo, bd)


def _rwkv_prep_kernel(h_ref, prev_ref, next_ref, mu_ref, w0_ref, wup_ref, a0_ref, aup_ref, gup_ref,
                      kk_ref, ka_ref, rk_ref, bd_ref,
                      r_ref, v_ref, an_ref, g_ref, bonus_ref, lw_ref, kd_ref, bdir_ref, *, per_batch):
    i = pl.program_id(0)
    x = h_ref[...]
    tm = x.shape[0]
    row = lax.broadcasted_iota(jnp.int32, x.shape, 0)
    prev_row = jnp.where(i % per_batch == 0, 0.0, prev_ref[7:8, :])
    next_row = jnp.where(i % per_batch == per_batch - 1, 0.0, next_ref[0:1, :])
    x_prev = jnp.where(row == 0, prev_row, pltpu.roll(x, 1, axis=0))
    x_next = jnp.where(row == tm - 1, next_row, pltpu.roll(x, tm - 1, axis=0))
    hs = x + mu_ref[...] * (0.5 * (x_prev + x_next) - x)
    gw = GROUP_WIDTH
    r = hs[:, 0:gw]
    k = hs[:, gw:2 * gw]
    v = hs[:, 2 * gw:3 * gw]
    wd = hs[:, 3 * gw:3 * gw + LANES]
    ad = hs[:, 3 * gw + LANES:3 * gw + 2 * LANES]
    gd = hs[:, 3 * gw + 2 * LANES:RWKV_COLS]
    bd = bd_ref[...]
    lora_w = _dot(jnp.tanh(wd).astype(bf16), wup_ref[...])
    lora_a = _dot(ad.astype(bf16), aup_ref[...])
    kk = k * kk_ref[...]
    kk = kk * lax.rsqrt(_head_sum(kk * kk, bd) + 1e-12)
    r_ref[...] = r
    v_ref[...] = v
    an_ref[...] = -kk
    g_ref[...] = _dot(_sigmoid(gd).astype(bf16), gup_ref[...])
    bonus_ref[...] = _head_sum(r * k * rk_ref[...], bd) * v
    for e in range(2):
        z = -(w0_ref[e:e + 1, :] + lora_w[:, e * gw:(e + 1) * gw])
        softplus = jnp.maximum(z, 0.0) + jnp.log(1.0 + jnp.exp(-jnp.abs(z)))
        lw_ref[e] = -jnp.exp(-softplus - 0.5)
        a = _sigmoid(a0_ref[e:e + 1, :] + lora_a[:, e * gw:(e + 1) * gw])
        kd_ref[e] = k * (1.0 + (a - 1.0) * ka_ref[...])
        bdir_ref[e] = kk * a


def _bmm(a, b):
    return lax.dot_general(a, b, (((2,), (1,)), ((0,), (0,))), preferred_element_type=f32)


def _bmm_nt(a, b):
    return lax.dot_general(a, b, (((2,), (2,)), ((0,), (0,))), preferred_element_type=f32)


def _bmm_tn(a, b):
    return lax.dot_general(a, b, (((1,), (1,)), ((0,), (0,))), preferred_element_type=f32)


def _rwkv_chunk_kernel(r_f, r_b, v_f, v_b, an_f, an_b, lw_f, lw_b, kd_f, kd_b, bd_f, bd_b,
                       o_f, o_b, h_scr, *, batch, ch):
    L = RWKV_CHUNK
    N = HEAD_DIM
    W = RWKV_GROUP_LANES
    hpg = W // N
    n_grp = GROUP_WIDTH // W

    @pl.when(pl.program_id(0) == 0)
    def _():
        h_scr[...] = jnp.zeros_like(h_scr)

    units = [(j, d, b, g) for j in range(ch) for d in range(2) for b in range(batch) for g in range(n_grp)]
    per_step = 2 * batch * n_grp
    n_units = len(units)

    def rows_of(j, d):
        return pl.ds((ch - 1 - j if d else j) * L, L)

    def load(ref_f, ref_b, lead):
        return jnp.stack([(ref_b if d else ref_f)[lead + (b, rows_of(j, d), pl.ds(g * W, W))]
                          for (j, d, b, g) in units])

    r = load(r_f, r_b, ())
    v = load(v_f, v_b, ())
    an = load(an_f, an_b, ())
    lw = load(lw_f, lw_b, (0,))
    kd = load(kd_f, kd_b, (0,))
    bdir = load(bd_f, bd_b, (0,))

    shape = (n_units, L, W)
    ui = lax.broadcasted_iota(jnp.int32, shape, 0)
    ti = lax.broadcasted_iota(jnp.int32, shape, 1)
    si = lax.broadcasted_iota(jnp.int32, shape, 2) % N
    sign = 1 - 2 * ((ui // (batch * n_grp)) % 2)
    ahead = (ti - si) * sign
    strict = ahead > 0
    incl = ahead >= 0
    bi = lax.broadcasted_iota(jnp.int32, (W, W), 0) // N
    bj = lax.broadcasted_iota(jnp.int32, (W, W), 1) // N
    diag_blocks = bi == bj

    def bdiag(x):
        return jnp.where(diag_blocks, jnp.concatenate([x] * hpg, axis=1), 0.0).astype(bf16)

    t2 = lax.broadcasted_iota(jnp.int32, (L, L), 0)
    s2 = lax.broadcasted_iota(jnp.int32, (L, L), 1)
    cum_parts = [None] * n_units
    for d in range(2):
        sel = [i for i, u in enumerate(units) if u[1] == d]
        tri = jnp.where((s2 >= t2) if d else (s2 <= t2), 1.0, 0.0).astype(bf16)
        hi, lo = _split_bf16(jnp.concatenate([lw[i] for i in sel], axis=1))
        cum_cat = _dot(tri, hi) + _dot(tri, lo)
        for k, i in enumerate(sel):
            cum_parts[i] = cum_cat[:, k * W:(k + 1) * W]
    cum = jnp.stack(cum_parts)
    p_incl = jnp.exp(cum)
    p_inv = jnp.exp(-cum)
    p_prev = jnp.exp(cum - lw)
    p_last = jnp.exp(jnp.sum(lw, axis=1, keepdims=True))
    a_t = an * p_prev
    b_t = bdir * p_inv
    k_t = kd * p_inv
    r_t = r * p_incl
    b_h = b_t * p_last
    k_h = k_t * p_last

    ar = jnp.concatenate([a_t, r_t], axis=1).astype(bf16)
    prod_b = _bmm_nt(ar, bdiag(b_t))
    prod_k = _bmm_nt(ar, bdiag(k_t))
    a_ab = jnp.where(strict, prod_b[:, :L], 0.0)
    a_rb = jnp.where(incl, prod_b[:, L:], 0.0).astype(bf16)
    a_ak = jnp.where(strict, prod_k[:, :L], 0.0).astype(bf16)
    a_rk = jnp.where(incl, prod_k[:, L:], 0.0).astype(bf16)

    m = 1
    t_inv = None
    while m < L:
        same = (ti // (2 * m)) == (si // (2 * m))
        later_half = ((ti // m) % 2 - (si // m) % 2) * sign == 1
        a_off = jnp.where(same, jnp.where(later_half, a_ab, 0.0), 0.0)
        if m == 1:
            t_inv = jnp.where(ti == si, 1.0, 0.0) + a_off
        else:
            left = _bmm(t_inv.astype(bf16), bdiag(a_off)).astype(bf16)
            t_inv = t_inv + _bmm(left, bdiag(t_inv))
        m *= 2

    v_bd = bdiag(v)
    akv = _bmm(a_ak, v_bd)
    tb = t_inv.astype(bf16)
    w_mat = _bmm(tb, bdiag(a_t))
    z_mat = _bmm(tb, bdiag(akv))
    q_mat = r_t + _bmm(a_rb, bdiag(w_mat))
    y_mat = _bmm(a_rb, bdiag(z_mat)) + _bmm(a_rk, v_bd)

    for j in range(ch):
        sl = slice(j * per_step, (j + 1) * per_step)
        h_t = h_scr[...]
        qw = jnp.concatenate([q_mat[sl], w_mat[sl]], axis=1).astype(bf16)
        ou = _bmm_nt(qw, h_t.astype(bf16))
        out = ou[:, :L] + y_mat[sl]
        u_mat = ou[:, L:] + z_mat[sl]
        keys = jnp.concatenate([b_h[sl], k_h[sl]], axis=1).astype(bf16)
        vals = jnp.concatenate([u_mat, v[sl]], axis=1).astype(bf16)
        upd = _bmm_tn(vals, keys)
        h_scr[...] = jnp.where(diag_blocks, p_last[sl] * h_t + upd, 0.0)
        for k, (_, d, b, g) in enumerate(units[sl]):
            (o_b if d else o_f)[b, rows_of(j, d), pl.ds(g * W, W)] = out[k]


def _rwkv_post_kernel(of_ref, ob_ref, bonus_ref, g_ref, lnw_ref, lnb_ref, bd_ref, out_ref):
    y = of_ref[...] + ob_ref[...]
    bd = bd_ref[...]
    mean = _head_sum(y, bd) * (1.0 / HEAD_DIM)
    yc = y - mean
    var = _head_sum(yc * yc, bd) * (1.0 / HEAD_DIM)
    yn = yc * lax.rsqrt(var + RWKV_GN_EPS) * lnw_ref[...] + lnb_ref[...]
    out_ref[...] = (yn + bonus_ref[...]) * g_ref[...]


def _rwkv_mixer(h_rwkv, mu, w0, w_up, a0, a_up, g_up, k_k, k_a, r_k, ln_w, ln_b, batch, seq):
    t = h_rwkv.shape[0]
    gw = GROUP_WIDTH
    tm = min(256, seq)
    per_batch = seq // tm
    n_halo = t // 8
    wup = jnp.zeros((2 * RWKV_LORA, 2 * gw), f32)
    aup = jnp.zeros((2 * RWKV_LORA, 2 * gw), f32)
    for e in range(2):
        wup = wup.at[e * RWKV_LORA:(e + 1) * RWKV_LORA, e * gw:(e + 1) * gw].set(w_up[e])
        aup = aup.at[e * RWKV_LORA:(e + 1) * RWKV_LORA, e * gw:(e + 1) * gw].set(a_up[e])
    head_id = jnp.arange(gw) // HEAD_DIM
    bd = (head_id[:, None] == head_id[None, :]).astype(bf16)
    vec = lambda a: a.reshape(1, -1).astype(f32)
    const = lambda a: pl.BlockSpec(a.shape, lambda i: (0,) * a.ndim)
    consts = [vec(mu), w0.astype(f32), wup.astype(bf16), a0.astype(f32), aup.astype(bf16),
              g_up.astype(bf16), vec(k_k), vec(k_a), vec(r_k), bd]
    row = pl.BlockSpec((tm, gw), lambda i: (i, 0))
    row2 = pl.BlockSpec((2, tm, gw), lambda i: (0, i, 0))
    one = jax.ShapeDtypeStruct((t, gw), f32)
    two = jax.ShapeDtypeStruct((2, t, gw), f32)
    r, v, an, g, bonus, lw, kd, bdir = pl.pallas_call(
        functools.partial(_rwkv_prep_kernel, per_batch=per_batch), grid=(t // tm,),
        in_specs=[pl.BlockSpec((tm, RWKV_COLS), lambda i: (i, 0)),
                  pl.BlockSpec((8, RWKV_COLS), lambda i: (jnp.maximum(i * (tm // 8) - 1, 0), 0)),
                  pl.BlockSpec((8, RWKV_COLS), lambda i: (jnp.minimum((i + 1) * (tm // 8), n_halo - 1), 0)),
                  ] + [const(a) for a in consts],
        out_specs=[row, row, row, row, row, row2, row2, row2],
        out_shape=[one, one, one, one, one, two, two, two],
        compiler_params=_cparams("parallel"), name="rwkv_prep",
    )(h_rwkv, h_rwkv, h_rwkv, *consts)

    L = RWKV_CHUNK
    ch = RWKV_CHUNKS_PER_STEP
    rows = ch * L
    n_steps = seq // rows
    n_grp = gw // RWKV_GROUP_LANES
    r3, v3, an3 = (a.reshape(batch, seq, gw) for a in (r, v, an))
    lw4, kd4, bd4 = (a.reshape(2, batch, seq, gw) for a in (lw, kd, bdir))
    fwd = pl.BlockSpec((batch, rows, gw), lambda c: (0, c, 0))
    bwd = pl.BlockSpec((batch, rows, gw), lambda c: (0, n_steps - 1 - c, 0))
    fwd_dir = pl.BlockSpec((1, batch, rows, gw), lambda c: (0, 0, c, 0))
    bwd_dir = pl.BlockSpec((1, batch, rows, gw), lambda c: (1, 0, n_steps - 1 - c, 0))
    o3 = jax.ShapeDtypeStruct((batch, seq, gw), f32)
    o_fwd, o_bwd = pl.pallas_call(
        functools.partial(_rwkv_chunk_kernel, batch=batch, ch=ch), grid=(n_steps,),
        in_specs=[fwd, bwd, fwd, bwd, fwd, bwd, fwd_dir, bwd_dir, fwd_dir, bwd_dir, fwd_dir, bwd_dir],
        out_specs=[fwd, bwd], out_shape=[o3, o3],
        scratch_shapes=[pltpu.VMEM((2 * batch * n_grp, RWKV_GROUP_LANES, RWKV_GROUP_LANES), f32)],
        compiler_params=_cparams("arbitrary"), name="rwkv_chunk",
    )(r3, r3, v3, v3, an3, an3, lw4, lw4, kd4, kd4, bd4, bd4)

    tp = min(512, seq)
    rowp = pl.BlockSpec((tp, gw), lambda i: (i, 0))
    cvec = pl.BlockSpec((1, gw), lambda i: (0, 0))
    return pl.pallas_call(
        _rwkv_post_kernel, grid=(t // tp,),
        in_specs=[rowp, rowp, rowp, rowp, cvec, cvec, pl.BlockSpec((gw, gw), lambda i: (0, 0))],
        out_specs=rowp, out_shape=one,
        compiler_params=_cparams("parallel"), name="rwkv_post",
    )(o_fwd.reshape(t, gw), o_bwd.reshape(t, gw), bonus, g, vec(ln_w), vec(ln_b), bd)


def _s5_operators(a_re, a_im, log_dt, b_re, b_im, c_re, c_im):
    L = S5_CHUNK
    lam_r, lam_i = a_re.astype(f32), a_im.astype(f32)
    dt = jnp.exp(log_dt.astype(f32))[..., None]
    mag = jnp.exp(lam_r * dt)
    abar_r, abar_i = mag * jnp.cos(lam_i * dt), mag * jnp.sin(lam_i * dt)
    den = lam_r * lam_r + lam_i * lam_i
    nr, ni = abar_r - 1.0, abar_i
    coef_r = (nr * lam_r + ni * lam_i) / den
    coef_i = (ni * lam_r - nr * lam_i) / den
    br, bi = b_re.astype(f32)[None], b_im.astype(f32)[None]
    bb_r = coef_r[..., None] * br - coef_i[..., None] * bi
    bb_i = coef_r[..., None] * bi + coef_i[..., None] * br
    steps = jnp.arange(L + 1, dtype=f32)[:, None, None, None]
    pmag = jnp.exp(steps * (lam_r * dt)[None])
    pw_r = pmag * jnp.cos(steps * (lam_i * dt)[None])
    pw_i = pmag * jnp.sin(steps * (lam_i * dt)[None])
    cr, ci = c_re.astype(f32)[None, None], c_im.astype(f32)[None, None]
    cp_r = cr * pw_r[:, :, :, None, :] - ci * pw_i[:, :, :, None, :]
    cp_i = cr * pw_i[:, :, :, None, :] + ci * pw_r[:, :, :, None, :]
    kern = (jnp.einsum('kegop,egpi->kegoi', cp_r, bb_r) - jnp.einsum('kegop,egpi->kegoi', cp_i, bb_i))
    idx = jnp.arange(L)
    kf, kb = kern[:L, 0], kern[:L, 1]
    by_lag = jnp.concatenate([kb[1:][::-1], kf[:1] + kb[:1], kf[1:], jnp.zeros_like(kf[:1])], axis=0)
    lag_tab = jnp.transpose(by_lag, (1, 3, 0, 2)).reshape(S5_GROUPS, S5_GROUP, 2 * L * S5_GROUP)
    g_, h_ = S5_GROUPS, S5_GROUP

    def state_in(pr, pi, e):
        re = pr[..., None] * bb_r[e][None] - pi[..., None] * bb_i[e][None]
        im = pr[..., None] * bb_i[e][None] + pi[..., None] * bb_r[e][None]
        return jnp.transpose(re, (1, 0, 3, 2)), jnp.transpose(im, (1, 0, 3, 2))

    f_re, f_im = state_in(pw_r[L - 1 - idx, 0], pw_i[L - 1 - idx, 0], 0)
    b_re_, b_im_ = state_in(pw_r[idx, 1], pw_i[idx, 1], 1)
    b_sum = jnp.concatenate([f_re, f_im, b_re_, b_im_], axis=-1).reshape(g_, L * h_, 4 * S5_STATE)

    def state_out(k_idx, e):
        re = jnp.transpose(cp_r[k_idx, e], (1, 3, 0, 2))
        im = jnp.transpose(cp_i[k_idx, e], (1, 3, 0, 2))
        return re, -im

    fo_r, fo_i = state_out(idx + 1, 0)
    bo_r, bo_i = state_out(L - idx, 1)
    c_out = jnp.concatenate([fo_r, fo_i, bo_r, bo_i], axis=1).reshape(g_, 4 * S5_STATE, L * h_)
    a_pow = jnp.stack([jnp.stack([pw_r[L, e], pw_i[L, e]]) for e in range(2)])
    return lag_tab, b_sum.astype(bf16), c_out.astype(bf16), a_pow


def _s5_state_in_kernel(u_ref, bsum_ref, s_ref):
    s_ref[0] = _dot(u_ref[0], bsum_ref[0])


def _s5_carry_kernel(sf_r, sf_i, sb_r, sb_i, af_r, af_i, ab_r, ab_i, xf_r, xf_i, xb_r, xb_i, *, nch):
    rows, p = af_r.shape
    zero = jnp.zeros((rows, p), f32)

    def body(i, carry):
        fr, fi, br, bi = carry
        j = nch - 1 - i
        xf_r[i] = fr
        xf_i[i] = fi
        xb_r[j] = br
        xb_i[j] = bi
        ar, ai = af_r[...], af_i[...]
        nfr = ar * fr - ai * fi + sf_r[i]
        nfi = ar * fi + ai * fr + sf_i[i]
        ar, ai = ab_r[...], ab_i[...]
        nbr = ar * br - ai * bi + sb_r[j]
        nbi = ar * bi + ai * br + sb_i[j]
        return nfr, nfi, nbr, nbi

    lax.fori_loop(0, nch, body, (zero, zero, zero, zero))


def _s5_out_kernel(u_ref, x_ref, lag_ref, cout_ref, y_ref):
    L, h_ = S5_CHUNK, S5_GROUP
    tab = lag_ref[0]
    width = tab.shape[1]
    blocks = []
    for s in range(L):
        start = (L - 1 - s) * h_
        win = pltpu.roll(tab, width - start, axis=1) if start else tab
        blocks.append(win[:, :L * h_])
    m_op = jnp.concatenate(blocks, axis=0).astype(bf16)
    y_ref[0] = _dot(u_ref[0], m_op) + _dot(x_ref[0], cout_ref[0])


def _s5_post_kernel(y_ref, u_ref, d_ref, w_ref, b_ref, o_ref):
    u = u_ref[...]
    z = y_ref[...] + d_ref[...] * u
    c0 = math.sqrt(2.0 / math.pi)
    z = 0.5 * z * (1.0 + jnp.tanh(c0 * (z + 0.044715 * (z * z * z))))
    gate = _dot(z.astype(bf16), w_ref[...]) + b_ref[...]
    o_ref[...] = z * _sigmoid(gate)


def _s5_mixer(u, a_re, a_im, log_dt, b_re, b_im, c_re, c_im, d_skip, glu_w, glu_b, batch, seq):
    t = u.shape[0]
    L, g_, h_, p = S5_CHUNK, S5_GROUPS, S5_GROUP, S5_STATE
    nch = seq // L
    rows = batch * nch
    lag_tab, b_sum, c_out, a_pow = _s5_operators(a_re, a_im, log_dt, b_re, b_im, c_re, c_im)
    ug = jnp.transpose(u.reshape(rows, L, g_, h_), (2, 0, 1, 3)).reshape(g_, rows, L * h_).astype(bf16)
    grp = lambda r, c: pl.BlockSpec((1, r, c), lambda g: (g, 0, 0))
    s_in = pl.pallas_call(
        _s5_state_in_kernel, grid=(g_,),
        in_specs=[grp(rows, L * h_), grp(L * h_, 4 * p)], out_specs=grp(rows, 4 * p),
        out_shape=jax.ShapeDtypeStruct((g_, rows, 4 * p), f32),
        compiler_params=_cparams("parallel"), name="s5_state_in",
    )(ug, b_sum)
    s_in = jnp.transpose(s_in.reshape(g_, batch, nch, 4, p), (3, 2, 1, 0, 4)).reshape(4, nch, batch * g_, p)
    coef = jnp.broadcast_to(a_pow[:, :, None], (2, 2, batch, g_, p)).reshape(2, 2, batch * g_, p)
    st = jax.ShapeDtypeStruct((nch, batch * g_, p), f32)
    rb = 8
    seq_spec = pl.BlockSpec((nch, rb, p), lambda i: (0, i, 0))
    coef_spec = pl.BlockSpec((rb, p), lambda i: (i, 0))
    xs = pl.pallas_call(
        functools.partial(_s5_carry_kernel, nch=nch), grid=(batch * g_ // rb,),
        in_specs=[seq_spec] * 4 + [coef_spec] * 4, out_specs=[seq_spec] * 4, out_shape=[st] * 4,
        compiler_params=_cparams("parallel"), name="s5_carry",
    )(s_in[0], s_in[1], s_in[2], s_in[3], coef[0, 0], coef[0, 1], coef[1, 0], coef[1, 1])
    x_cat = jnp.stack(xs).reshape(4, nch, batch, g_, p)
    x_cat = jnp.transpose(x_cat, (3, 2, 1, 0, 4)).reshape(g_, rows, 4 * p).astype(bf16)
    y = pl.pallas_call(
        _s5_out_kernel, grid=(g_,),
        in_specs=[grp(rows, L * h_), grp(rows, 4 * p), grp(h_, 2 * L * h_), grp(4 * p, L * h_)],
        out_specs=grp(rows, L * h_),
        out_shape=jax.ShapeDtypeStruct((g_, rows, L * h_), f32),
        compiler_params=_cparams("parallel"), name="s5_out",
    )(ug, x_cat, lag_tab, c_out)
    y = jnp.transpose(y.reshape(g_, rows, L, h_), (1, 2, 0, 3)).reshape(t, g_ * h_)
    tm = min(512, seq)
    w = g_ * h_
    row = pl.BlockSpec((tm, w), lambda i: (i, 0))
    cvec = pl.BlockSpec((1, w), lambda i: (0, 0))
    return pl.pallas_call(
        _s5_post_kernel, grid=(t // tm,),
        in_specs=[row, row, cvec, pl.BlockSpec((w, w), lambda i: (0, 0)), cvec],
        out_specs=row, out_shape=jax.ShapeDtypeStruct((t, w), f32),
        compiler_params=_cparams("parallel"), name="s5_post",
    )(y, u, d_skip.reshape(1, w).astype(f32), glu_w.astype(bf16), glu_b.reshape(1, w).astype(f32))


def _outproj_kernel(o0_ref, o1_ref, o2_ref, o3_ref, w_ref, x_ref, g_ref, out_ref):
    acc = None
    for i, o_ref in enumerate((o0_ref, o1_ref, o2_ref, o3_ref)):
        part = _dot(o_ref[...].astype(bf16), w_ref[i * GROUP_WIDTH:(i + 1) * GROUP_WIDTH, :])
        acc = part if acc is None else acc + part
    out_ref[...] = x_ref[...] + g_ref[0] * acc


def _outproj(outs, w_out, x, gate, seq):
    t, d = x.shape
    tm = min(512, seq)
    per_batch = seq // tm
    part = pl.BlockSpec((tm, GROUP_WIDTH), lambda i: (i, 0))
    row = pl.BlockSpec((tm, d), lambda i: (i, 0))
    return pl.pallas_call(
        _outproj_kernel, grid=(t // tm,),
        in_specs=[part, part, part, part, pl.BlockSpec((4 * GROUP_WIDTH, d), lambda i: (0, 0)), row,
                  pl.BlockSpec((1, 1, d), lambda i: (i // per_batch, 0, 0))],
        out_specs=row, out_shape=jax.ShapeDtypeStruct((t, d), f32),
        compiler_params=_cparams("parallel"), name="out_proj",
    )(*outs, w_out.astype(bf16), x, gate)


def _moe_ffn_kernel(xs_ref, gate_ref, g2_ref, w1_ref, w3_ref, w2_ref, o_ref, hid_scr, *, n_up, tf):
    step = pl.program_id(1)
    b = pl.program_id(2)

    @pl.when(step < n_up)
    def _():
        xs = xs_ref[b, 0]
        up = _dot(xs, w1_ref[0, 0].astype(bf16))
        lin = _dot(xs, w3_ref[0, 0].astype(bf16))
        col = pl.multiple_of(step * tf, tf)
        hid_scr[b, :, pl.ds(col, tf)] = (up * _sigmoid(up) * lin).astype(bf16)

    @pl.when(step >= n_up)
    def _():
        o_ref[0, 0] = _dot(hid_scr[b], w2_ref[0, 0].astype(bf16)) * gate_ref[0, 0] * g2_ref[0]


def _moe_ffn(xs, gate, g2, w1, w3, w2, layer):
    batch, n_exp, cap, d = xs.shape
    ff = w1.shape[3]
    tf = 256
    tn = 1024
    n_up, n_down = ff // tf, d // tn
    up = lambda s: jnp.minimum(s, n_up - 1)
    down = lambda s: jnp.maximum(s - n_up, 0)
    out_batch = lambda s, b: jnp.where(s < n_up, 0, b)
    return pl.pallas_call(
        functools.partial(_moe_ffn_kernel, n_up=n_up, tf=tf), grid=(n_exp, n_up + n_down, batch),
        in_specs=[pl.BlockSpec((batch, 1, cap, d), lambda e, s, b: (0, e, 0, 0)),
                  pl.BlockSpec((1, 1, cap, 1), lambda e, s, b: (b, e, 0, 0)),
                  pl.BlockSpec((1, 1, tn), lambda e, s, b: (b, 0, down(s))),
                  pl.BlockSpec((1, 1, d, tf), lambda e, s, b: (layer, e, 0, up(s))),
                  pl.BlockSpec((1, 1, d, tf), lambda e, s, b: (layer, e, 0, up(s))),
                  pl.BlockSpec((1, 1, ff, tn), lambda e, s, b: (layer, e, 0, down(s)))],
        out_specs=pl.BlockSpec((1, 1, cap, tn), lambda e, s, b: (out_batch(s, b), e, 0, down(s))),
        out_shape=jax.ShapeDtypeStruct((batch, n_exp, cap, d), f32),
        scratch_shapes=[pltpu.VMEM((batch, cap, ff), bf16)],
        compiler_params=_cparams("parallel", "arbitrary", "arbitrary"), name="moe_ffn",
    )(xs, gate, g2, w1, w3, w2)


def _moe_combine_kernel(idx_ref, ys_ref, x_hbm, out_hbm, slab, sem, *, width, group):
    b, h, e = pl.program_id(0), pl.program_id(1), pl.program_id(2)
    cols = pl.ds(pl.multiple_of(h * width, width), width)

    def slab_copy(src, dst):
        return pltpu.make_async_copy(src, dst, sem)

    @pl.when(e == 0)
    def _():
        cp = slab_copy(x_hbm.at[b, :, cols], slab)
        cp.start()
        cp.wait()

    cap = ys_ref.shape[2]

    def body(g, carry):
        r0 = pl.multiple_of(g * group, group)
        tok = [idx_ref[0, 0, r0 + k] for k in range(group)]
        rows = [slab[pl.ds(tok[k], 1), :] + ys_ref[0, 0, pl.ds(r0 + k, 1), :] for k in range(group)]
        for k in range(group):
            slab[pl.ds(tok[k], 1), :] = rows[k]
        return carry

    lax.fori_loop(0, cap // group, body, 0)

    @pl.when(e == pl.num_programs(2) - 1)
    def _():
        cp = slab_copy(slab, out_hbm.at[b, :, cols])
        cp.start()
        cp.wait()


def _moe_combine(x, ys, idx):
    batch, seq, d = x.shape
    n_exp, cap = idx.shape[1], idx.shape[2]
    width = d // 2
    return pl.pallas_call(
        functools.partial(_moe_combine_kernel, width=width, group=8),
        grid=(batch, d // width, n_exp),
        in_specs=[pl.BlockSpec((1, 1, cap), lambda b, h, e: (b * n_exp + e, 0, 0), memory_space=pltpu.SMEM),
                  pl.BlockSpec((1, 1, cap, width), lambda b, h, e: (b, e, 0, h)),
                  pl.BlockSpec(memory_space=pl.ANY)],
        out_specs=pl.BlockSpec(memory_space=pl.ANY),
        out_shape=jax.ShapeDtypeStruct((batch, seq, d), f32),
        scratch_shapes=[pltpu.VMEM((seq, width), f32), pltpu.SemaphoreType.DMA(())],
        compiler_params=_cparams("arbitrary", "arbitrary", "arbitrary"), name="moe_combine",
    )(idx.reshape(batch * n_exp, 1, cap), ys, x)


def _moe_residual(x, xn, logits, g2, w1, w3, w2, layer, batch, seq):
    d = xn.shape[1]
    capacity = EC_CAPACITY_FACTOR * seq // N_EXPERTS
    aff = jax.nn.softmax(logits.reshape(batch, seq, N_EXPERTS), axis=-1)
    gate, idx = lax.top_k(jnp.swapaxes(aff, 1, 2), capacity)
    bidx = jnp.arange(batch)[:, None, None]
    xs = xn.reshape(batch, seq, d)[bidx, idx]
    ys = _moe_ffn(xs, gate[..., None], g2, w1, w3, w2, layer)
    return _moe_combine(x.reshape(batch, seq, d), ys, idx).reshape(batch * seq, d)


def _in_proj_weights(w_in_l):
    o1, o2, o3 = SWA_COLS, SWA_COLS + MLA_COLS, SWA_COLS + MLA_COLS + RWKV_COLS
    w_mla = jnp.zeros((w_in_l.shape[0], MLA_COLS_PAD), f32).at[:, :MLA_COLS].set(w_in_l[:, o1:o2])
    return (w_in_l[:, :o1].astype(bf16), w_mla.astype(bf16),
            w_in_l[:, o2:o3].astype(bf16), w_in_l[:, o3:].astype(bf16))


def kernel(x, c, positions, ada_w, ada_b, norm1_g, norm2_g, w_in, w_out, swa_q_gain, swa_k_gain, swa_sink, mla_q_a_gain, mla_kv_a_gain, mla_w_uq, mla_w_ukv, mla_q_gain, mla_k_gain, rwkv_mu, rwkv_w0, rwkv_w_up, rwkv_a0, rwkv_a_up, rwkv_g_up, rwkv_k_k, rwkv_k_a, rwkv_r_k, rwkv_ln_w, rwkv_ln_b, s5_a_re, s5_a_im, s5_log_dt, s5_b_re, s5_b_im, s5_c_re, s5_c_im, s5_d, s5_glu_w, s5_glu_b, router_w, moe_w1, moe_w3, moe_w2):
    batch, seq, d = x.shape
    depth = ada_w.shape[0]
    t = batch * seq
    cos_h, sin_h, cos_r, sin_r = _rope_tables(positions)
    mod = _ada_mod(c, ada_w, ada_b)
    xf = x.reshape(t, d)
    for l in range(depth):
        sh1, sc1, g1, sh2, sc2, g2 = [m[:, None, :] for m in jnp.split(mod[l], 6, axis=-1)]
        xn = _norm_call(xf, norm1_g[l], sc1, sh1, seq)
        w_swa, w_mla, w_rwkv, w_s5 = _in_proj_weights(w_in[l])
        h_swa = _matmul(xn, w_swa, tm=1024, tn=SWA_COLS, name="in_proj_swa")
        h_mla = _matmul(xn, w_mla, tm=1024, tn=MLA_COLS_PAD, name="in_proj_mla")
        h_rwkv = _matmul(xn, w_rwkv, tm=1024, tn=640, name="in_proj_rwkv")
        h_s5 = _matmul(xn, w_s5, tm=1024, tn=GROUP_WIDTH, name="in_proj_s5")
        o_swa = _swa_mixer(h_swa, cos_h, sin_h, swa_q_gain[l], swa_k_gain[l], swa_sink[l], batch, seq)
        o_mla = _mla_mixer(h_mla, cos_r, sin_r, mla_q_a_gain[l], mla_kv_a_gain[l], mla_w_uq[l],
                           mla_w_ukv[l], mla_q_gain[l], mla_k_gain[l], batch, seq)
        o_rwkv = _rwkv_mixer(h_rwkv, rwkv_mu[l], rwkv_w0[l], rwkv_w_up[l], rwkv_a0[l], rwkv_a_up[l],
                             rwkv_g_up[l], rwkv_k_k[l], rwkv_k_a[l], rwkv_r_k[l],
                             rwkv_ln_w[l], rwkv_ln_b[l], batch, seq)
        o_s5 = _s5_mixer(h_s5, s5_a_re[l], s5_a_im[l], s5_log_dt[l], s5_b_re[l], s5_b_im[l],
                         s5_c_re[l], s5_c_im[l], s5_d[l], s5_glu_w[l], s5_glu_b[l], batch, seq)
        xf = _outproj((o_swa, o_mla, o_rwkv, o_s5), w_out[l], xf, g1, seq)
        xn2, logits = _norm_call(xf, norm2_g[l], sc2, sh2, seq, router_w=router_w[l])
        xf = _moe_residual(xf, xn2, logits, g2, moe_w1, moe_w3, moe_w2, l, batch, seq)
    return xf.reshape(batch, seq, d)
```

```python
import functools
import math

import jax
import jax.numpy as jnp
from jax import lax
from jax.experimental import pallas as pl
from jax.experimental.pallas import tpu as pltpu

f32 = jnp.float32
bf16 = jnp.bfloat16

D_MODEL = 2048
HEAD_DIM = 64
GROUP_WIDTH = 512
ROPE_THETA = 10000.0
NORM_EPS = 1e-6
NEG_INF = -1e30

SWA_HEADS = 8
SWA_KV_HEADS = 2
SWA_WINDOW = 128
SWA_BLOCK = 128
SWA_Q = 512
SWA_KV = 128
SWA_COLS = 768
SWA_BLOCKS_PER_STEP = 4

MLA_HEADS = 8
MLA_NOPE = 64
MLA_ROPE = 32
MLA_V = 64
MLA_QK = 96
MLA_Q_RANK = 384
MLA_KV_RANK = 128
MLA_COLS = 544
MLA_COLS_PAD = 640
MLA_HEAD_PAD = 128

RWKV_HEADS = 8
RWKV_LORA = 64
RWKV_GATE_LORA = 128
RWKV_GN_EPS = 64e-5
RWKV_COLS = 1920
RWKV_CHUNK = 64
RWKV_CHUNKS_PER_STEP = 4
RWKV_GROUP_LANES = 256

S5_GROUP = 16
S5_GROUPS = 32
S5_STATE = 64
S5_CHUNK = 32

N_EXPERTS = 16
EC_CAPACITY_FACTOR = 2
D_FF_EXPERT = 1024

VMEM_LIMIT_BYTES = 52 * 1024 * 1024
MOE_DOWN_VMEM_LIMIT_BYTES = 57 * 1024 * 1024
LANES = 128


def _cparams(*sem):
    return pltpu.CompilerParams(dimension_semantics=sem, vmem_limit_bytes=VMEM_LIMIT_BYTES)


def _sigmoid(x):
    return 1.0 / (1.0 + jnp.exp(-x))


def _split_bf16(x):
    hi = x.astype(bf16)
    lo = (x - hi.astype(f32)).astype(bf16)
    return hi, lo


def _dot(a, b):
    return jnp.dot(a, b, preferred_element_type=f32)


def _dot_nt(a, b):
    return lax.dot_general(a, b, (((1,), (1,)), ((), ())), preferred_element_type=f32)


def _dot_tn(a, b):
    return lax.dot_general(a, b, (((0,), (0,)), ((), ())), preferred_element_type=f32)


def _ada_kernel(c_ref, w_ref, b_ref, o_ref):
    c = c_ref[...]
    cond = (c * _sigmoid(c)).astype(bf16)
    o_ref[0] = _dot(cond, w_ref[0].astype(bf16)) + b_ref[0]


def _ada_mod(c, ada_w, ada_b):
    depth, d, n = ada_w.shape
    b = c.shape[0]
    rows = 8
    c_pad = jnp.zeros((rows, d), f32).at[:b].set(c)
    tn = 512
    out = pl.pallas_call(
        _ada_kernel,
        grid=(depth, n // tn),
        in_specs=[
            pl.BlockSpec((rows, d), lambda l, j: (0, 0)),
            pl.BlockSpec((1, d, tn), lambda l, j: (l, 0, j)),
            pl.BlockSpec((1, 1, tn), lambda l, j: (l, 0, j)),
        ],
        out_specs=pl.BlockSpec((1, rows, tn), lambda l, j: (l, 0, j)),
        out_shape=jax.ShapeDtypeStruct((depth, rows, n), f32),
        compiler_params=_cparams("parallel", "parallel"),
        name="ada_mod",
    )(c_pad, ada_w, ada_b.reshape(depth, 1, n))
    return out[:, :b]


def _mm_kernel(x_ref, w_ref, o_ref):
    o_ref[...] = _dot(x_ref[...].astype(bf16), w_ref[...].astype(bf16)).astype(o_ref.dtype)


def _matmul(x, w, *, tm, tn, name):
    m, k = x.shape
    n = w.shape[1]
    tm = min(tm, m)
    return pl.pallas_call(
        _mm_kernel,
        grid=(m // tm, n // tn),
        in_specs=[pl.BlockSpec((tm, k), lambda i, j: (i, 0)),
                  pl.BlockSpec((k, tn), lambda i, j: (0, j))],
        out_specs=pl.BlockSpec((tm, tn), lambda i, j: (i, j)),
        out_shape=jax.ShapeDtypeStruct((m, n), f32),
        compiler_params=_cparams("parallel", "parallel"),
        name=name,
    )(x, w)


def _norm_mod(x, g_ref, sc_ref, sh_ref):
    ms = jnp.mean(x * x, axis=-1, keepdims=True)
    y = x * lax.rsqrt(ms + NORM_EPS) * g_ref[...]
    return y * (1.0 + sc_ref[0]) + sh_ref[0]


def _norm_kernel(x_ref, g_ref, sc_ref, sh_ref, o_ref):
    o_ref[...] = _norm_mod(x_ref[...], g_ref, sc_ref, sh_ref).astype(o_ref.dtype)


def _norm_router_kernel(x_ref, g_ref, sc_ref, sh_ref, whi_ref, wlo_ref, o_ref, logit_ref):
    y = _norm_mod(x_ref[...], g_ref, sc_ref, sh_ref)
    o_ref[...] = y.astype(o_ref.dtype)
    yhi, ylo = _split_bf16(y)
    whi = whi_ref[...]
    logit_ref[...] = _dot(yhi, whi) + _dot(ylo, whi) + _dot(yhi, wlo_ref[...])


def _norm_call(x, gain, scale, shift, seq, router_w=None):
    t, d = x.shape
    tm = min(512, seq)
    per_batch = seq // tm
    row = pl.BlockSpec((tm, d), lambda i: (i, 0))
    mod = pl.BlockSpec((1, 1, d), lambda i: (i // per_batch, 0, 0))
    in_specs = [row, pl.BlockSpec((1, d), lambda i: (0, 0)), mod, mod]
    args = [x, gain.reshape(1, d), scale, shift]
    if router_w is None:
        return pl.pallas_call(
            _norm_kernel, grid=(t // tm,), in_specs=in_specs, out_specs=row,
            out_shape=jax.ShapeDtypeStruct((t, d), bf16),
            compiler_params=_cparams("parallel"), name="norm_mod",
        )(*args)
    e = router_w.shape[1]
    whi, wlo = _split_bf16(jnp.zeros((d, LANES), f32).at[:, :e].set(router_w))
    wspec = pl.BlockSpec((d, LANES), lambda i: (0, 0))
    xn, logits = pl.pallas_call(
        _norm_router_kernel, grid=(t // tm,), in_specs=in_specs + [wspec, wspec],
        out_specs=[row, pl.BlockSpec((tm, LANES), lambda i: (i, 0))],
        out_shape=[jax.ShapeDtypeStruct((t, d), bf16), jax.ShapeDtypeStruct((t, LANES), f32)],
        compiler_params=_cparams("parallel"), name="norm_mod_router",
    )(*args, whi, wlo)
    return xn, logits[:, :e]


def _rope_lanes(x, cos_t, sin_t, half, first_mask):
    n = x.shape[-1]
    fwd = pltpu.roll(x, n - half, axis=1)
    bwd = pltpu.roll(x, half, axis=1)
    return x * cos_t + jnp.where(first_mask, fwd, bwd) * sin_t


def _rope_tables(positions):
    t = positions.size
    pos = positions.reshape(t, 1).astype(f32)

    def tables(dim):
        inv_freq = ROPE_THETA ** (-jnp.arange(0, dim, 2, dtype=f32) / dim)
        ang = pos * inv_freq
        return jnp.cos(ang), jnp.sin(ang)

    c, s = tables(HEAD_DIM)
    cos_h = jnp.concatenate([c, c, c, c], axis=-1)
    sin_h = jnp.concatenate([-s, s, -s, s], axis=-1)
    c, s = tables(MLA_ROPE)
    one = jnp.ones((t, MLA_NOPE), f32)
    zero = jnp.zeros((t, MLA_NOPE), f32)
    pad1 = jnp.ones((t, MLA_HEAD_PAD - MLA_QK), f32)
    pad0 = jnp.zeros((t, MLA_HEAD_PAD - MLA_QK), f32)
    cos_r = jnp.concatenate([one, c, c, pad1], axis=-1)
    sin_r = jnp.concatenate([zero, -s, s, pad0], axis=-1)
    return cos_h, sin_h, cos_r, sin_r


def _pair_rmsnorm(x, gain):
    lane = lax.broadcasted_iota(jnp.int32, x.shape, 1)
    lo = lane < HEAD_DIM
    x2 = x * x
    s_lo = jnp.sum(jnp.where(lo, x2, 0.0), axis=-1, keepdims=True)
    s_hi = jnp.sum(jnp.where(lo, 0.0, x2), axis=-1, keepdims=True)
    ms = jnp.where(lo, s_lo, s_hi) * (1.0 / HEAD_DIM)
    return x * lax.rsqrt(ms + NORM_EPS) * gain


def _swa_prep_kernel(h_ref, cos_ref, sin_ref, qg_ref, kg_ref, q_ref, k_ref, v_ref):
    cos_t = cos_ref[...]
    sin_t = sin_ref[...]
    lane = lax.broadcasted_iota(jnp.int32, cos_t.shape, 1)
    first = (lane % HEAD_DIM) < (HEAD_DIM // 2)
    scale = HEAD_DIM ** -0.5
    for p in range(SWA_Q // LANES):
        x = _pair_rmsnorm(h_ref[:, p * LANES:(p + 1) * LANES], qg_ref[...])
        x = _rope_lanes(x, cos_t, sin_t, HEAD_DIM // 2, first)
        q_ref[:, p * LANES:(p + 1) * LANES] = (x * scale).astype(bf16)
    x = _pair_rmsnorm(h_ref[:, SWA_Q:SWA_Q + SWA_KV], kg_ref[...])
    k_ref[...] = _rope_lanes(x, cos_t, sin_t, HEAD_DIM // 2, first).astype(bf16)
    v_ref[...] = h_ref[:, SWA_Q + SWA_KV:SWA_COLS].astype(bf16)


def _swa_attn_kernel(q_ref, k_ref, v_ref, sink_ref, o_ref, *, seq, nblk):
    step = pl.program_id(1)
    band = 3 * SWA_BLOCK
    grp = SWA_HEADS // SWA_KV_HEADS
    rows = grp * SWA_BLOCK
    row_in_blk = lax.broadcasted_iota(jnp.int32, (rows, band), 0) % SWA_BLOCK
    col = lax.broadcasted_iota(jnp.int32, (rows, band), 1)
    sinks = [jnp.concatenate(
        [jnp.broadcast_to(sink_ref[kh * grp + g:kh * grp + g + 1, 0:1], (SWA_BLOCK, 1)) for g in range(grp)],
        axis=0) for kh in range(SWA_KV_HEADS)]
    qs, ks, vs, valid, sink = [], [], [], [], []
    for i in range(nblk):
        n = step * nblk + i
        start = pl.multiple_of(jnp.clip((n - 1) * SWA_BLOCK, 0, seq - band), SWA_BLOCK)
        kb = k_ref[pl.ds(start, band), :]
        vb = v_ref[pl.ds(start, band), :]
        q = q_ref[i * SWA_BLOCK:(i + 1) * SWA_BLOCK, :]
        ok = jnp.abs(n * SWA_BLOCK + row_in_blk - (start + col)) <= SWA_WINDOW
        for kh in range(SWA_KV_HEADS):
            qs.append(jnp.concatenate(
                [q[:, (kh * grp + g) * HEAD_DIM:(kh * grp + g + 1) * HEAD_DIM] for g in range(grp)], axis=0))
            ks.append(kb[:, kh * HEAD_DIM:(kh + 1) * HEAD_DIM])
            vs.append(vb[:, kh * HEAD_DIM:(kh + 1) * HEAD_DIM])
            valid.append(ok)
            sink.append(sinks[kh])
    sink = jnp.stack(sink)
    s = jnp.where(jnp.stack(valid), _bmm_nt(jnp.stack(qs), jnp.stack(ks)), NEG_INF)
    m = jnp.maximum(jnp.max(s, axis=-1, keepdims=True), sink)
    p = jnp.exp(s - m)
    denom = jnp.sum(p, axis=-1, keepdims=True) + jnp.exp(sink - m)
    o = _bmm(p.astype(bf16), jnp.stack(vs)) / denom
    for i in range(nblk):
        o_ref[i * SWA_BLOCK:(i + 1) * SWA_BLOCK, :] = jnp.concatenate(
            [o[i * SWA_KV_HEADS + kh, g * SWA_BLOCK:(g + 1) * SWA_BLOCK]
             for kh in range(SWA_KV_HEADS) for g in range(grp)], axis=-1)


def _swa_mixer(h_swa, cos_h, sin_h, q_gain, k_gain, sink, batch, seq):
    t = h_swa.shape[0]
    tm = min(512, seq)
    row = lambda w: pl.BlockSpec((tm, w), lambda i: (i, 0))
    const = pl.BlockSpec((1, LANES), lambda i: (0, 0))
    q, k, v = pl.pallas_call(
        _swa_prep_kernel, grid=(t // tm,),
        in_specs=[row(SWA_COLS), row(LANES), row(LANES), const, const],
        out_specs=[row(SWA_Q), row(SWA_KV), row(SWA_KV)],
        out_shape=[jax.ShapeDtypeStruct((t, SWA_Q), bf16),
                   jax.ShapeDtypeStruct((t, SWA_KV), bf16),
                   jax.ShapeDtypeStruct((t, SWA_KV), bf16)],
        compiler_params=_cparams("parallel"), name="swa_prep",
    )(h_swa, cos_h, sin_h, jnp.tile(q_gain, 2).reshape(1, LANES), jnp.tile(k_gain, 2).reshape(1, LANES))
    nblk = SWA_BLOCKS_PER_STEP
    tq = nblk * SWA_BLOCK
    nb = seq // tq
    sink_t = jnp.broadcast_to(sink.astype(f32).reshape(SWA_HEADS, 1), (SWA_HEADS, LANES))
    return pl.pallas_call(
        functools.partial(_swa_attn_kernel, seq=seq, nblk=nblk), grid=(batch, nb),
        in_specs=[pl.BlockSpec((tq, SWA_Q), lambda b, n: (b * nb + n, 0)),
                  pl.BlockSpec((seq, SWA_KV), lambda b, n: (b, 0)),
                  pl.BlockSpec((seq, SWA_KV), lambda b, n: (b, 0)),
                  pl.BlockSpec((SWA_HEADS, LANES), lambda b, n: (0, 0))],
        out_specs=pl.BlockSpec((tq, SWA_Q), lambda b, n: (b * nb + n, 0)),
        out_shape=jax.ShapeDtypeStruct((t, SWA_Q), f32),
        compiler_params=_cparams("parallel", "parallel"), name="swa_attn",
    )(q, k, v, sink_t)


def _mla_prep_kernel(h_ref, cos_ref, sin_ref, qag_ref, kvag_ref, wq_ref, wk_ref, wpe_ref, wv_ref,
                     qg_ref, kg_ref, q_ref, k_ref, v_ref):
    cos_t = cos_ref[...]
    sin_t = sin_ref[...]
    lane = lax.broadcasted_iota(jnp.int32, cos_t.shape, 1)
    first = lane < MLA_NOPE + MLA_ROPE // 2
    cq = h_ref[:, :MLA_Q_RANK]
    cq = cq * lax.rsqrt(jnp.mean(cq * cq, axis=-1, keepdims=True) + NORM_EPS) * qag_ref[...]
    ckv = h_ref[:, MLA_Q_RANK:MLA_Q_RANK + MLA_KV_RANK]
    ckv = (ckv * lax.rsqrt(jnp.mean(ckv * ckv, axis=-1, keepdims=True) + NORM_EPS) * kvag_ref[...]).astype(bf16)
    kpe = h_ref[:, MLA_Q_RANK + MLA_KV_RANK:MLA_COLS_PAD].astype(bf16)
    q_all = _dot(cq.astype(bf16), wq_ref[...])
    k_all = _dot(ckv, wk_ref[...]) + _dot(kpe, wpe_ref[...])
    v_all = _dot(ckv, wv_ref[...])
    one_lane = lane == MLA_V
    scale = MLA_QK ** -0.5 * math.log2(math.e)

    def head_norm(x, gain):
        ms = jnp.sum(x * x, axis=-1, keepdims=True) * (1.0 / MLA_QK)
        x = x * lax.rsqrt(ms + NORM_EPS) * gain
        return _rope_lanes(x, cos_t, sin_t, MLA_ROPE // 2, first)

    for h in range(MLA_HEADS):
        sl = slice(h * MLA_HEAD_PAD, (h + 1) * MLA_HEAD_PAD)
        q_ref[0, h] = (head_norm(q_all[:, sl], qg_ref[...]) * scale).astype(bf16)
        k_ref[0, h] = head_norm(k_all[:, sl], kg_ref[...]).astype(bf16)
        v_ref[0, h] = jnp.where(one_lane, 1.0, v_all[:, sl]).astype(bf16)


def _mla_attn_kernel(q_ref, k_ref, v_ref, o_ref, *, tk, nk):
    n_heads = q_ref.shape[1]
    tq = q_ref.shape[2]
    qs = [q_ref[0, hh] for hh in range(n_heads)]

    def body(j, carry):
        off = pl.multiple_of(j * tk, tk)
        new = []
        for hh in range(n_heads):
            m, acc = carry[hh]
            kj = k_ref[0, hh, pl.ds(off, tk), :]
            vj = v_ref[0, hh, pl.ds(off, tk), :]
            s = _dot_nt(qs[hh], kj)
            m_new = jnp.maximum(m, jnp.max(s, axis=-1, keepdims=True))
            alpha = jnp.exp2(m - m_new)
            p = jnp.exp2(s - m_new)
            new.append((m_new, acc * alpha + _dot(p.astype(bf16), vj)))
        return tuple(new)

    init = tuple((jnp.full((tq, 1), NEG_INF, f32), jnp.zeros((tq, MLA_HEAD_PAD), f32))
                 for _ in range(n_heads))
    final = lax.fori_loop(0, nk, body, init)
    o_ref[0] = jnp.concatenate([acc[:, :MLA_V] / acc[:, MLA_V:MLA_V + 1] for _, acc in final], axis=-1)


def _mla_weights(w_uq, w_ukv, q_gain, k_gain):
    hp = MLA_HEAD_PAD
    wq = jnp.zeros((MLA_Q_RANK, MLA_HEADS, hp), f32).at[:, :, :MLA_QK].set(
        w_uq.reshape(MLA_Q_RANK, MLA_HEADS, MLA_QK))
    kv = w_ukv.reshape(MLA_KV_RANK, MLA_HEADS, MLA_NOPE + MLA_V)
    wk = jnp.zeros((MLA_KV_RANK, MLA_HEADS, hp), f32).at[:, :, :MLA_NOPE].set(kv[:, :, :MLA_NOPE])
    wv = jnp.zeros((MLA_KV_RANK, MLA_HEADS, hp), f32).at[:, :, :MLA_V].set(kv[:, :, MLA_NOPE:])
    eye = jnp.eye(MLA_ROPE, dtype=f32)
    wpe = jnp.zeros((LANES, MLA_HEADS, hp), f32).at[:MLA_ROPE, :, MLA_NOPE:MLA_QK].set(
        jnp.broadcast_to(eye[:, None, :], (MLA_ROPE, MLA_HEADS, MLA_ROPE)))
    flat = lambda w: w.reshape(w.shape[0], MLA_HEADS * hp).astype(bf16)
    pad = lambda g: jnp.zeros((1, hp), f32).at[0, :MLA_QK].set(g)
    return flat(wq), flat(wk), flat(wpe), flat(wv), pad(q_gain), pad(k_gain)


def _mla_mixer(h_mla, cos_r, sin_r, q_a_gain, kv_a_gain, w_uq, w_ukv, q_gain, k_gain, batch, seq):
    t = h_mla.shape[0]
    tm = min(512, seq)
    per_batch = seq // tm
    hp = MLA_HEAD_PAD
    wq, wk, wpe, wv, qg, kg = _mla_weights(w_uq, w_ukv, q_gain, k_gain)
    row = lambda w: pl.BlockSpec((tm, w), lambda b, i: (b * per_batch + i, 0))
    const = lambda a: pl.BlockSpec(a.shape, lambda b, i: (0,) * a.ndim)
    head_out = pl.BlockSpec((1, MLA_HEADS, tm, hp), lambda b, i: (b, 0, i, 0))
    head_shape = jax.ShapeDtypeStruct((batch, MLA_HEADS, seq, hp), bf16)
    qag = q_a_gain.reshape(1, MLA_Q_RANK)
    kvag = kv_a_gain.reshape(1, MLA_KV_RANK)
    q, k, v = pl.pallas_call(
        _mla_prep_kernel, grid=(batch, per_batch),
        in_specs=[row(MLA_COLS_PAD), row(LANES), row(LANES), const(qag), const(kvag),
                  const(wq), const(wk), const(wpe), const(wv), const(qg), const(kg)],
        out_specs=[head_out, head_out, head_out],
        out_shape=[head_shape, head_shape, head_shape],
        compiler_params=_cparams("parallel", "parallel"), name="mla_prep",
    )(h_mla, cos_r, sin_r, qag, kvag, wq, wk, wpe, wv, qg, kg)
    tq = min(1024, seq)
    tk = min(2048, seq)
    kv_spec = pl.BlockSpec((1, 2, seq, hp), lambda b, h, i: (b, h, 0, 0))
    o = pl.pallas_call(
        functools.partial(_mla_attn_kernel, tk=tk, nk=seq // tk),
        grid=(batch, MLA_HEADS // 2, seq // tq),
        in_specs=[pl.BlockSpec((1, 2, tq, hp), lambda b, h, i: (b, h, i, 0)), kv_spec, kv_spec],
        out_specs=pl.BlockSpec((1, tq, 2 * MLA_V), lambda b, h, i: (b, i, h)),
        out_shape=jax.ShapeDtypeStruct((batch, seq, MLA_HEADS * MLA_V), f32),
        compiler_params=_cparams("parallel", "parallel", "parallel"), name="mla_attn",
    )(q, k, v)
    return o.reshape(t, MLA_HEADS * MLA_V)


def _head_sum(x, bd):
    hi, lo = _split_bf16(x)
    return _dot(hi, bd) + _dot(lo, bd)


def _rwkv_prep_kernel(h_ref, prev_ref, next_ref, mu_ref, w0_ref, wup_ref, a0_ref, aup_ref, gup_ref,
                      kk_ref, ka_ref, rk_ref, bd_ref,
                      shared_ref, g_ref, bonus_ref, dir_ref, *, per_batch):
    i = pl.program_id(0)
    x = h_ref[...]
    tm = x.shape[0]
    row = lax.broadcasted_iota(jnp.int32, x.shape, 0)
    prev_row = jnp.where(i % per_batch == 0, 0.0, prev_ref[7:8, :])
    next_row = jnp.where(i % per_batch == per_batch - 1, 0.0, next_ref[0:1, :])
    x_prev = jnp.where(row == 0, prev_row, pltpu.roll(x, 1, axis=0))
    x_next = jnp.where(row == tm - 1, next_row, pltpu.roll(x, tm - 1, axis=0))
    hs = x + mu_ref[...] * (0.5 * (x_prev + x_next) - x)
    gw = GROUP_WIDTH
    r = hs[:, 0:gw]
    k = hs[:, gw:2 * gw]
    v = hs[:, 2 * gw:3 * gw]
    wd = hs[:, 3 * gw:3 * gw + LANES]
    ad = hs[:, 3 * gw + LANES:3 * gw + 2 * LANES]
    gd = hs[:, 3 * gw + 2 * LANES:RWKV_COLS]
    bd = bd_ref[...]
    lora_w = _dot(jnp.tanh(wd).astype(bf16), wup_ref[...])
    lora_a = _dot(ad.astype(bf16), aup_ref[...])
    kk = k * kk_ref[...]
    kk = kk * lax.rsqrt(_head_sum(kk * kk, bd) + 1e-12)
    shared_ref[:, 0:gw] = r
    shared_ref[:, gw:2 * gw] = v
    shared_ref[:, 2 * gw:3 * gw] = -kk
    g_ref[...] = _dot(_sigmoid(gd).astype(bf16), gup_ref[...])
    bonus_ref[...] = _head_sum(r * k * rk_ref[...], bd) * v
    for e in range(2):
        z = -(w0_ref[e:e + 1, :] + lora_w[:, e * gw:(e + 1) * gw])
        softplus = jnp.maximum(z, 0.0) + jnp.log(1.0 + jnp.exp(-jnp.abs(z)))
        dir_ref[e, :, 0:gw] = -jnp.exp(-softplus - 0.5)
        a = _sigmoid(a0_ref[e:e + 1, :] + lora_a[:, e * gw:(e + 1) * gw])
        dir_ref[e, :, gw:2 * gw] = k * (1.0 + (a - 1.0) * ka_ref[...])
        dir_ref[e, :, 2 * gw:3 * gw] = kk * a


def _bmm(a, b):
    return lax.dot_general(a, b, (((2,), (1,)), ((0,), (0,))), preferred_element_type=f32)


def _bmm_nt(a, b):
    return lax.dot_general(a, b, (((2,), (2,)), ((0,), (0,))), preferred_element_type=f32)


def _bmm_tn(a, b):
    return lax.dot_general(a, b, (((1,), (1,)), ((0,), (0,))), preferred_element_type=f32)


def _rwkv_chunk_kernel(shared_f, shared_b, dir_f, dir_b, o_f, o_b, h_scr, *, batch, ch):
    L = RWKV_CHUNK
    N = HEAD_DIM
    W = RWKV_GROUP_LANES
    hpg = W // N
    n_grp = GROUP_WIDTH // W

    @pl.when(pl.program_id(0) == 0)
    def _():
        h_scr[...] = jnp.zeros_like(h_scr)

    units = [(j, d, b, g) for j in range(ch) for d in range(2) for b in range(batch) for g in range(n_grp)]
    per_step = 2 * batch * n_grp
    n_units = len(units)

    def rows_of(j, d):
        return pl.ds((ch - 1 - j if d else j) * L, L)

    def load(ref_f, ref_b, lead, field):
        return jnp.stack([(ref_b if d else ref_f)[lead + (b, rows_of(j, d),
                                                          pl.ds(field * GROUP_WIDTH + g * W, W))]
                          for (j, d, b, g) in units])

    r = load(shared_f, shared_b, (), 0)
    v = load(shared_f, shared_b, (), 1)
    an = load(shared_f, shared_b, (), 2)
    lw = load(dir_f, dir_b, (0,), 0)
    kd = load(dir_f, dir_b, (0,), 1)
    bdir = load(dir_f, dir_b, (0,), 2)

    shape = (n_units, L, W)
    ui = lax.broadcasted_iota(jnp.int32, shape, 0)
    ti = lax.broadcasted_iota(jnp.int32, shape, 1)
    si = lax.broadcasted_iota(jnp.int32, shape, 2) % N
    sign = 1 - 2 * ((ui // (batch * n_grp)) % 2)
    ahead = (ti - si) * sign
    strict = ahead > 0
    incl = ahead >= 0
    bi = lax.broadcasted_iota(jnp.int32, (W, W), 0) // N
    bj = lax.broadcasted_iota(jnp.int32, (W, W), 1) // N
    diag_blocks = bi == bj

    def bdiag(x):
        return jnp.where(diag_blocks, jnp.concatenate([x] * hpg, axis=1), 0.0).astype(bf16)

    t2 = lax.broadcasted_iota(jnp.int32, (L, L), 0)
    s2 = lax.broadcasted_iota(jnp.int32, (L, L), 1)
    cum_parts = [None] * n_units
    for d in range(2):
        sel = [i for i, u in enumerate(units) if u[1] == d]
        tri = jnp.where((s2 >= t2) if d else (s2 <= t2), 1.0, 0.0).astype(bf16)
        hi, lo = _split_bf16(jnp.concatenate([lw[i] for i in sel], axis=1))
        cum_cat = _dot(tri, hi) + _dot(tri, lo)
        for k, i in enumerate(sel):
            cum_parts[i] = cum_cat[:, k * W:(k + 1) * W]
    cum = jnp.stack(cum_parts)
    p_incl = jnp.exp(cum)
    p_inv = jnp.exp(-cum)
    p_prev = jnp.exp(cum - lw)
    p_last = jnp.exp(jnp.sum(lw, axis=1, keepdims=True))
    a_t = an * p_prev
    b_t = bdir * p_inv
    k_t = kd * p_inv
    r_t = r * p_incl
    b_h = b_t * p_last
    k_h = k_t * p_last

    ar = jnp.concatenate([a_t, r_t], axis=1).astype(bf16)
    prod_b = _bmm_nt(ar, bdiag(b_t))
    prod_k = _bmm_nt(ar, bdiag(k_t))
    a_ab = jnp.where(strict, prod_b[:, :L], 0.0)
    a_rb = jnp.where(incl, prod_b[:, L:], 0.0).astype(bf16)
    a_ak = jnp.where(strict, prod_k[:, :L], 0.0).astype(bf16)
    a_rk = jnp.where(incl, prod_k[:, L:], 0.0).astype(bf16)

    m = 1
    t_inv = None
    while m < L:
        same = (ti // (2 * m)) == (si // (2 * m))
        later_half = ((ti // m) % 2 - (si // m) % 2) * sign == 1
        a_off = jnp.where(same, jnp.where(later_half, a_ab, 0.0), 0.0)
        if m == 1:
            t_inv = jnp.where(ti == si, 1.0, 0.0) + a_off
        else:
            left = _bmm(t_inv.astype(bf16), bdiag(a_off)).astype(bf16)
            t_inv = t_inv + _bmm(left, bdiag(t_inv))
        m *= 2

    av = _bmm(jnp.concatenate([a_ak, a_rk], axis=1), bdiag(v))
    akv = av[:, :L]
    tb = t_inv.astype(bf16)
    w_mat = _bmm(tb, bdiag(a_t))
    z_mat = _bmm(tb, bdiag(akv))
    q_mat = r_t + _bmm(a_rb, bdiag(w_mat))
    y_mat = _bmm(a_rb, bdiag(z_mat)) + av[:, L:]

    for j in range(ch):
        sl = slice(j * per_step, (j + 1) * per_step)
        h_t = h_scr[...]
        qw = jnp.concatenate([q_mat[sl], w_mat[sl]], axis=1).astype(bf16)
        ou = _bmm_nt(qw, h_t.astype(bf16))
        out = ou[:, :L] + y_mat[sl]
        u_mat = ou[:, L:] + z_mat[sl]
        keys = jnp.concatenate([b_h[sl], k_h[sl]], axis=1).astype(bf16)
        vals = jnp.concatenate([u_mat, v[sl]], axis=1).astype(bf16)
        upd = _bmm_tn(vals, keys)
        h_scr[...] = jnp.where(diag_blocks, p_last[sl] * h_t + upd, 0.0)
        for k, (_, d, b, g) in enumerate(units[sl]):
            (o_b if d else o_f)[b, rows_of(j, d), pl.ds(g * W, W)] = out[k]


def _rwkv_post_kernel(of_ref, ob_ref, bonus_ref, g_ref, lnw_ref, lnb_ref, bd_ref, out_ref):
    y = of_ref[...] + ob_ref[...]
    bd = bd_ref[...]
    mean = _head_sum(y, bd) * (1.0 / HEAD_DIM)
    yc = y - mean
    var = _head_sum(yc * yc, bd) * (1.0 / HEAD_DIM)
    yn = yc * lax.rsqrt(var + RWKV_GN_EPS) * lnw_ref[...] + lnb_ref[...]
    out_ref[...] = (yn + bonus_ref[...]) * g_ref[...]


def _rwkv_mixer(h_rwkv, mu, w0, w_up, a0, a_up, g_up, k_k, k_a, r_k, ln_w, ln_b, batch, seq):
    t = h_rwkv.shape[0]
    gw = GROUP_WIDTH
    tm = min(256, seq)
    per_batch = seq // tm
    n_halo = t // 8
    wup = jnp.zeros((2 * RWKV_LORA, 2 * gw), f32)
    aup = jnp.zeros((2 * RWKV_LORA, 2 * gw), f32)
    for e in range(2):
        wup = wup.at[e * RWKV_LORA:(e + 1) * RWKV_LORA, e * gw:(e + 1) * gw].set(w_up[e])
        aup = aup.at[e * RWKV_LORA:(e + 1) * RWKV_LORA, e * gw:(e + 1) * gw].set(a_up[e])
    head_id = jnp.arange(gw) // HEAD_DIM
    bd = (head_id[:, None] == head_id[None, :]).astype(bf16)
    vec = lambda a: a.reshape(1, -1).astype(f32)
    const = lambda a: pl.BlockSpec(a.shape, lambda i: (0,) * a.ndim)
    consts = [vec(mu), w0.astype(f32), wup.astype(bf16), a0.astype(f32), aup.astype(bf16),
              g_up.astype(bf16), vec(k_k), vec(k_a), vec(r_k), bd]
    row = pl.BlockSpec((tm, gw), lambda i: (i, 0))
    one = jax.ShapeDtypeStruct((t, gw), f32)
    n_field = 3
    shared, g, bonus, per_dir = pl.pallas_call(
        functools.partial(_rwkv_prep_kernel, per_batch=per_batch), grid=(t // tm,),
        in_specs=[pl.BlockSpec((tm, RWKV_COLS), lambda i: (i, 0)),
                  pl.BlockSpec((8, RWKV_COLS), lambda i: (jnp.maximum(i * (tm // 8) - 1, 0), 0)),
                  pl.BlockSpec((8, RWKV_COLS), lambda i: (jnp.minimum((i + 1) * (tm // 8), n_halo - 1), 0)),
                  ] + [const(a) for a in consts],
        out_specs=[pl.BlockSpec((tm, n_field * gw), lambda i: (i, 0)), row, row,
                   pl.BlockSpec((2, tm, n_field * gw), lambda i: (0, i, 0))],
        out_shape=[jax.ShapeDtypeStruct((t, n_field * gw), f32), one, one,
                   jax.ShapeDtypeStruct((2, t, n_field * gw), f32)],
        compiler_params=_cparams("parallel"), name="rwkv_prep",
    )(h_rwkv, h_rwkv, h_rwkv, *consts)

    L = RWKV_CHUNK
    ch = RWKV_CHUNKS_PER_STEP
    rows = ch * L
    n_steps = seq // rows
    n_grp = gw // RWKV_GROUP_LANES
    shared3 = shared.reshape(batch, seq, n_field * gw)
    per_dir4 = per_dir.reshape(2, batch, seq, n_field * gw)
    fwd = lambda w: pl.BlockSpec((batch, rows, w), lambda c: (0, c, 0))
    bwd = lambda w: pl.BlockSpec((batch, rows, w), lambda c: (0, n_steps - 1 - c, 0))
    fwd_dir = pl.BlockSpec((1, batch, rows, n_field * gw), lambda c: (0, 0, c, 0))
    bwd_dir = pl.BlockSpec((1, batch, rows, n_field * gw), lambda c: (1, 0, n_steps - 1 - c, 0))
    o3 = jax.ShapeDtypeStruct((batch, seq, gw), f32)
    o_fwd, o_bwd = pl.pallas_call(
        functools.partial(_rwkv_chunk_kernel, batch=batch, ch=ch), grid=(n_steps,),
        in_specs=[fwd(n_field * gw), bwd(n_field * gw), fwd_dir, bwd_dir],
        out_specs=[fwd(gw), bwd(gw)], out_shape=[o3, o3],
        scratch_shapes=[pltpu.VMEM((2 * batch * n_grp, RWKV_GROUP_LANES, RWKV_GROUP_LANES), f32)],
        compiler_params=_cparams("arbitrary"), name="rwkv_chunk",
    )(shared3, shared3, per_dir4, per_dir4)

    tp = min(512, seq)
    rowp = pl.BlockSpec((tp, gw), lambda i: (i, 0))
    cvec = pl.BlockSpec((1, gw), lambda i: (0, 0))
    return pl.pallas_call(
        _rwkv_post_kernel, grid=(t // tp,),
        in_specs=[rowp, rowp, rowp, rowp, cvec, cvec, pl.BlockSpec((gw, gw), lambda i: (0, 0))],
        out_specs=rowp, out_shape=one,
        compiler_params=_cparams("parallel"), name="rwkv_post",
    )(o_fwd.reshape(t, gw), o_bwd.reshape(t, gw), bonus, g, vec(ln_w), vec(ln_b), bd)


def _s5_operators(a_re, a_im, log_dt, b_re, b_im, c_re, c_im):
    L = S5_CHUNK
    lam_r, lam_i = a_re.astype(f32), a_im.astype(f32)
    dt = jnp.exp(log_dt.astype(f32))[..., None]
    mag = jnp.exp(lam_r * dt)
    abar_r, abar_i = mag * jnp.cos(lam_i * dt), mag * jnp.sin(lam_i * dt)
    den = lam_r * lam_r + lam_i * lam_i
    nr, ni = abar_r - 1.0, abar_i
    coef_r = (nr * lam_r + ni * lam_i) / den
    coef_i = (ni * lam_r - nr * lam_i) / den
    br, bi = b_re.astype(f32)[None], b_im.astype(f32)[None]
    bb_r = coef_r[..., None] * br - coef_i[..., None] * bi
    bb_i = coef_r[..., None] * bi + coef_i[..., None] * br
    steps = jnp.arange(L + 1, dtype=f32)[:, None, None, None]
    pmag = jnp.exp(steps * (lam_r * dt)[None])
    pw_r = pmag * jnp.cos(steps * (lam_i * dt)[None])
    pw_i = pmag * jnp.sin(steps * (lam_i * dt)[None])
    cr, ci = c_re.astype(f32)[None, None], c_im.astype(f32)[None, None]
    cp_r = cr * pw_r[:, :, :, None, :] - ci * pw_i[:, :, :, None, :]
    cp_i = cr * pw_i[:, :, :, None, :] + ci * pw_r[:, :, :, None, :]
    kern = (jnp.einsum('kegop,egpi->kegoi', cp_r, bb_r) - jnp.einsum('kegop,egpi->kegoi', cp_i, bb_i))
    idx = jnp.arange(L)
    kf, kb = kern[:L, 0], kern[:L, 1]
    by_lag = jnp.concatenate([kb[1:][::-1], kf[:1] + kb[:1], kf[1:], jnp.zeros_like(kf[:1])], axis=0)
    lag_tab = jnp.transpose(by_lag, (1, 3, 0, 2)).reshape(S5_GROUPS, S5_GROUP, 2 * L * S5_GROUP)
    g_, h_ = S5_GROUPS, S5_GROUP

    def state_in(pr, pi, e):
        re = pr[..., None] * bb_r[e][None] - pi[..., None] * bb_i[e][None]
        im = pr[..., None] * bb_i[e][None] + pi[..., None] * bb_r[e][None]
        return jnp.transpose(re, (1, 0, 3, 2)), jnp.transpose(im, (1, 0, 3, 2))

    f_re, f_im = state_in(pw_r[L - 1 - idx, 0], pw_i[L - 1 - idx, 0], 0)
    b_re_, b_im_ = state_in(pw_r[idx, 1], pw_i[idx, 1], 1)
    b_sum = jnp.concatenate([f_re, f_im, b_re_, b_im_], axis=-1).reshape(g_, L * h_, 4 * S5_STATE)

    def state_out(k_idx, e):
        re = jnp.transpose(cp_r[k_idx, e], (1, 3, 0, 2))
        im = jnp.transpose(cp_i[k_idx, e], (1, 3, 0, 2))
        return re, -im

    fo_r, fo_i = state_out(idx + 1, 0)
    bo_r, bo_i = state_out(L - idx, 1)
    c_out = jnp.concatenate([fo_r, fo_i, bo_r, bo_i], axis=1).reshape(g_, 4 * S5_STATE, L * h_)
    a_pow = jnp.stack([jnp.stack([pw_r[L, e], pw_i[L, e]]) for e in range(2)])
    return lag_tab, b_sum.astype(bf16), c_out.astype(bf16), a_pow


def _s5_state_in_kernel(u_ref, bsum_ref, s_ref):
    s_ref[0] = _dot(u_ref[0], bsum_ref[0])


def _s5_carry_kernel(sf_r, sf_i, sb_r, sb_i, af_r, af_i, ab_r, ab_i, xf_r, xf_i, xb_r, xb_i, *, nch):
    rows, p = af_r.shape
    zero = jnp.zeros((rows, p), f32)

    def body(i, carry):
        fr, fi, br, bi = carry
        j = nch - 1 - i
        xf_r[i] = fr
        xf_i[i] = fi
        xb_r[j] = br
        xb_i[j] = bi
        ar, ai = af_r[...], af_i[...]
        nfr = ar * fr - ai * fi + sf_r[i]
        nfi = ar * fi + ai * fr + sf_i[i]
        ar, ai = ab_r[...], ab_i[...]
        nbr = ar * br - ai * bi + sb_r[j]
        nbi = ar * bi + ai * br + sb_i[j]
        return nfr, nfi, nbr, nbi

    lax.fori_loop(0, nch, body, (zero, zero, zero, zero))


def _s5_out_kernel(u_ref, x_ref, lag_ref, cout_ref, y_ref):
    L, h_ = S5_CHUNK, S5_GROUP
    tab = lag_ref[0]
    width = tab.shape[1]
    blocks = []
    for s in range(L):
        start = (L - 1 - s) * h_
        win = pltpu.roll(tab, width - start, axis=1) if start else tab
        blocks.append(win[:, :L * h_])
    m_op = jnp.concatenate(blocks, axis=0).astype(bf16)
    y_ref[0] = _dot(u_ref[0], m_op) + _dot(x_ref[0], cout_ref[0])


def _s5_post_kernel(y_ref, u_ref, d_ref, w_ref, b_ref, o_ref):
    u = u_ref[...]
    z = y_ref[...] + d_ref[...] * u
    c0 = math.sqrt(2.0 / math.pi)
    z = 0.5 * z * (1.0 + jnp.tanh(c0 * (z + 0.044715 * (z * z * z))))
    gate = _dot(z.astype(bf16), w_ref[...]) + b_ref[...]
    o_ref[...] = z * _sigmoid(gate)


def _s5_mixer(u, a_re, a_im, log_dt, b_re, b_im, c_re, c_im, d_skip, glu_w, glu_b, batch, seq):
    t = u.shape[0]
    L, g_, h_, p = S5_CHUNK, S5_GROUPS, S5_GROUP, S5_STATE
    nch = seq // L
    rows = batch * nch
    lag_tab, b_sum, c_out, a_pow = _s5_operators(a_re, a_im, log_dt, b_re, b_im, c_re, c_im)
    ug = jnp.transpose(u.reshape(rows, L, g_, h_), (2, 0, 1, 3)).reshape(g_, rows, L * h_).astype(bf16)
    grp = lambda r, c: pl.BlockSpec((1, r, c), lambda g: (g, 0, 0))
    s_in = pl.pallas_call(
        _s5_state_in_kernel, grid=(g_,),
        in_specs=[grp(rows, L * h_), grp(L * h_, 4 * p)], out_specs=grp(rows, 4 * p),
        out_shape=jax.ShapeDtypeStruct((g_, rows, 4 * p), f32),
        compiler_params=_cparams("parallel"), name="s5_state_in",
    )(ug, b_sum)
    s_in = jnp.transpose(s_in.reshape(g_, batch, nch, 4, p), (3, 2, 1, 0, 4)).reshape(4, nch, batch * g_, p)
    coef = jnp.broadcast_to(a_pow[:, :, None], (2, 2, batch, g_, p)).reshape(2, 2, batch * g_, p)
    st = jax.ShapeDtypeStruct((nch, batch * g_, p), f32)
    rb = 8
    seq_spec = pl.BlockSpec((nch, rb, p), lambda i: (0, i, 0))
    coef_spec = pl.BlockSpec((rb, p), lambda i: (i, 0))
    xs = pl.pallas_call(
        functools.partial(_s5_carry_kernel, nch=nch), grid=(batch * g_ // rb,),
        in_specs=[seq_spec] * 4 + [coef_spec] * 4, out_specs=[seq_spec] * 4, out_shape=[st] * 4,
        compiler_params=_cparams("parallel"), name="s5_carry",
    )(s_in[0], s_in[1], s_in[2], s_in[3], coef[0, 0], coef[0, 1], coef[1, 0], coef[1, 1])
    x_cat = jnp.stack(xs).reshape(4, nch, batch, g_, p)
    x_cat = jnp.transpose(x_cat, (3, 2, 1, 0, 4)).reshape(g_, rows, 4 * p).astype(bf16)
    y = pl.pallas_call(
        _s5_out_kernel, grid=(g_,),
        in_specs=[grp(rows, L * h_), grp(rows, 4 * p), grp(h_, 2 * L * h_), grp(4 * p, L * h_)],
        out_specs=grp(rows, L * h_),
        out_shape=jax.ShapeDtypeStruct((g_, rows, L * h_), f32),
        compiler_params=_cparams("parallel"), name="s5_out",
    )(ug, x_cat, lag_tab, c_out)
    y = jnp.transpose(y.reshape(g_, rows, L, h_), (1, 2, 0, 3)).reshape(t, g_ * h_)
    tm = min(512, seq)
    w = g_ * h_
    row = pl.BlockSpec((tm, w), lambda i: (i, 0))
    cvec = pl.BlockSpec((1, w), lambda i: (0, 0))
    return pl.pallas_call(
        _s5_post_kernel, grid=(t // tm,),
        in_specs=[row, row, cvec, pl.BlockSpec((w, w), lambda i: (0, 0)), cvec],
        out_specs=row, out_shape=jax.ShapeDtypeStruct((t, w), f32),
        compiler_params=_cparams("parallel"), name="s5_post",
    )(y, u, d_skip.reshape(1, w).astype(f32), glu_w.astype(bf16), glu_b.reshape(1, w).astype(f32))


def _outproj_kernel(o0_ref, o1_ref, o2_ref, o3_ref, w_ref, x_ref, g_ref, out_ref):
    acc = None
    for i, o_ref in enumerate((o0_ref, o1_ref, o2_ref, o3_ref)):
        part = _dot(o_ref[...].astype(bf16), w_ref[i * GROUP_WIDTH:(i + 1) * GROUP_WIDTH, :])
        acc = part if acc is None else acc + part
    out_ref[...] = x_ref[...] + g_ref[0] * acc


def _outproj(outs, w_out, x, gate, seq):
    t, d = x.shape
    tm = min(512, seq)
    per_batch = seq // tm
    part = pl.BlockSpec((tm, GROUP_WIDTH), lambda i: (i, 0))
    row = pl.BlockSpec((tm, d), lambda i: (i, 0))
    return pl.pallas_call(
        _outproj_kernel, grid=(t // tm,),
        in_specs=[part, part, part, part, pl.BlockSpec((4 * GROUP_WIDTH, d), lambda i: (0, 0)), row,
                  pl.BlockSpec((1, 1, d), lambda i: (i // per_batch, 0, 0))],
        out_specs=row, out_shape=jax.ShapeDtypeStruct((t, d), f32),
        compiler_params=_cparams("parallel"), name="out_proj",
    )(*outs, w_out.astype(bf16), x, gate)


def _moe_up_kernel(xs_ref, w1_ref, w3_ref, hid_ref):
    b = pl.program_id(2)
    xs = xs_ref[b, 0]
    up = _dot(xs, w1_ref[0, 0].astype(bf16))
    lin = _dot(xs, w3_ref[0, 0].astype(bf16))
    hid_ref[0, 0] = (up * _sigmoid(up) * lin).astype(bf16)


def _moe_up(xs, w1, w3, layer):
    batch, n_exp, cap, d = xs.shape
    ff = w1.shape[3]
    tf = 512
    return pl.pallas_call(
        _moe_up_kernel, grid=(n_exp, ff // tf, batch),
        in_specs=[pl.BlockSpec((batch, 1, cap, d), lambda e, s, b: (0, e, 0, 0)),
                  pl.BlockSpec((1, 1, d, tf), lambda e, s, b: (layer, e, 0, s)),
                  pl.BlockSpec((1, 1, d, tf), lambda e, s, b: (layer, e, 0, s))],
        out_specs=pl.BlockSpec((1, 1, cap, tf), lambda e, s, b: (b, e, 0, s)),
        out_shape=jax.ShapeDtypeStruct((batch, n_exp, cap, ff), bf16),
        compiler_params=_cparams("parallel", "parallel", "arbitrary"), name="moe_up",
    )(xs, w1, w3)


def _moe_down_kernel(idx_ref, hid_ref, gate_ref, g2_ref, w2_ref, x_hbm, out_hbm, slab, ys_scr, sem,
                     *, width, group):
    b, h, e = pl.program_id(0), pl.program_id(1), pl.program_id(2)
    cols = pl.ds(pl.multiple_of(h * width, width), width)

    def slab_copy(src, dst):
        return pltpu.make_async_copy(src, dst, sem)

    @pl.when(e == 0)
    def _():
        cp = slab_copy(x_hbm.at[b, :, cols], slab)
        cp.start()
        cp.wait()

    ys_scr[...] = _dot(hid_ref[0, 0], w2_ref[0, 0].astype(bf16)) * gate_ref[0, 0] * g2_ref[0]
    cap = ys_scr.shape[0]

    def body(g, carry):
        r0 = pl.multiple_of(g * group, group)
        tok = [idx_ref[0, 0, r0 + k] for k in range(group)]
        rows = [slab[pl.ds(tok[k], 1), :] + ys_scr[pl.ds(r0 + k, 1), :] for k in range(group)]
        for k in range(group):
            slab[pl.ds(tok[k], 1), :] = rows[k]
        return carry

    lax.fori_loop(0, cap // group, body, 0)

    @pl.when(e == pl.num_programs(2) - 1)
    def _():
        cp = slab_copy(slab, out_hbm.at[b, :, cols])
        cp.start()
        cp.wait()


def _moe_down_combine(x, hid, gate, g2, w2, idx, layer):
    batch, seq, d = x.shape
    n_exp, cap = idx.shape[1], idx.shape[2]
    ff = hid.shape[3]
    width = d // 2
    return pl.pallas_call(
        functools.partial(_moe_down_kernel, width=width, group=8),
        grid=(batch, d // width, n_exp),
        in_specs=[pl.BlockSpec((1, 1, cap), lambda b, h, e: (b * n_exp + e, 0, 0), memory_space=pltpu.SMEM),
                  pl.BlockSpec((1, 1, cap, ff), lambda b, h, e: (b, e, 0, 0)),
                  pl.BlockSpec((1, 1, cap, 1), lambda b, h, e: (b, e, 0, 0)),
                  pl.BlockSpec((1, 1, width), lambda b, h, e: (b, 0, h)),
                  pl.BlockSpec((1, 1, ff, width), lambda b, h, e: (layer, e, 0, h)),
                  pl.BlockSpec(memory_space=pl.ANY)],
        out_specs=pl.BlockSpec(memory_space=pl.ANY),
        out_shape=jax.ShapeDtypeStruct((batch, seq, d), f32),
        scratch_shapes=[pltpu.VMEM((seq, width), f32), pltpu.VMEM((cap, width), f32),
                        pltpu.SemaphoreType.DMA(())],
        compiler_params=pltpu.CompilerParams(dimension_semantics=("arbitrary",) * 3,
                                             vmem_limit_bytes=MOE_DOWN_VMEM_LIMIT_BYTES),
        name="moe_down_combine",
    )(idx.reshape(batch * n_exp, 1, cap), hid, gate, g2, w2, x)


def _moe_residual(x, xn, logits, g2, w1, w3, w2, layer, batch, seq):
    d = xn.shape[1]
    capacity = EC_CAPACITY_FACTOR * seq // N_EXPERTS
    aff = jax.nn.softmax(logits.reshape(batch, seq, N_EXPERTS), axis=-1)
    gate, idx = lax.top_k(jnp.swapaxes(aff, 1, 2), capacity)
    bidx = jnp.arange(batch)[:, None, None]
    xs = xn.reshape(batch, seq, d)[bidx, idx]
    hid = _moe_up(xs, w1, w3, layer)
    out = _moe_down_combine(x.reshape(batch, seq, d), hid, gate[..., None], g2, w2, idx, layer)
    return out.reshape(batch * seq, d)


def _in_proj_weights(w_in_l):
    o1, o2, o3 = SWA_COLS, SWA_COLS + MLA_COLS, SWA_COLS + MLA_COLS + RWKV_COLS
    w_mla = jnp.zeros((w_in_l.shape[0], MLA_COLS_PAD), f32).at[:, :MLA_COLS].set(w_in_l[:, o1:o2])
    return (w_in_l[:, :o1].astype(bf16), w_mla.astype(bf16),
            w_in_l[:, o2:o3].astype(bf16), w_in_l[:, o3:].astype(bf16))


def kernel(x, c, positions, ada_w, ada_b, norm1_g, norm2_g, w_in, w_out, swa_q_gain, swa_k_gain, swa_sink, mla_q_a_gain, mla_kv_a_gain, mla_w_uq, mla_w_ukv, mla_q_gain, mla_k_gain, rwkv_mu, rwkv_w0, rwkv_w_up, rwkv_a0, rwkv_a_up, rwkv_g_up, rwkv_k_k, rwkv_k_a, rwkv_r_k, rwkv_ln_w, rwkv_ln_b, s5_a_re, s5_a_im, s5_log_dt, s5_b_re, s5_b_im, s5_c_re, s5_c_im, s5_d, s5_glu_w, s5_glu_b, router_w, moe_w1, moe_w3, moe_w2):
    batch, seq, d = x.shape
    depth = ada_w.shape[0]
    t = batch * seq
    cos_h, sin_h, cos_r, sin_r = _rope_tables(positions)
    mod = _ada_mod(c, ada_w, ada_b)
    xf = x.reshape(t, d)
    for l in range(depth):
        sh1, sc1, g1, sh2, sc2, g2 = [m[:, None, :] for m in jnp.split(mod[l], 6, axis=-1)]
        xn = _norm_call(xf, norm1_g[l], sc1, sh1, seq)
        w_swa, w_mla, w_rwkv, w_s5 = _in_proj_weights(w_in[l])
        h_swa = _matmul(xn, w_swa, tm=1024, tn=SWA_COLS, name="in_proj_swa")
        h_mla = _matmul(xn, w_mla, tm=1024, tn=MLA_COLS_PAD, name="in_proj_mla")
        h_rwkv = _matmul(xn, w_rwkv, tm=1024, tn=640, name="in_proj_rwkv")
        h_s5 = _matmul(xn, w_s5, tm=1024, tn=GROUP_WIDTH, name="in_proj_s5")
        o_swa = _swa_mixer(h_swa, cos_h, sin_h, swa_q_gain[l], swa_k_gain[l], swa_sink[l], batch, seq)
        o_mla = _mla_mixer(h_mla, cos_r, sin_r, mla_q_a_gain[l], mla_kv_a_gain[l], mla_w_uq[l],
                           mla_w_ukv[l], mla_q_gain[l], mla_k_gain[l], batch, seq)
        o_rwkv = _rwkv_mixer(h_rwkv, rwkv_mu[l], rwkv_w0[l], rwkv_w_up[l], rwkv_a0[l], rwkv_a_up[l],
                             rwkv_g_up[l], rwkv_k_k[l], rwkv_k_a[l], rwkv_r_k[l],
                             rwkv_ln_w[l], rwkv_ln_b[l], batch, seq)
        o_s5 = _s5_mixer(h_s5, s5_a_re[l], s5_a_im[l], s5_log_dt[l], s5_b_re[l], s5_b_im[l],
                         s5_c_re[l], s5_c_im[l], s5_d[l], s5_glu_w[l], s5_glu_b[l], batch, seq)
        xf = _outproj((o_swa, o_mla, o_rwkv, o_s5), w_out[l], xf, g1, seq)
        xn2, logits = _norm_call(xf, norm2_g[l], sc2, sh2, seq, router_w=router_w[l])
        xf = _moe_residual(xf, xn2, logits, g2, moe_w1, moe_w3, moe_w2, l, batch, seq)
    return xf.reshape(batch, seq, d)
```

```python
import functools
import math

import jax
import jax.numpy as jnp
from jax import lax
from jax.experimental import pallas as pl
from jax.experimental.pallas import tpu as pltpu

f32 = jnp.float32
bf16 = jnp.bfloat16

D_MODEL = 2048
HEAD_DIM = 64
GROUP_WIDTH = 512
ROPE_THETA = 10000.0
NORM_EPS = 1e-6
NEG_INF = -1e30

SWA_HEADS = 8
SWA_KV_HEADS = 2
SWA_WINDOW = 128
SWA_BLOCK = 128
SWA_Q = 512
SWA_KV = 128
SWA_COLS = 768
SWA_BLOCKS_PER_STEP = 4

MLA_HEADS = 8
MLA_NOPE = 64
MLA_ROPE = 32
MLA_V = 64
MLA_QK = 96
MLA_Q_RANK = 384
MLA_KV_RANK = 128
MLA_COLS = 544
MLA_COLS_PAD = 640
MLA_HEAD_PAD = 128

RWKV_HEADS = 8
RWKV_LORA = 64
RWKV_GATE_LORA = 128
RWKV_GN_EPS = 64e-5
RWKV_COLS = 1920
RWKV_CHUNK = 64
RWKV_CHUNKS_PER_STEP = 4
RWKV_GROUP_LANES = 256

S5_GROUP = 16
S5_GROUPS = 32
S5_STATE = 64
S5_CHUNK = 32

N_EXPERTS = 16
EC_CAPACITY_FACTOR = 2
D_FF_EXPERT = 1024

VMEM_LIMIT_BYTES = 52 * 1024 * 1024
MOE_DOWN_VMEM_LIMIT_BYTES = 57 * 1024 * 1024
LANES = 128


def _cparams(*sem):
    return pltpu.CompilerParams(dimension_semantics=sem, vmem_limit_bytes=VMEM_LIMIT_BYTES)


def _sigmoid(x):
    return 1.0 / (1.0 + jnp.exp(-x))


def _split_bf16(x):
    hi = x.astype(bf16)
    lo = (x - hi.astype(f32)).astype(bf16)
    return hi, lo


def _dot(a, b):
    return jnp.dot(a, b, preferred_element_type=f32)


def _dot_nt(a, b):
    return lax.dot_general(a, b, (((1,), (1,)), ((), ())), preferred_element_type=f32)


def _dot_tn(a, b):
    return lax.dot_general(a, b, (((0,), (0,)), ((), ())), preferred_element_type=f32)


def _ada_kernel(c_ref, w_ref, b_ref, o_ref):
    c = c_ref[...]
    cond = (c * _sigmoid(c)).astype(bf16)
    o_ref[0] = _dot(cond, w_ref[0].astype(bf16)) + b_ref[0]


def _ada_mod(c, ada_w, ada_b):
    depth, d, n = ada_w.shape
    b = c.shape[0]
    rows = 8
    c_pad = jnp.zeros((rows, d), f32).at[:b].set(c)
    tn = 512
    out = pl.pallas_call(
        _ada_kernel,
        grid=(depth, n // tn),
        in_specs=[
            pl.BlockSpec((rows, d), lambda l, j: (0, 0)),
            pl.BlockSpec((1, d, tn), lambda l, j: (l, 0, j)),
            pl.BlockSpec((1, 1, tn), lambda l, j: (l, 0, j)),
        ],
        out_specs=pl.BlockSpec((1, rows, tn), lambda l, j: (l, 0, j)),
        out_shape=jax.ShapeDtypeStruct((depth, rows, n), f32),
        compiler_params=_cparams("parallel", "parallel"),
        name="ada_mod",
    )(c_pad, ada_w, ada_b.reshape(depth, 1, n))
    return out[:, :b]


def _mm_kernel(x_ref, w_ref, o_ref):
    o_ref[...] = _dot(x_ref[...].astype(bf16), w_ref[...].astype(bf16)).astype(o_ref.dtype)


def _matmul(x, w, *, tm, tn, name):
    m, k = x.shape
    n = w.shape[1]
    tm = min(tm, m)
    return pl.pallas_call(
        _mm_kernel,
        grid=(m // tm, n // tn),
        in_specs=[pl.BlockSpec((tm, k), lambda i, j: (i, 0)),
                  pl.BlockSpec((k, tn), lambda i, j: (0, j))],
        out_specs=pl.BlockSpec((tm, tn), lambda i, j: (i, j)),
        out_shape=jax.ShapeDtypeStruct((m, n), f32),
        compiler_params=_cparams("parallel", "parallel"),
        name=name,
    )(x, w)


def _norm_mod(x, g_ref, sc_ref, sh_ref):
    ms = jnp.mean(x * x, axis=-1, keepdims=True)
    y = x * lax.rsqrt(ms + NORM_EPS) * g_ref[...]
    return y * (1.0 + sc_ref[0]) + sh_ref[0]


def _norm_kernel(x_ref, g_ref, sc_ref, sh_ref, o_ref):
    o_ref[...] = _norm_mod(x_ref[...], g_ref, sc_ref, sh_ref).astype(o_ref.dtype)


def _norm_router_kernel(x_ref, g_ref, sc_ref, sh_ref, whi_ref, wlo_ref, o_ref, logit_ref):
    y = _norm_mod(x_ref[...], g_ref, sc_ref, sh_ref)
    o_ref[...] = y.astype(o_ref.dtype)
    yhi, ylo = _split_bf16(y)
    whi = whi_ref[...]
    logit_ref[...] = _dot(yhi, whi) + _dot(ylo, whi) + _dot(yhi, wlo_ref[...])


def _norm_call(x, gain, scale, shift, seq, router_w=None):
    t, d = x.shape
    tm = min(512, seq)
    per_batch = seq // tm
    row = pl.BlockSpec((tm, d), lambda i: (i, 0))
    mod = pl.BlockSpec((1, 1, d), lambda i: (i // per_batch, 0, 0))
    in_specs = [row, pl.BlockSpec((1, d), lambda i: (0, 0)), mod, mod]
    args = [x, gain.reshape(1, d), scale, shift]
    if router_w is None:
        return pl.pallas_call(
            _norm_kernel, grid=(t // tm,), in_specs=in_specs, out_specs=row,
            out_shape=jax.ShapeDtypeStruct((t, d), bf16),
            compiler_params=_cparams("parallel"), name="norm_mod",
        )(*args)
    e = router_w.shape[1]
    whi, wlo = _split_bf16(jnp.zeros((d, LANES), f32).at[:, :e].set(router_w))
    wspec = pl.BlockSpec((d, LANES), lambda i: (0, 0))
    xn, logits = pl.pallas_call(
        _norm_router_kernel, grid=(t // tm,), in_specs=in_specs + [wspec, wspec],
        out_specs=[row, pl.BlockSpec((tm, LANES), lambda i: (i, 0))],
        out_shape=[jax.ShapeDtypeStruct((t, d), bf16), jax.ShapeDtypeStruct((t, LANES), f32)],
        compiler_params=_cparams("parallel"), name="norm_mod_router",
    )(*args, whi, wlo)
    return xn, logits[:, :e]


def _rope_lanes(x, cos_t, sin_t, half, first_mask):
    n = x.shape[-1]
    fwd = pltpu.roll(x, n - half, axis=1)
    bwd = pltpu.roll(x, half, axis=1)
    return x * cos_t + jnp.where(first_mask, fwd, bwd) * sin_t


def _rope_tables(positions):
    t = positions.size
    pos = positions.reshape(t, 1).astype(f32)

    def tables(dim):
        inv_freq = ROPE_THETA ** (-jnp.arange(0, dim, 2, dtype=f32) / dim)
        ang = pos * inv_freq
        return jnp.cos(ang), jnp.sin(ang)

    c, s = tables(HEAD_DIM)
    cos_h = jnp.concatenate([c, c, c, c], axis=-1)
    sin_h = jnp.concatenate([-s, s, -s, s], axis=-1)
    c, s = tables(MLA_ROPE)
    one = jnp.ones((t, MLA_NOPE), f32)
    zero = jnp.zeros((t, MLA_NOPE), f32)
    pad1 = jnp.ones((t, MLA_HEAD_PAD - MLA_QK), f32)
    pad0 = jnp.zeros((t, MLA_HEAD_PAD - MLA_QK), f32)
    cos_r = jnp.concatenate([one, c, c, pad1], axis=-1)
    sin_r = jnp.concatenate([zero, -s, s, pad0], axis=-1)
    return cos_h, sin_h, cos_r, sin_r


def _pair_rmsnorm(x, gain):
    lane = lax.broadcasted_iota(jnp.int32, x.shape, 1)
    lo = lane < HEAD_DIM
    x2 = x * x
    s_lo = jnp.sum(jnp.where(lo, x2, 0.0), axis=-1, keepdims=True)
    s_hi = jnp.sum(jnp.where(lo, 0.0, x2), axis=-1, keepdims=True)
    ms = jnp.where(lo, s_lo, s_hi) * (1.0 / HEAD_DIM)
    return x * lax.rsqrt(ms + NORM_EPS) * gain


def _swa_prep_kernel(h_ref, cos_ref, sin_ref, qg_ref, kg_ref, q_ref, k_ref, v_ref):
    cos_t = cos_ref[...]
    sin_t = sin_ref[...]
    lane = lax.broadcasted_iota(jnp.int32, cos_t.shape, 1)
    first = (lane % HEAD_DIM) < (HEAD_DIM // 2)
    scale = HEAD_DIM ** -0.5
    for p in range(SWA_Q // LANES):
        x = _pair_rmsnorm(h_ref[:, p * LANES:(p + 1) * LANES], qg_ref[...])
        x = _rope_lanes(x, cos_t, sin_t, HEAD_DIM // 2, first)
        q_ref[:, p * LANES:(p + 1) * LANES] = (x * scale).astype(bf16)
    x = _pair_rmsnorm(h_ref[:, SWA_Q:SWA_Q + SWA_KV], kg_ref[...])
    k_ref[...] = _rope_lanes(x, cos_t, sin_t, HEAD_DIM // 2, first).astype(bf16)
    v_ref[...] = h_ref[:, SWA_Q + SWA_KV:SWA_COLS].astype(bf16)


def _swa_attn_kernel(q_ref, k_ref, v_ref, sink_ref, o_ref, *, seq, nblk):
    step = pl.program_id(1)
    band = 3 * SWA_BLOCK
    grp = SWA_HEADS // SWA_KV_HEADS
    rows = grp * SWA_BLOCK
    row_in_blk = lax.broadcasted_iota(jnp.int32, (rows, band), 0) % SWA_BLOCK
    col = lax.broadcasted_iota(jnp.int32, (rows, band), 1)
    sinks = [jnp.concatenate(
        [jnp.broadcast_to(sink_ref[kh * grp + g:kh * grp + g + 1, 0:1], (SWA_BLOCK, 1)) for g in range(grp)],
        axis=0) for kh in range(SWA_KV_HEADS)]
    qs, ks, vs, valid, sink = [], [], [], [], []
    for i in range(nblk):
        n = step * nblk + i
        start = pl.multiple_of(jnp.clip((n - 1) * SWA_BLOCK, 0, seq - band), SWA_BLOCK)
        kb = k_ref[pl.ds(start, band), :]
        vb = v_ref[pl.ds(start, band), :]
        q = q_ref[i * SWA_BLOCK:(i + 1) * SWA_BLOCK, :]
        ok = jnp.abs(n * SWA_BLOCK + row_in_blk - (start + col)) <= SWA_WINDOW
        for kh in range(SWA_KV_HEADS):
            qs.append(jnp.concatenate(
                [q[:, (kh * grp + g) * HEAD_DIM:(kh * grp + g + 1) * HEAD_DIM] for g in range(grp)], axis=0))
            ks.append(kb[:, kh * HEAD_DIM:(kh + 1) * HEAD_DIM])
            vs.append(vb[:, kh * HEAD_DIM:(kh + 1) * HEAD_DIM])
            valid.append(ok)
            sink.append(sinks[kh])
    sink = jnp.stack(sink)
    s = jnp.where(jnp.stack(valid), _bmm_nt(jnp.stack(qs), jnp.stack(ks)), NEG_INF)
    m = jnp.maximum(jnp.max(s, axis=-1, keepdims=True), sink)
    p = jnp.exp(s - m)
    denom = jnp.sum(p, axis=-1, keepdims=True) + jnp.exp(sink - m)
    o = _bmm(p.astype(bf16), jnp.stack(vs)) / denom
    for i in range(nblk):
        o_ref[i * SWA_BLOCK:(i + 1) * SWA_BLOCK, :] = jnp.concatenate(
            [o[i * SWA_KV_HEADS + kh, g * SWA_BLOCK:(g + 1) * SWA_BLOCK]
             for kh in range(SWA_KV_HEADS) for g in range(grp)], axis=-1)


def _swa_mixer(h_swa, cos_h, sin_h, q_gain, k_gain, sink, batch, seq):
    t = h_swa.shape[0]
    tm = min(512, seq)
    row = lambda w: pl.BlockSpec((tm, w), lambda i: (i, 0))
    const = pl.BlockSpec((1, LANES), lambda i: (0, 0))
    q, k, v = pl.pallas_call(
        _swa_prep_kernel, grid=(t // tm,),
        in_specs=[row(SWA_COLS), row(LANES), row(LANES), const, const],
        out_specs=[row(SWA_Q), row(SWA_KV), row(SWA_KV)],
        out_shape=[jax.ShapeDtypeStruct((t, SWA_Q), bf16),
                   jax.ShapeDtypeStruct((t, SWA_KV), bf16),
                   jax.ShapeDtypeStruct((t, SWA_KV), bf16)],
        compiler_params=_cparams("parallel"), name="swa_prep",
    )(h_swa, cos_h, sin_h, jnp.tile(q_gain, 2).reshape(1, LANES), jnp.tile(k_gain, 2).reshape(1, LANES))
    nblk = SWA_BLOCKS_PER_STEP
    tq = nblk * SWA_BLOCK
    nb = seq // tq
    sink_t = jnp.broadcast_to(sink.astype(f32).reshape(SWA_HEADS, 1), (SWA_HEADS, LANES))
    return pl.pallas_call(
        functools.partial(_swa_attn_kernel, seq=seq, nblk=nblk), grid=(batch, nb),
        in_specs=[pl.BlockSpec((tq, SWA_Q), lambda b, n: (b * nb + n, 0)),
                  pl.BlockSpec((seq, SWA_KV), lambda b, n: (b, 0)),
                  pl.BlockSpec((seq, SWA_KV), lambda b, n: (b, 0)),
                  pl.BlockSpec((SWA_HEADS, LANES), lambda b, n: (0, 0))],
        out_specs=pl.BlockSpec((tq, SWA_Q), lambda b, n: (b * nb + n, 0)),
        out_shape=jax.ShapeDtypeStruct((t, SWA_Q), f32),
        compiler_params=_cparams("parallel", "parallel"), name="swa_attn",
    )(q, k, v, sink_t)


def _mla_prep_kernel(h_ref, cos_ref, sin_ref, qag_ref, kvag_ref, wq_ref, wk_ref, wpe_ref, wv_ref,
                     qg_ref, kg_ref, q_ref, k_ref, v_ref):
    cos_t = cos_ref[...]
    sin_t = sin_ref[...]
    lane = lax.broadcasted_iota(jnp.int32, cos_t.shape, 1)
    first = lane < MLA_NOPE + MLA_ROPE // 2
    cq = h_ref[:, :MLA_Q_RANK]
    cq = cq * lax.rsqrt(jnp.mean(cq * cq, axis=-1, keepdims=True) + NORM_EPS) * qag_ref[...]
    ckv = h_ref[:, MLA_Q_RANK:MLA_Q_RANK + MLA_KV_RANK]
    ckv = (ckv * lax.rsqrt(jnp.mean(ckv * ckv, axis=-1, keepdims=True) + NORM_EPS) * kvag_ref[...]).astype(bf16)
    kpe = h_ref[:, MLA_Q_RANK + MLA_KV_RANK:MLA_COLS_PAD].astype(bf16)
    q_all = _dot(cq.astype(bf16), wq_ref[...])
    k_all = _dot(ckv, wk_ref[...]) + _dot(kpe, wpe_ref[...])
    v_all = _dot(ckv, wv_ref[...])
    one_lane = lane == MLA_V
    scale = MLA_QK ** -0.5 * math.log2(math.e)

    def head_norm(x, gain):
        ms = jnp.sum(x * x, axis=-1, keepdims=True) * (1.0 / MLA_QK)
        x = x * lax.rsqrt(ms + NORM_EPS) * gain
        return _rope_lanes(x, cos_t, sin_t, MLA_ROPE // 2, first)

    for h in range(MLA_HEADS):
        sl = slice(h * MLA_HEAD_PAD, (h + 1) * MLA_HEAD_PAD)
        q_ref[0, h] = (head_norm(q_all[:, sl], qg_ref[...]) * scale).astype(bf16)
        k_ref[0, h] = head_norm(k_all[:, sl], kg_ref[...]).astype(bf16)
        v_ref[0, h] = jnp.where(one_lane, 1.0, v_all[:, sl]).astype(bf16)


def _mla_attn_kernel(q_ref, k_ref, v_ref, o_ref, *, tk, nk):
    n_heads = q_ref.shape[1]
    tq = q_ref.shape[2]
    qs = [q_ref[0, hh] for hh in range(n_heads)]

    def body(j, carry):
        off = pl.multiple_of(j * tk, tk)
        new = []
        for hh in range(n_heads):
            m, acc = carry[hh]
            kj = k_ref[0, hh, pl.ds(off, tk), :]
            vj = v_ref[0, hh, pl.ds(off, tk), :]
            s = _dot_nt(qs[hh], kj)
            m_new = jnp.maximum(m, jnp.max(s, axis=-1, keepdims=True))
            alpha = jnp.exp2(m - m_new)
            p = jnp.exp2(s - m_new)
            new.append((m_new, acc * alpha + _dot(p.astype(bf16), vj)))
        return tuple(new)

    init = tuple((jnp.full((tq, 1), NEG_INF, f32), jnp.zeros((tq, MLA_HEAD_PAD), f32))
                 for _ in range(n_heads))
    final = lax.fori_loop(0, nk, body, init)
    o_ref[0] = jnp.concatenate([acc[:, :MLA_V] / acc[:, MLA_V:MLA_V + 1] for _, acc in final], axis=-1)


def _mla_weights(w_uq, w_ukv, q_gain, k_gain):
    hp = MLA_HEAD_PAD
    wq = jnp.zeros((MLA_Q_RANK, MLA_HEADS, hp), f32).at[:, :, :MLA_QK].set(
        w_uq.reshape(MLA_Q_RANK, MLA_HEADS, MLA_QK))
    kv = w_ukv.reshape(MLA_KV_RANK, MLA_HEADS, MLA_NOPE + MLA_V)
    wk = jnp.zeros((MLA_KV_RANK, MLA_HEADS, hp), f32).at[:, :, :MLA_NOPE].set(kv[:, :, :MLA_NOPE])
    wv = jnp.zeros((MLA_KV_RANK, MLA_HEADS, hp), f32).at[:, :, :MLA_V].set(kv[:, :, MLA_NOPE:])
    eye = jnp.eye(MLA_ROPE, dtype=f32)
    wpe = jnp.zeros((LANES, MLA_HEADS, hp), f32).at[:MLA_ROPE, :, MLA_NOPE:MLA_QK].set(
        jnp.broadcast_to(eye[:, None, :], (MLA_ROPE, MLA_HEADS, MLA_ROPE)))
    flat = lambda w: w.reshape(w.shape[0], MLA_HEADS * hp).astype(bf16)
    pad = lambda g: jnp.zeros((1, hp), f32).at[0, :MLA_QK].set(g)
    return flat(wq), flat(wk), flat(wpe), flat(wv), pad(q_gain), pad(k_gain)


def _mla_mixer(h_mla, cos_r, sin_r, q_a_gain, kv_a_gain, w_uq, w_ukv, q_gain, k_gain, batch, seq):
    t = h_mla.shape[0]
    tm = min(512, seq)
    per_batch = seq // tm
    hp = MLA_HEAD_PAD
    wq, wk, wpe, wv, qg, kg = _mla_weights(w_uq, w_ukv, q_gain, k_gain)
    row = lambda w: pl.BlockSpec((tm, w), lambda b, i: (b * per_batch + i, 0))
    const = lambda a: pl.BlockSpec(a.shape, lambda b, i: (0,) * a.ndim)
    head_out = pl.BlockSpec((1, MLA_HEADS, tm, hp), lambda b, i: (b, 0, i, 0))
    head_shape = jax.ShapeDtypeStruct((batch, MLA_HEADS, seq, hp), bf16)
    qag = q_a_gain.reshape(1, MLA_Q_RANK)
    kvag = kv_a_gain.reshape(1, MLA_KV_RANK)
    q, k, v = pl.pallas_call(
        _mla_prep_kernel, grid=(batch, per_batch),
        in_specs=[row(MLA_COLS_PAD), row(LANES), row(LANES), const(qag), const(kvag),
                  const(wq), const(wk), const(wpe), const(wv), const(qg), const(kg)],
        out_specs=[head_out, head_out, head_out],
        out_shape=[head_shape, head_shape, head_shape],
        compiler_params=_cparams("parallel", "parallel"), name="mla_prep",
    )(h_mla, cos_r, sin_r, qag, kvag, wq, wk, wpe, wv, qg, kg)
    tq = min(1024, seq)
    tk = min(2048, seq)
    kv_spec = pl.BlockSpec((1, 2, seq, hp), lambda b, h, i: (b, h, 0, 0))
    o = pl.pallas_call(
        functools.partial(_mla_attn_kernel, tk=tk, nk=seq // tk),
        grid=(batch, MLA_HEADS // 2, seq // tq),
        in_specs=[pl.BlockSpec((1, 2, tq, hp), lambda b, h, i: (b, h, i, 0)), kv_spec, kv_spec],
        out_specs=pl.BlockSpec((1, tq, 2 * MLA_V), lambda b, h, i: (b, i, h)),
        out_shape=jax.ShapeDtypeStruct((batch, seq, MLA_HEADS * MLA_V), f32),
        compiler_params=_cparams("parallel", "parallel", "parallel"), name="mla_attn",
    )(q, k, v)
    return o.reshape(t, MLA_HEADS * MLA_V)


def _head_sum(x, bd):
    hi, lo = _split_bf16(x)
    return _dot(hi, bd) + _dot(lo, bd)


def _rwkv_prep_kernel(h_ref, prev_ref, next_ref, mu_ref, w0_ref, wup_ref, a0_ref, aup_ref, gup_ref,
                      kk_ref, ka_ref, rk_ref, bd_ref,
                      shared_ref, g_ref, bonus_ref, dir_ref, *, per_batch):
    i = pl.program_id(0)
    x = h_ref[...]
    tm = x.shape[0]
    row = lax.broadcasted_iota(jnp.int32, x.shape, 0)
    prev_row = jnp.where(i % per_batch == 0, 0.0, prev_ref[7:8, :])
    next_row = jnp.where(i % per_batch == per_batch - 1, 0.0, next_ref[0:1, :])
    x_prev = jnp.where(row == 0, prev_row, pltpu.roll(x, 1, axis=0))
    x_next = jnp.where(row == tm - 1, next_row, pltpu.roll(x, tm - 1, axis=0))
    hs = x + mu_ref[...] * (0.5 * (x_prev + x_next) - x)
    gw = GROUP_WIDTH
    r = hs[:, 0:gw]
    k = hs[:, gw:2 * gw]
    v = hs[:, 2 * gw:3 * gw]
    wd = hs[:, 3 * gw:3 * gw + LANES]
    ad = hs[:, 3 * gw + LANES:3 * gw + 2 * LANES]
    gd = hs[:, 3 * gw + 2 * LANES:RWKV_COLS]
    bd = bd_ref[...]
    lora_w = _dot(jnp.tanh(wd).astype(bf16), wup_ref[...])
    lora_a = _dot(ad.astype(bf16), aup_ref[...])
    kk = k * kk_ref[...]
    kk = kk * lax.rsqrt(_head_sum(kk * kk, bd) + 1e-12)
    shared_ref[:, 0:gw] = r
    shared_ref[:, gw:2 * gw] = v
    shared_ref[:, 2 * gw:3 * gw] = -kk
    g_ref[...] = _dot(_sigmoid(gd).astype(bf16), gup_ref[...])
    bonus_ref[...] = _head_sum(r * k * rk_ref[...], bd) * v
    for e in range(2):
        z = -(w0_ref[e:e + 1, :] + lora_w[:, e * gw:(e + 1) * gw])
        softplus = jnp.maximum(z, 0.0) + jnp.log(1.0 + jnp.exp(-jnp.abs(z)))
        dir_ref[e, :, 0:gw] = -jnp.exp(-softplus - 0.5)
        a = _sigmoid(a0_ref[e:e + 1, :] + lora_a[:, e * gw:(e + 1) * gw])
        dir_ref[e, :, gw:2 * gw] = k * (1.0 + (a - 1.0) * ka_ref[...])
        dir_ref[e, :, 2 * gw:3 * gw] = kk * a


def _bmm(a, b):
    return lax.dot_general(a, b, (((2,), (1,)), ((0,), (0,))), preferred_element_type=f32)


def _bmm_nt(a, b):
    return lax.dot_general(a, b, (((2,), (2,)), ((0,), (0,))), preferred_element_type=f32)


def _bmm_tn(a, b):
    return lax.dot_general(a, b, (((1,), (1,)), ((0,), (0,))), preferred_element_type=f32)


def _rwkv_chunk_kernel(shared_f, shared_b, dir_f, dir_b, o_f, o_b, h_scr, *, batch, ch):
    L = RWKV_CHUNK
    N = HEAD_DIM
    W = RWKV_GROUP_LANES
    hpg = W // N
    n_grp = GROUP_WIDTH // W

    @pl.when(pl.program_id(0) == 0)
    def _():
        h_scr[...] = jnp.zeros_like(h_scr)

    units = [(j, d, b, g) for j in range(ch) for d in range(2) for b in range(batch) for g in range(n_grp)]
    per_step = 2 * batch * n_grp
    n_units = len(units)

    def rows_of(j, d):
        return pl.ds((ch - 1 - j if d else j) * L, L)

    def load(ref_f, ref_b, lead, field):
        return jnp.stack([(ref_b if d else ref_f)[lead + (b, rows_of(j, d),
                                                          pl.ds(field * GROUP_WIDTH + g * W, W))]
                          for (j, d, b, g) in units])

    r = load(shared_f, shared_b, (), 0)
    v = load(shared_f, shared_b, (), 1)
    an = load(shared_f, shared_b, (), 2)
    lw = load(dir_f, dir_b, (0,), 0)
    kd = load(dir_f, dir_b, (0,), 1)
    bdir = load(dir_f, dir_b, (0,), 2)

    shape = (n_units, L, W)
    ui = lax.broadcasted_iota(jnp.int32, shape, 0)
    ti = lax.broadcasted_iota(jnp.int32, shape, 1)
    si = lax.broadcasted_iota(jnp.int32, shape, 2) % N
    sign = 1 - 2 * ((ui // (batch * n_grp)) % 2)
    ahead = (ti - si) * sign
    strict = ahead > 0
    incl = ahead >= 0
    bi = lax.broadcasted_iota(jnp.int32, (W, W), 0) // N
    bj = lax.broadcasted_iota(jnp.int32, (W, W), 1) // N
    diag_blocks = bi == bj

    def bdiag(x):
        return jnp.where(diag_blocks, jnp.concatenate([x] * hpg, axis=1), 0.0).astype(bf16)

    t2 = lax.broadcasted_iota(jnp.int32, (L, L), 0)
    s2 = lax.broadcasted_iota(jnp.int32, (L, L), 1)
    cum_parts = [None] * n_units
    for d in range(2):
        sel = [i for i, u in enumerate(units) if u[1] == d]
        tri = jnp.where((s2 >= t2) if d else (s2 <= t2), 1.0, 0.0).astype(bf16)
        hi, lo = _split_bf16(jnp.concatenate([lw[i] for i in sel], axis=1))
        cum_cat = _dot(tri, hi) + _dot(tri, lo)
        for k, i in enumerate(sel):
            cum_parts[i] = cum_cat[:, k * W:(k + 1) * W]
    cum = jnp.stack(cum_parts)
    p_incl = jnp.exp(cum)
    p_inv = jnp.exp(-cum)
    p_prev = jnp.exp(cum - lw)
    p_last = jnp.exp(jnp.sum(lw, axis=1, keepdims=True))
    a_t = an * p_prev
    b_t = bdir * p_inv
    k_t = kd * p_inv
    r_t = r * p_incl
    b_h = b_t * p_last
    k_h = k_t * p_last

    ar = jnp.concatenate([a_t, r_t], axis=1).astype(bf16)
    prod_b = _bmm_nt(ar, bdiag(b_t))
    prod_k = _bmm_nt(ar, bdiag(k_t))
    a_ab = jnp.where(strict, prod_b[:, :L], 0.0)
    a_rb = jnp.where(incl, prod_b[:, L:], 0.0).astype(bf16)
    a_ak = jnp.where(strict, prod_k[:, :L], 0.0).astype(bf16)
    a_rk = jnp.where(incl, prod_k[:, L:], 0.0).astype(bf16)

    m = 1
    t_inv = None
    while m < L:
        same = (ti // (2 * m)) == (si // (2 * m))
        later_half = ((ti // m) % 2 - (si // m) % 2) * sign == 1
        a_off = jnp.where(same, jnp.where(later_half, a_ab, 0.0), 0.0)
        if m == 1:
            t_inv = jnp.where(ti == si, 1.0, 0.0) + a_off
        else:
            left = _bmm(t_inv.astype(bf16), bdiag(a_off)).astype(bf16)
            t_inv = t_inv + _bmm(left, bdiag(t_inv))
        m *= 2

    av = _bmm(jnp.concatenate([a_ak, a_rk], axis=1), bdiag(v))
    akv = av[:, :L]
    tb = t_inv.astype(bf16)
    w_mat = _bmm(tb, bdiag(a_t))
    z_mat = _bmm(tb, bdiag(akv))
    q_mat = r_t + _bmm(a_rb, bdiag(w_mat))
    y_mat = _bmm(a_rb, bdiag(z_mat)) + av[:, L:]

    for j in range(ch):
        sl = slice(j * per_step, (j + 1) * per_step)
        h_t = h_scr[...]
        qw = jnp.concatenate([q_mat[sl], w_mat[sl]], axis=1).astype(bf16)
        ou = _bmm_nt(qw, h_t.astype(bf16))
        out = ou[:, :L] + y_mat[sl]
        u_mat = ou[:, L:] + z_mat[sl]
        keys = jnp.concatenate([b_h[sl], k_h[sl]], axis=1).astype(bf16)
        vals = jnp.concatenate([u_mat, v[sl]], axis=1).astype(bf16)
        upd = _bmm_tn(vals, keys)
        h_scr[...] = jnp.where(diag_blocks, p_last[sl] * h_t + upd, 0.0)
        for k, (_, d, b, g) in enumerate(units[sl]):
            (o_b if d else o_f)[b, rows_of(j, d), pl.ds(g * W, W)] = out[k]


def _rwkv_post_kernel(of_ref, ob_ref, bonus_ref, g_ref, lnw_ref, lnb_ref, bd_ref, out_ref):
    y = of_ref[...] + ob_ref[...]
    bd = bd_ref[...]
    mean = _head_sum(y, bd) * (1.0 / HEAD_DIM)
    yc = y - mean
    var = _head_sum(yc * yc, bd) * (1.0 / HEAD_DIM)
    yn = yc * lax.rsqrt(var + RWKV_GN_EPS) * lnw_ref[...] + lnb_ref[...]
    out_ref[...] = (yn + bonus_ref[...]) * g_ref[...]


def _rwkv_mixer(h_rwkv, mu, w0, w_up, a0, a_up, g_up, k_k, k_a, r_k, ln_w, ln_b, batch, seq):
    t = h_rwkv.shape[0]
    gw = GROUP_WIDTH
    tm = min(256, seq)
    per_batch = seq // tm
    n_halo = t // 8
    wup = jnp.zeros((2 * RWKV_LORA, 2 * gw), f32)
    aup = jnp.zeros((2 * RWKV_LORA, 2 * gw), f32)
    for e in range(2):
        wup = wup.at[e * RWKV_LORA:(e + 1) * RWKV_LORA, e * gw:(e + 1) * gw].set(w_up[e])
        aup = aup.at[e * RWKV_LORA:(e + 1) * RWKV_LORA, e * gw:(e + 1) * gw].set(a_up[e])
    head_id = jnp.arange(gw) // HEAD_DIM
    bd = (head_id[:, None] == head_id[None, :]).astype(bf16)
    vec = lambda a: a.reshape(1, -1).astype(f32)
    const = lambda a: pl.BlockSpec(a.shape, lambda i: (0,) * a.ndim)
    consts = [vec(mu), w0.astype(f32), wup.astype(bf16), a0.astype(f32), aup.astype(bf16),
              g_up.astype(bf16), vec(k_k), vec(k_a), vec(r_k), bd]
    row = pl.BlockSpec((tm, gw), lambda i: (i, 0))
    one = jax.ShapeDtypeStruct((t, gw), f32)
    n_field = 3
    shared, g, bonus, per_dir = pl.pallas_call(
        functools.partial(_rwkv_prep_kernel, per_batch=per_batch), grid=(t // tm,),
        in_specs=[pl.BlockSpec((tm, RWKV_COLS), lambda i: (i, 0)),
                  pl.BlockSpec((8, RWKV_COLS), lambda i: (jnp.maximum(i * (tm // 8) - 1, 0), 0)),
                  pl.BlockSpec((8, RWKV_COLS), lambda i: (jnp.minimum((i + 1) * (tm // 8), n_halo - 1), 0)),
                  ] + [const(a) for a in consts],
        out_specs=[pl.BlockSpec((tm, n_field * gw), lambda i: (i, 0)), row, row,
                   pl.BlockSpec((2, tm, n_field * gw), lambda i: (0, i, 0))],
        out_shape=[jax.ShapeDtypeStruct((t, n_field * gw), f32), one, one,
                   jax.ShapeDtypeStruct((2, t, n_field * gw), f32)],
        compiler_params=_cparams("parallel"), name="rwkv_prep",
    )(h_rwkv, h_rwkv, h_rwkv, *consts)

    L = RWKV_CHUNK
    ch = RWKV_CHUNKS_PER_STEP
    rows = ch * L
    n_steps = seq // rows
    n_grp = gw // RWKV_GROUP_LANES
    shared3 = shared.reshape(batch, seq, n_field * gw)
    per_dir4 = per_dir.reshape(2, batch, seq, n_field * gw)
    fwd = lambda w: pl.BlockSpec((batch, rows, w), lambda c: (0, c, 0))
    bwd = lambda w: pl.BlockSpec((batch, rows, w), lambda c: (0, n_steps - 1 - c, 0))
    fwd_dir = pl.BlockSpec((1, batch, rows, n_field * gw), lambda c: (0, 0, c, 0))
    bwd_dir = pl.BlockSpec((1, batch, rows, n_field * gw), lambda c: (1, 0, n_steps - 1 - c, 0))
    o3 = jax.ShapeDtypeStruct((batch, seq, gw), f32)
    o_fwd, o_bwd = pl.pallas_call(
        functools.partial(_rwkv_chunk_kernel, batch=batch, ch=ch), grid=(n_steps,),
        in_specs=[fwd(n_field * gw), bwd(n_field * gw), fwd_dir, bwd_dir],
        out_specs=[fwd(gw), bwd(gw)], out_shape=[o3, o3],
        scratch_shapes=[pltpu.VMEM((2 * batch * n_grp, RWKV_GROUP_LANES, RWKV_GROUP_LANES), f32)],
        compiler_params=_cparams("arbitrary"), name="rwkv_chunk",
    )(shared3, shared3, per_dir4, per_dir4)

    tp = min(512, seq)
    rowp = pl.BlockSpec((tp, gw), lambda i: (i, 0))
    cvec = pl.BlockSpec((1, gw), lambda i: (0, 0))
    return pl.pallas_call(
        _rwkv_post_kernel, grid=(t // tp,),
        in_specs=[rowp, rowp, rowp, rowp, cvec, cvec, pl.BlockSpec((gw, gw), lambda i: (0, 0))],
        out_specs=rowp, out_shape=one,
        compiler_params=_cparams("parallel"), name="rwkv_post",
    )(o_fwd.reshape(t, gw), o_bwd.reshape(t, gw), bonus, g, vec(ln_w), vec(ln_b), bd)


def _s5_operators(a_re, a_im, log_dt, b_re, b_im, c_re, c_im):
    L = S5_CHUNK
    lam_r, lam_i = a_re.astype(f32), a_im.astype(f32)
    dt = jnp.exp(log_dt.astype(f32))[..., None]
    mag = jnp.exp(lam_r * dt)
    abar_r, abar_i = mag * jnp.cos(lam_i * dt), mag * jnp.sin(lam_i * dt)
    den = lam_r * lam_r + lam_i * lam_i
    nr, ni = abar_r - 1.0, abar_i
    coef_r = (nr * lam_r + ni * lam_i) / den
    coef_i = (ni * lam_r - nr * lam_i) / den
    br, bi = b_re.astype(f32)[None], b_im.astype(f32)[None]
    bb_r = coef_r[..., None] * br - coef_i[..., None] * bi
    bb_i = coef_r[..., None] * bi + coef_i[..., None] * br
    steps = jnp.arange(L + 1, dtype=f32)[:, None, None, None]
    pmag = jnp.exp(steps * (lam_r * dt)[None])
    pw_r = pmag * jnp.cos(steps * (lam_i * dt)[None])
    pw_i = pmag * jnp.sin(steps * (lam_i * dt)[None])
    cr, ci = c_re.astype(f32)[None, None], c_im.astype(f32)[None, None]
    cp_r = cr * pw_r[:, :, :, None, :] - ci * pw_i[:, :, :, None, :]
    cp_i = cr * pw_i[:, :, :, None, :] + ci * pw_r[:, :, :, None, :]
    kern = (jnp.einsum('kegop,egpi->kegoi', cp_r, bb_r) - jnp.einsum('kegop,egpi->kegoi', cp_i, bb_i))
    kf, kb = kern[:L, 0], kern[:L, 1]
    by_lag = jnp.concatenate([kb[1:][::-1], kf[:1] + kb[:1], kf[1:], jnp.zeros_like(kf[:1])], axis=0)
    lag_tab = jnp.transpose(by_lag, (1, 3, 0, 2)).reshape(S5_GROUPS, S5_GROUP, 2 * L * S5_GROUP)
    g_, h_ = S5_GROUPS, S5_GROUP

    def state_in(pr, pi, e):
        re = pr[..., None] * bb_r[e][None] - pi[..., None] * bb_i[e][None]
        im = pr[..., None] * bb_i[e][None] + pi[..., None] * bb_r[e][None]
        return jnp.transpose(re, (1, 0, 3, 2)), jnp.transpose(im, (1, 0, 3, 2))

    f_re, f_im = state_in(pw_r[:L, 0][::-1], pw_i[:L, 0][::-1], 0)
    b_re_, b_im_ = state_in(pw_r[:L, 1], pw_i[:L, 1], 1)
    b_sum = jnp.concatenate([f_re, f_im, b_re_, b_im_], axis=-1).reshape(g_, L * h_, 4 * S5_STATE)

    def state_out(cpr, cpi):
        return jnp.transpose(cpr, (1, 3, 0, 2)), -jnp.transpose(cpi, (1, 3, 0, 2))

    fo_r, fo_i = state_out(cp_r[1:L + 1, 0], cp_i[1:L + 1, 0])
    bo_r, bo_i = state_out(cp_r[1:L + 1, 1][::-1], cp_i[1:L + 1, 1][::-1])
    c_out = jnp.concatenate([fo_r, fo_i, bo_r, bo_i], axis=1).reshape(g_, 4 * S5_STATE, L * h_)
    a_pow = jnp.stack([jnp.stack([pw_r[L, e], pw_i[L, e]]) for e in range(2)])
    return lag_tab, b_sum.astype(bf16), c_out.astype(bf16), a_pow


def _s5_state_in_kernel(u_ref, bsum_ref, s_ref):
    s_ref[0] = _dot(u_ref[0], bsum_ref[0])


def _s5_carry_kernel(sf_r, sf_i, sb_r, sb_i, af_r, af_i, ab_r, ab_i, xf_r, xf_i, xb_r, xb_i, *, nch):
    rows, p = af_r.shape
    zero = jnp.zeros((rows, p), f32)

    def body(i, carry):
        fr, fi, br, bi = carry
        j = nch - 1 - i
        xf_r[i] = fr
        xf_i[i] = fi
        xb_r[j] = br
        xb_i[j] = bi
        ar, ai = af_r[...], af_i[...]
        nfr = ar * fr - ai * fi + sf_r[i]
        nfi = ar * fi + ai * fr + sf_i[i]
        ar, ai = ab_r[...], ab_i[...]
        nbr = ar * br - ai * bi + sb_r[j]
        nbi = ar * bi + ai * br + sb_i[j]
        return nfr, nfi, nbr, nbi

    lax.fori_loop(0, nch, body, (zero, zero, zero, zero))


def _s5_out_kernel(u_ref, x_ref, lag_ref, cout_ref, y_ref):
    L, h_ = S5_CHUNK, S5_GROUP
    tab = lag_ref[0]
    width = tab.shape[1]
    blocks = []
    for s in range(L):
        start = (L - 1 - s) * h_
        win = pltpu.roll(tab, width - start, axis=1) if start else tab
        blocks.append(win[:, :L * h_])
    m_op = jnp.concatenate(blocks, axis=0).astype(bf16)
    y_ref[0] = _dot(u_ref[0], m_op) + _dot(x_ref[0], cout_ref[0])


def _s5_post_kernel(y_ref, u_ref, d_ref, w_ref, b_ref, o_ref):
    u = u_ref[...]
    z = y_ref[...] + d_ref[...] * u
    c0 = math.sqrt(2.0 / math.pi)
    z = 0.5 * z * (1.0 + jnp.tanh(c0 * (z + 0.044715 * (z * z * z))))
    gate = _dot(z.astype(bf16), w_ref[...]) + b_ref[...]
    o_ref[...] = z * _sigmoid(gate)


def _s5_mixer(u, a_re, a_im, log_dt, b_re, b_im, c_re, c_im, d_skip, glu_w, glu_b, batch, seq):
    t = u.shape[0]
    L, g_, h_, p = S5_CHUNK, S5_GROUPS, S5_GROUP, S5_STATE
    nch = seq // L
    rows = batch * nch
    lag_tab, b_sum, c_out, a_pow = _s5_operators(a_re, a_im, log_dt, b_re, b_im, c_re, c_im)
    ug = jnp.transpose(u.reshape(rows, L, g_, h_), (2, 0, 1, 3)).reshape(g_, rows, L * h_).astype(bf16)
    grp = lambda r, c: pl.BlockSpec((1, r, c), lambda g: (g, 0, 0))
    s_in = pl.pallas_call(
        _s5_state_in_kernel, grid=(g_,),
        in_specs=[grp(rows, L * h_), grp(L * h_, 4 * p)], out_specs=grp(rows, 4 * p),
        out_shape=jax.ShapeDtypeStruct((g_, rows, 4 * p), f32),
        compiler_params=_cparams("parallel"), name="s5_state_in",
    )(ug, b_sum)
    s_in = jnp.transpose(s_in.reshape(g_, batch, nch, 4, p), (3, 2, 1, 0, 4)).reshape(4, nch, batch * g_, p)
    coef = jnp.broadcast_to(a_pow[:, :, None], (2, 2, batch, g_, p)).reshape(2, 2, batch * g_, p)
    st = jax.ShapeDtypeStruct((nch, batch * g_, p), f32)
    rb = 8
    seq_spec = pl.BlockSpec((nch, rb, p), lambda i: (0, i, 0))
    coef_spec = pl.BlockSpec((rb, p), lambda i: (i, 0))
    xs = pl.pallas_call(
        functools.partial(_s5_carry_kernel, nch=nch), grid=(batch * g_ // rb,),
        in_specs=[seq_spec] * 4 + [coef_spec] * 4, out_specs=[seq_spec] * 4, out_shape=[st] * 4,
        compiler_params=_cparams("parallel"), name="s5_carry",
    )(s_in[0], s_in[1], s_in[2], s_in[3], coef[0, 0], coef[0, 1], coef[1, 0], coef[1, 1])
    x_cat = jnp.stack(xs).reshape(4, nch, batch, g_, p)
    x_cat = jnp.transpose(x_cat, (3, 2, 1, 0, 4)).reshape(g_, rows, 4 * p).astype(bf16)
    y = pl.pallas_call(
        _s5_out_kernel, grid=(g_,),
        in_specs=[grp(rows, L * h_), grp(rows, 4 * p), grp(h_, 2 * L * h_), grp(4 * p, L * h_)],
        out_specs=grp(rows, L * h_),
        out_shape=jax.ShapeDtypeStruct((g_, rows, L * h_), f32),
        compiler_params=_cparams("parallel"), name="s5_out",
    )(ug, x_cat, lag_tab, c_out)
    y = jnp.transpose(y.reshape(g_, rows, L, h_), (1, 2, 0, 3)).reshape(t, g_ * h_)
    tm = min(512, seq)
    w = g_ * h_
    row = pl.BlockSpec((tm, w), lambda i: (i, 0))
    cvec = pl.BlockSpec((1, w), lambda i: (0, 0))
    return pl.pallas_call(
        _s5_post_kernel, grid=(t // tm,),
        in_specs=[row, row, cvec, pl.BlockSpec((w, w), lambda i: (0, 0)), cvec],
        out_specs=row, out_shape=jax.ShapeDtypeStruct((t, w), f32),
        compiler_params=_cparams("parallel"), name="s5_post",
    )(y, u, d_skip.reshape(1, w).astype(f32), glu_w.astype(bf16), glu_b.reshape(1, w).astype(f32))


def _outproj_kernel(o0_ref, o1_ref, o2_ref, o3_ref, w_ref, x_ref, g_ref, out_ref):
    acc = None
    for i, o_ref in enumerate((o0_ref, o1_ref, o2_ref, o3_ref)):
        part = _dot(o_ref[...].astype(bf16), w_ref[i * GROUP_WIDTH:(i + 1) * GROUP_WIDTH, :])
        acc = part if acc is None else acc + part
    out_ref[...] = x_ref[...] + g_ref[0] * acc


def _outproj(outs, w_out, x, gate, seq):
    t, d = x.shape
    tm = min(512, seq)
    per_batch = seq // tm
    part = pl.BlockSpec((tm, GROUP_WIDTH), lambda i: (i, 0))
    row = pl.BlockSpec((tm, d), lambda i: (i, 0))
    return pl.pallas_call(
        _outproj_kernel, grid=(t // tm,),
        in_specs=[part, part, part, part, pl.BlockSpec((4 * GROUP_WIDTH, d), lambda i: (0, 0)), row,
                  pl.BlockSpec((1, 1, d), lambda i: (i // per_batch, 0, 0))],
        out_specs=row, out_shape=jax.ShapeDtypeStruct((t, d), f32),
        compiler_params=_cparams("parallel"), name="out_proj",
    )(*outs, w_out.astype(bf16), x, gate)


def _moe_up_kernel(xs_ref, w1_ref, w3_ref, hid_ref):
    b = pl.program_id(2)
    xs = xs_ref[b, 0]
    up = _dot(xs, w1_ref[0, 0].astype(bf16))
    lin = _dot(xs, w3_ref[0, 0].astype(bf16))
    hid_ref[0, 0] = (up * _sigmoid(up) * lin).astype(bf16)


def _moe_up(xs, w1, w3, layer):
    batch, n_exp, cap, d = xs.shape
    ff = w1.shape[3]
    tf = 512
    return pl.pallas_call(
        _moe_up_kernel, grid=(n_exp, ff // tf, batch),
        in_specs=[pl.BlockSpec((batch, 1, cap, d), lambda e, s, b: (0, e, 0, 0)),
                  pl.BlockSpec((1, 1, d, tf), lambda e, s, b: (layer, e, 0, s)),
                  pl.BlockSpec((1, 1, d, tf), lambda e, s, b: (layer, e, 0, s))],
        out_specs=pl.BlockSpec((1, 1, cap, tf), lambda e, s, b: (b, e, 0, s)),
        out_shape=jax.ShapeDtypeStruct((batch, n_exp, cap, ff), bf16),
        compiler_params=_cparams("parallel", "parallel", "arbitrary"), name="moe_up",
    )(xs, w1, w3)


def _moe_down_kernel(idx_ref, hid_ref, gate_ref, g2_ref, w2_ref, x_hbm, out_hbm, slab, ys_scr, sem,
                     *, width, group):
    b, h, e = pl.program_id(0), pl.program_id(1), pl.program_id(2)
    cols = pl.ds(pl.multiple_of(h * width, width), width)

    def slab_copy(src, dst):
        return pltpu.make_async_copy(src, dst, sem)

    @pl.when(e == 0)
    def _():
        cp = slab_copy(x_hbm.at[b, :, cols], slab)
        cp.start()
        cp.wait()

    ys_scr[...] = _dot(hid_ref[0, 0], w2_ref[0, 0].astype(bf16)) * gate_ref[0, 0] * g2_ref[0]
    cap = ys_scr.shape[0]

    def body(g, carry):
        r0 = pl.multiple_of(g * group, group)
        tok = [idx_ref[0, 0, r0 + k] for k in range(group)]
        rows = [slab[pl.ds(tok[k], 1), :] + ys_scr[pl.ds(r0 + k, 1), :] for k in range(group)]
        for k in range(group):
            slab[pl.ds(tok[k], 1), :] = rows[k]
        return carry

    lax.fori_loop(0, cap // group, body, 0)

    @pl.when(e == pl.num_programs(2) - 1)
    def _():
        cp = slab_copy(slab, out_hbm.at[b, :, cols])
        cp.start()
        cp.wait()


def _moe_down_combine(x, hid, gate, g2, w2, idx, layer):
    batch, seq, d = x.shape
    n_exp, cap = idx.shape[1], idx.shape[2]
    ff = hid.shape[3]
    width = d // 2
    return pl.pallas_call(
        functools.partial(_moe_down_kernel, width=width, group=8),
        grid=(batch, d // width, n_exp),
        in_specs=[pl.BlockSpec((1, 1, cap), lambda b, h, e: (b * n_exp + e, 0, 0), memory_space=pltpu.SMEM),
                  pl.BlockSpec((1, 1, cap, ff), lambda b, h, e: (b, e, 0, 0)),
                  pl.BlockSpec((1, 1, cap, 1), lambda b, h, e: (b, e, 0, 0)),
                  pl.BlockSpec((1, 1, width), lambda b, h, e: (b, 0, h)),
                  pl.BlockSpec((1, 1, ff, width), lambda b, h, e: (layer, e, 0, h)),
                  pl.BlockSpec(memory_space=pl.ANY)],
        out_specs=pl.BlockSpec(memory_space=pl.ANY),
        out_shape=jax.ShapeDtypeStruct((batch, seq, d), f32),
        scratch_shapes=[pltpu.VMEM((seq, width), f32), pltpu.VMEM((cap, width), f32),
                        pltpu.SemaphoreType.DMA(())],
        compiler_params=pltpu.CompilerParams(dimension_semantics=("arbitrary",) * 3,
                                             vmem_limit_bytes=MOE_DOWN_VMEM_LIMIT_BYTES),
        name="moe_down_combine",
    )(idx.reshape(batch * n_exp, 1, cap), hid, gate, g2, w2, x)


def _moe_residual(x, xn, logits, g2, w1, w3, w2, layer, batch, seq):
    d = xn.shape[1]
    capacity = EC_CAPACITY_FACTOR * seq // N_EXPERTS
    aff = jax.nn.softmax(logits.reshape(batch, seq, N_EXPERTS), axis=-1)
    gate, idx = lax.top_k(jnp.swapaxes(aff, 1, 2), capacity)
    bidx = jnp.arange(batch)[:, None, None]
    xs = xn.reshape(batch, seq, d)[bidx, idx]
    hid = _moe_up(xs, w1, w3, layer)
    out = _moe_down_combine(x.reshape(batch, seq, d), hid, gate[..., None], g2, w2, idx, layer)
    return out.reshape(batch * seq, d)


def _in_proj_weights(w_in_l):
    o1, o2, o3 = SWA_COLS, SWA_COLS + MLA_COLS, SWA_COLS + MLA_COLS + RWKV_COLS
    w_mla = jnp.zeros((w_in_l.shape[0], MLA_COLS_PAD), f32).at[:, :MLA_COLS].set(w_in_l[:, o1:o2])
    return (w_in_l[:, :o1].astype(bf16), w_mla.astype(bf16),
            w_in_l[:, o2:o3].astype(bf16), w_in_l[:, o3:].astype(bf16))


def kernel(x, c, positions, ada_w, ada_b, norm1_g, norm2_g, w_in, w_out, swa_q_gain, swa_k_gain, swa_sink, mla_q_a_gain, mla_kv_a_gain, mla_w_uq, mla_w_ukv, mla_q_gain, mla_k_gain, rwkv_mu, rwkv_w0, rwkv_w_up, rwkv_a0, rwkv_a_up, rwkv_g_up, rwkv_k_k, rwkv_k_a, rwkv_r_k, rwkv_ln_w, rwkv_ln_b, s5_a_re, s5_a_im, s5_log_dt, s5_b_re, s5_b_im, s5_c_re, s5_c_im, s5_d, s5_glu_w, s5_glu_b, router_w, moe_w1, moe_w3, moe_w2):
    batch, seq, d = x.shape
    depth = ada_w.shape[0]
    t = batch * seq
    cos_h, sin_h, cos_r, sin_r = _rope_tables(positions)
    mod = _ada_mod(c, ada_w, ada_b)
    xf = x.reshape(t, d)
    for l in range(depth):
        sh1, sc1, g1, sh2, sc2, g2 = [m[:, None, :] for m in jnp.split(mod[l], 6, axis=-1)]
        xn = _norm_call(xf, norm1_g[l], sc1, sh1, seq)
        w_swa, w_mla, w_rwkv, w_s5 = _in_proj_weights(w_in[l])
        h_swa = _matmul(xn, w_swa, tm=1024, tn=SWA_COLS, name="in_proj_swa")
        h_mla = _matmul(xn, w_mla, tm=1024, tn=MLA_COLS_PAD, name="in_proj_mla")
        h_rwkv = _matmul(xn, w_rwkv, tm=1024, tn=640, name="in_proj_rwkv")
        h_s5 = _matmul(xn, w_s5, tm=1024, tn=GROUP_WIDTH, name="in_proj_s5")
        o_swa = _swa_mixer(h_swa, cos_h, sin_h, swa_q_gain[l], swa_k_gain[l], swa_sink[l], batch, seq)
        o_mla = _mla_mixer(h_mla, cos_r, sin_r, mla_q_a_gain[l], mla_kv_a_gain[l], mla_w_uq[l],
                           mla_w_ukv[l], mla_q_gain[l], mla_k_gain[l], batch, seq)
        o_rwkv = _rwkv_mixer(h_rwkv, rwkv_mu[l], rwkv_w0[l], rwkv_w_up[l], rwkv_a0[l], rwkv_a_up[l],
                             rwkv_g_up[l], rwkv_k_k[l], rwkv_k_a[l], rwkv_r_k[l],
                             rwkv_ln_w[l], rwkv_ln_b[l], batch, seq)
        o_s5 = _s5_mixer(h_s5, s5_a_re[l], s5_a_im[l], s5_log_dt[l], s5_b_re[l], s5_b_im[l],
                         s5_c_re[l], s5_c_im[l], s5_d[l], s5_glu_w[l], s5_glu_b[l], batch, seq)
        xf = _outproj((o_swa, o_mla, o_rwkv, o_s5), w_out[l], xf, g1, seq)
        xn2, logits = _norm_call(xf, norm2_g[l], sc2, sh2, seq, router_w=router_w[l])
        xf = _moe_residual(xf, xn2, logits, g2, moe_w1, moe_w3, moe_w2, l, batch, seq)
    return xf.reshape(batch, seq, d)
```

```python
import functools
import math

import jax
import jax.numpy as jnp
from jax import lax
from jax.experimental import pallas as pl
from jax.experimental.pallas import tpu as pltpu

f32 = jnp.float32
bf16 = jnp.bfloat16

D_MODEL = 2048
HEAD_DIM = 64
GROUP_WIDTH = 512
ROPE_THETA = 10000.0
NORM_EPS = 1e-6
NEG_INF = -1e30

SWA_HEADS = 8
SWA_KV_HEADS = 2
SWA_WINDOW = 128
SWA_BLOCK = 128
SWA_Q = 512
SWA_KV = 128
SWA_COLS = 768
SWA_BLOCKS_PER_STEP = 8

MLA_HEADS = 8
MLA_NOPE = 64
MLA_ROPE = 32
MLA_V = 64
MLA_QK = 96
MLA_Q_RANK = 384
MLA_KV_RANK = 128
MLA_COLS = 544
MLA_COLS_PAD = 640
MLA_HEAD_PAD = 128

RWKV_HEADS = 8
RWKV_LORA = 64
RWKV_GATE_LORA = 128
RWKV_GN_EPS = 64e-5
RWKV_COLS = 1920
RWKV_CHUNK = 64
RWKV_CHUNKS_PER_STEP = 4
RWKV_GROUP_LANES = 256

S5_GROUP = 16
S5_GROUPS = 32
S5_STATE = 64
S5_CHUNK = 32

N_EXPERTS = 16
EC_CAPACITY_FACTOR = 2
D_FF_EXPERT = 1024

VMEM_LIMIT_BYTES = 52 * 1024 * 1024
MOE_DOWN_VMEM_LIMIT_BYTES = 57 * 1024 * 1024
LANES = 128


def _cparams(*sem):
    return pltpu.CompilerParams(dimension_semantics=sem, vmem_limit_bytes=VMEM_LIMIT_BYTES)


def _sigmoid(x):
    return 1.0 / (1.0 + jnp.exp(-x))


def _split_bf16(x):
    hi = x.astype(bf16)
    lo = (x - hi.astype(f32)).astype(bf16)
    return hi, lo


def _dot(a, b):
    return jnp.dot(a, b, preferred_element_type=f32)


def _dot_nt(a, b):
    return lax.dot_general(a, b, (((1,), (1,)), ((), ())), preferred_element_type=f32)


def _dot_tn(a, b):
    return lax.dot_general(a, b, (((0,), (0,)), ((), ())), preferred_element_type=f32)


def _ada_kernel(c_ref, w_ref, b_ref, o_ref):
    c = c_ref[...]
    cond = (c * _sigmoid(c)).astype(bf16)
    o_ref[0] = _dot(cond, w_ref[0].astype(bf16)) + b_ref[0]


def _ada_mod(c, ada_w, ada_b):
    depth, d, n = ada_w.shape
    b = c.shape[0]
    rows = 8
    c_pad = jnp.zeros((rows, d), f32).at[:b].set(c)
    tn = 512
    out = pl.pallas_call(
        _ada_kernel,
        grid=(depth, n // tn),
        in_specs=[
            pl.BlockSpec((rows, d), lambda l, j: (0, 0)),
            pl.BlockSpec((1, d, tn), lambda l, j: (l, 0, j)),
            pl.BlockSpec((1, 1, tn), lambda l, j: (l, 0, j)),
        ],
        out_specs=pl.BlockSpec((1, rows, tn), lambda l, j: (l, 0, j)),
        out_shape=jax.ShapeDtypeStruct((depth, rows, n), f32),
        compiler_params=_cparams("parallel", "parallel"),
        name="ada_mod",
    )(c_pad, ada_w, ada_b.reshape(depth, 1, n))
    return out[:, :b]


def _mm_kernel(x_ref, w_ref, o_ref):
    o_ref[...] = _dot(x_ref[...].astype(bf16), w_ref[...].astype(bf16)).astype(o_ref.dtype)


def _matmul(x, w, *, tm, tn, name):
    m, k = x.shape
    n = w.shape[1]
    tm = min(tm, m)
    return pl.pallas_call(
        _mm_kernel,
        grid=(m // tm, n // tn),
        in_specs=[pl.BlockSpec((tm, k), lambda i, j: (i, 0)),
                  pl.BlockSpec((k, tn), lambda i, j: (0, j))],
        out_specs=pl.BlockSpec((tm, tn), lambda i, j: (i, j)),
        out_shape=jax.ShapeDtypeStruct((m, n), f32),
        compiler_params=_cparams("parallel", "parallel"),
        name=name,
    )(x, w)


def _norm_mod(x, g_ref, sc_ref, sh_ref):
    ms = jnp.mean(x * x, axis=-1, keepdims=True)
    y = x * lax.rsqrt(ms + NORM_EPS) * g_ref[...]
    return y * (1.0 + sc_ref[0]) + sh_ref[0]


def _norm_kernel(x_ref, g_ref, sc_ref, sh_ref, o_ref):
    o_ref[...] = _norm_mod(x_ref[...], g_ref, sc_ref, sh_ref).astype(o_ref.dtype)


def _norm_router_kernel(x_ref, g_ref, sc_ref, sh_ref, whi_ref, wlo_ref, o_ref, logit_ref):
    y = _norm_mod(x_ref[...], g_ref, sc_ref, sh_ref)
    o_ref[...] = y.astype(o_ref.dtype)
    yhi, ylo = _split_bf16(y)
    whi = whi_ref[...]
    logit_ref[...] = _dot(yhi, whi) + _dot(ylo, whi) + _dot(yhi, wlo_ref[...])


def _norm_call(x, gain, scale, shift, seq, router_w=None):
    t, d = x.shape
    tm = min(512, seq)
    per_batch = seq // tm
    row = pl.BlockSpec((tm, d), lambda i: (i, 0))
    mod = pl.BlockSpec((1, 1, d), lambda i: (i // per_batch, 0, 0))
    in_specs = [row, pl.BlockSpec((1, d), lambda i: (0, 0)), mod, mod]
    args = [x, gain.reshape(1, d), scale, shift]
    if router_w is None:
        return pl.pallas_call(
            _norm_kernel, grid=(t // tm,), in_specs=in_specs, out_specs=row,
            out_shape=jax.ShapeDtypeStruct((t, d), bf16),
            compiler_params=_cparams("parallel"), name="norm_mod",
        )(*args)
    e = router_w.shape[1]
    whi, wlo = _split_bf16(jnp.zeros((d, LANES), f32).at[:, :e].set(router_w))
    wspec = pl.BlockSpec((d, LANES), lambda i: (0, 0))
    xn, logits = pl.pallas_call(
        _norm_router_kernel, grid=(t // tm,), in_specs=in_specs + [wspec, wspec],
        out_specs=[row, pl.BlockSpec((tm, LANES), lambda i: (i, 0))],
        out_shape=[jax.ShapeDtypeStruct((t, d), bf16), jax.ShapeDtypeStruct((t, LANES), f32)],
        compiler_params=_cparams("parallel"), name="norm_mod_router",
    )(*args, whi, wlo)
    return xn, logits[:, :e]


def _rope_lanes(x, cos_t, sin_t, half, first_mask):
    n = x.shape[-1]
    fwd = pltpu.roll(x, n - half, axis=1)
    bwd = pltpu.roll(x, half, axis=1)
    return x * cos_t + jnp.where(first_mask, fwd, bwd) * sin_t


def _rope_tables(positions):
    t = positions.size
    pos = positions.reshape(t, 1).astype(f32)

    def tables(dim):
        inv_freq = ROPE_THETA ** (-jnp.arange(0, dim, 2, dtype=f32) / dim)
        ang = pos * inv_freq
        return jnp.cos(ang), jnp.sin(ang)

    c, s = tables(HEAD_DIM)
    cos_h = jnp.concatenate([c, c, c, c], axis=-1)
    sin_h = jnp.concatenate([-s, s, -s, s], axis=-1)
    c, s = tables(MLA_ROPE)
    one = jnp.ones((t, MLA_NOPE), f32)
    zero = jnp.zeros((t, MLA_NOPE), f32)
    pad1 = jnp.ones((t, MLA_HEAD_PAD - MLA_QK), f32)
    pad0 = jnp.zeros((t, MLA_HEAD_PAD - MLA_QK), f32)
    cos_r = jnp.concatenate([one, c, c, pad1], axis=-1)
    sin_r = jnp.concatenate([zero, -s, s, pad0], axis=-1)
    return cos_h, sin_h, cos_r, sin_r


def _pair_rmsnorm(x, gain):
    lane = lax.broadcasted_iota(jnp.int32, x.shape, 1)
    lo = lane < HEAD_DIM
    x2 = x * x
    s_lo = jnp.sum(jnp.where(lo, x2, 0.0), axis=-1, keepdims=True)
    s_hi = jnp.sum(jnp.where(lo, 0.0, x2), axis=-1, keepdims=True)
    ms = jnp.where(lo, s_lo, s_hi) * (1.0 / HEAD_DIM)
    return x * lax.rsqrt(ms + NORM_EPS) * gain


def _swa_prep_kernel(h_ref, cos_ref, sin_ref, qg_ref, kg_ref, q_ref, k_ref, v_ref):
    cos_t = cos_ref[...]
    sin_t = sin_ref[...]
    lane = lax.broadcasted_iota(jnp.int32, cos_t.shape, 1)
    first = (lane % HEAD_DIM) < (HEAD_DIM // 2)
    scale = HEAD_DIM ** -0.5
    for p in range(SWA_Q // LANES):
        x = _pair_rmsnorm(h_ref[:, p * LANES:(p + 1) * LANES], qg_ref[...])
        x = _rope_lanes(x, cos_t, sin_t, HEAD_DIM // 2, first)
        q_ref[:, p * LANES:(p + 1) * LANES] = (x * scale).astype(bf16)
    x = _pair_rmsnorm(h_ref[:, SWA_Q:SWA_Q + SWA_KV], kg_ref[...])
    k_ref[...] = _rope_lanes(x, cos_t, sin_t, HEAD_DIM // 2, first).astype(bf16)
    v_ref[...] = h_ref[:, SWA_Q + SWA_KV:SWA_COLS].astype(bf16)


def _swa_attn_kernel(q_ref, k_ref, v_ref, sink_ref, o_ref, *, seq, nblk):
    step = pl.program_id(1)
    band = 3 * SWA_BLOCK
    grp = SWA_HEADS // SWA_KV_HEADS
    rows = grp * SWA_BLOCK
    row_in_blk = lax.broadcasted_iota(jnp.int32, (rows, band), 0) % SWA_BLOCK
    col = lax.broadcasted_iota(jnp.int32, (rows, band), 1)
    sinks = [jnp.concatenate(
        [jnp.broadcast_to(sink_ref[kh * grp + g:kh * grp + g + 1, 0:1], (SWA_BLOCK, 1)) for g in range(grp)],
        axis=0) for kh in range(SWA_KV_HEADS)]
    qs, ks, vs, valid, sink = [], [], [], [], []
    for i in range(nblk):
        n = step * nblk + i
        start = pl.multiple_of(jnp.clip((n - 1) * SWA_BLOCK, 0, seq - band), SWA_BLOCK)
        kb = k_ref[pl.ds(start, band), :]
        vb = v_ref[pl.ds(start, band), :]
        q = q_ref[i * SWA_BLOCK:(i + 1) * SWA_BLOCK, :]
        ok = jnp.abs(n * SWA_BLOCK + row_in_blk - (start + col)) <= SWA_WINDOW
        for kh in range(SWA_KV_HEADS):
            qs.append(jnp.concatenate(
                [q[:, (kh * grp + g) * HEAD_DIM:(kh * grp + g + 1) * HEAD_DIM] for g in range(grp)], axis=0))
            ks.append(kb[:, kh * HEAD_DIM:(kh + 1) * HEAD_DIM])
            vs.append(vb[:, kh * HEAD_DIM:(kh + 1) * HEAD_DIM])
            valid.append(ok)
            sink.append(sinks[kh])
    sink = jnp.stack(sink)
    s = jnp.where(jnp.stack(valid), _bmm_nt(jnp.stack(qs), jnp.stack(ks)), NEG_INF)
    m = jnp.maximum(jnp.max(s, axis=-1, keepdims=True), sink)
    p = jnp.exp(s - m)
    denom = jnp.sum(p, axis=-1, keepdims=True) + jnp.exp(sink - m)
    o = _bmm(p.astype(bf16), jnp.stack(vs)) / denom
    for i in range(nblk):
        o_ref[i * SWA_BLOCK:(i + 1) * SWA_BLOCK, :] = jnp.concatenate(
            [o[i * SWA_KV_HEADS + kh, g * SWA_BLOCK:(g + 1) * SWA_BLOCK]
             for kh in range(SWA_KV_HEADS) for g in range(grp)], axis=-1)


def _swa_mixer(h_swa, cos_h, sin_h, q_gain, k_gain, sink, batch, seq):
    t = h_swa.shape[0]
    tm = min(512, seq)
    row = lambda w: pl.BlockSpec((tm, w), lambda i: (i, 0))
    const = pl.BlockSpec((1, LANES), lambda i: (0, 0))
    q, k, v = pl.pallas_call(
        _swa_prep_kernel, grid=(t // tm,),
        in_specs=[row(SWA_COLS), row(LANES), row(LANES), const, const],
        out_specs=[row(SWA_Q), row(SWA_KV), row(SWA_KV)],
        out_shape=[jax.ShapeDtypeStruct((t, SWA_Q), bf16),
                   jax.ShapeDtypeStruct((t, SWA_KV), bf16),
                   jax.ShapeDtypeStruct((t, SWA_KV), bf16)],
        compiler_params=_cparams("parallel"), name="swa_prep",
    )(h_swa, cos_h, sin_h, jnp.tile(q_gain, 2).reshape(1, LANES), jnp.tile(k_gain, 2).reshape(1, LANES))
    nblk = SWA_BLOCKS_PER_STEP
    tq = nblk * SWA_BLOCK
    nb = seq // tq
    sink_t = jnp.broadcast_to(sink.astype(f32).reshape(SWA_HEADS, 1), (SWA_HEADS, LANES))
    return pl.pallas_call(
        functools.partial(_swa_attn_kernel, seq=seq, nblk=nblk), grid=(batch, nb),
        in_specs=[pl.BlockSpec((tq, SWA_Q), lambda b, n: (b * nb + n, 0)),
                  pl.BlockSpec((seq, SWA_KV), lambda b, n: (b, 0)),
                  pl.BlockSpec((seq, SWA_KV), lambda b, n: (b, 0)),
                  pl.BlockSpec((SWA_HEADS, LANES), lambda b, n: (0, 0))],
        out_specs=pl.BlockSpec((tq, SWA_Q), lambda b, n: (b * nb + n, 0)),
        out_shape=jax.ShapeDtypeStruct((t, SWA_Q), f32),
        compiler_params=_cparams("parallel", "parallel"), name="swa_attn",
    )(q, k, v, sink_t)


def _mla_prep_kernel(h_ref, cos_ref, sin_ref, qag_ref, kvag_ref, wq_ref, wk_ref, wpe_ref, wv_ref,
                     qg_ref, kg_ref, q_ref, k_ref, v_ref):
    cos_t = cos_ref[...]
    sin_t = sin_ref[...]
    lane = lax.broadcasted_iota(jnp.int32, cos_t.shape, 1)
    first = lane < MLA_NOPE + MLA_ROPE // 2
    cq = h_ref[:, :MLA_Q_RANK]
    cq = cq * lax.rsqrt(jnp.mean(cq * cq, axis=-1, keepdims=True) + NORM_EPS) * qag_ref[...]
    ckv = h_ref[:, MLA_Q_RANK:MLA_Q_RANK + MLA_KV_RANK]
    ckv = (ckv * lax.rsqrt(jnp.mean(ckv * ckv, axis=-1, keepdims=True) + NORM_EPS) * kvag_ref[...]).astype(bf16)
    kpe = h_ref[:, MLA_Q_RANK + MLA_KV_RANK:MLA_COLS_PAD].astype(bf16)
    q_all = _dot(cq.astype(bf16), wq_ref[...])
    k_all = _dot(ckv, wk_ref[...]) + _dot(kpe, wpe_ref[...])
    v_all = _dot(ckv, wv_ref[...])
    one_lane = lane == MLA_V
    scale = MLA_QK ** -0.5 * math.log2(math.e)

    def head_norm(x, gain):
        ms = jnp.sum(x * x, axis=-1, keepdims=True) * (1.0 / MLA_QK)
        x = x * lax.rsqrt(ms + NORM_EPS) * gain
        return _rope_lanes(x, cos_t, sin_t, MLA_ROPE // 2, first)

    for h in range(MLA_HEADS):
        sl = slice(h * MLA_HEAD_PAD, (h + 1) * MLA_HEAD_PAD)
        q_ref[0, h] = (head_norm(q_all[:, sl], qg_ref[...]) * scale).astype(bf16)
        k_ref[0, h] = head_norm(k_all[:, sl], kg_ref[...]).astype(bf16)
        v_ref[0, h] = jnp.where(one_lane, 1.0, v_all[:, sl]).astype(bf16)


def _mla_attn_kernel(q_ref, k_ref, v_ref, o_ref, *, tk, nk):
    n_heads = q_ref.shape[1]
    tq = q_ref.shape[2]
    qs = [q_ref[0, hh] for hh in range(n_heads)]

    def body(j, carry):
        off = pl.multiple_of(j * tk, tk)
        new = []
        for hh in range(n_heads):
            m, acc = carry[hh]
            kj = k_ref[0, hh, pl.ds(off, tk), :]
            vj = v_ref[0, hh, pl.ds(off, tk), :]
            s = _dot_nt(qs[hh], kj)
            m_new = jnp.maximum(m, jnp.max(s, axis=-1, keepdims=True))
            alpha = jnp.exp2(m - m_new)
            p = jnp.exp2(s - m_new)
            new.append((m_new, acc * alpha + _dot(p.astype(bf16), vj)))
        return tuple(new)

    init = tuple((jnp.full((tq, 1), NEG_INF, f32), jnp.zeros((tq, MLA_HEAD_PAD), f32))
                 for _ in range(n_heads))
    final = lax.fori_loop(0, nk, body, init)
    o_ref[0] = jnp.concatenate([acc[:, :MLA_V] / acc[:, MLA_V:MLA_V + 1] for _, acc in final], axis=-1)


def _mla_weights(w_uq, w_ukv, q_gain, k_gain):
    hp = MLA_HEAD_PAD
    wq = jnp.zeros((MLA_Q_RANK, MLA_HEADS, hp), f32).at[:, :, :MLA_QK].set(
        w_uq.reshape(MLA_Q_RANK, MLA_HEADS, MLA_QK))
    kv = w_ukv.reshape(MLA_KV_RANK, MLA_HEADS, MLA_NOPE + MLA_V)
    wk = jnp.zeros((MLA_KV_RANK, MLA_HEADS, hp), f32).at[:, :, :MLA_NOPE].set(kv[:, :, :MLA_NOPE])
    wv = jnp.zeros((MLA_KV_RANK, MLA_HEADS, hp), f32).at[:, :, :MLA_V].set(kv[:, :, MLA_NOPE:])
    eye = jnp.eye(MLA_ROPE, dtype=f32)
    wpe = jnp.zeros((LANES, MLA_HEADS, hp), f32).at[:MLA_ROPE, :, MLA_NOPE:MLA_QK].set(
        jnp.broadcast_to(eye[:, None, :], (MLA_ROPE, MLA_HEADS, MLA_ROPE)))
    flat = lambda w: w.reshape(w.shape[0], MLA_HEADS * hp).astype(bf16)
    pad = lambda g: jnp.zeros((1, hp), f32).at[0, :MLA_QK].set(g)
    return flat(wq), flat(wk), flat(wpe), flat(wv), pad(q_gain), pad(k_gain)


def _mla_mixer(h_mla, cos_r, sin_r, q_a_gain, kv_a_gain, w_uq, w_ukv, q_gain, k_gain, batch, seq):
    t = h_mla.shape[0]
    tm = min(512, seq)
    per_batch = seq // tm
    hp = MLA_HEAD_PAD
    wq, wk, wpe, wv, qg, kg = _mla_weights(w_uq, w_ukv, q_gain, k_gain)
    row = lambda w: pl.BlockSpec((tm, w), lambda b, i: (b * per_batch + i, 0))
    const = lambda a: pl.BlockSpec(a.shape, lambda b, i: (0,) * a.ndim)
    head_out = pl.BlockSpec((1, MLA_HEADS, tm, hp), lambda b, i: (b, 0, i, 0))
    head_shape = jax.ShapeDtypeStruct((batch, MLA_HEADS, seq, hp), bf16)
    qag = q_a_gain.reshape(1, MLA_Q_RANK)
    kvag = kv_a_gain.reshape(1, MLA_KV_RANK)
    q, k, v = pl.pallas_call(
        _mla_prep_kernel, grid=(batch, per_batch),
        in_specs=[row(MLA_COLS_PAD), row(LANES), row(LANES), const(qag), const(kvag),
                  const(wq), const(wk), const(wpe), const(wv), const(qg), const(kg)],
        out_specs=[head_out, head_out, head_out],
        out_shape=[head_shape, head_shape, head_shape],
        compiler_params=_cparams("parallel", "parallel"), name="mla_prep",
    )(h_mla, cos_r, sin_r, qag, kvag, wq, wk, wpe, wv, qg, kg)
    tq = min(1024, seq)
    tk = min(2048, seq)
    kv_spec = pl.BlockSpec((1, 2, seq, hp), lambda b, h, i: (b, h, 0, 0))
    o = pl.pallas_call(
        functools.partial(_mla_attn_kernel, tk=tk, nk=seq // tk),
        grid=(batch, MLA_HEADS // 2, seq // tq),
        in_specs=[pl.BlockSpec((1, 2, tq, hp), lambda b, h, i: (b, h, i, 0)), kv_spec, kv_spec],
        out_specs=pl.BlockSpec((1, tq, 2 * MLA_V), lambda b, h, i: (b, i, h)),
        out_shape=jax.ShapeDtypeStruct((batch, seq, MLA_HEADS * MLA_V), f32),
        compiler_params=_cparams("parallel", "parallel", "parallel"), name="mla_attn",
    )(q, k, v)
    return o.reshape(t, MLA_HEADS * MLA_V)


def _head_sum(x, bd):
    hi, lo = _split_bf16(x)
    return _dot(hi, bd) + _dot(lo, bd)


def _rwkv_prep_kernel(h_ref, prev_ref, next_ref, mu_ref, w0_ref, wup_ref, a0_ref, aup_ref, gup_ref,
                      kk_ref, ka_ref, rk_ref, bd_ref,
                      shared_ref, g_ref, bonus_ref, dir_ref, *, per_batch):
    i = pl.program_id(0)
    x = h_ref[...]
    tm = x.shape[0]
    row = lax.broadcasted_iota(jnp.int32, x.shape, 0)
    prev_row = jnp.where(i % per_batch == 0, 0.0, prev_ref[7:8, :])
    next_row = jnp.where(i % per_batch == per_batch - 1, 0.0, next_ref[0:1, :])
    x_prev = jnp.where(row == 0, prev_row, pltpu.roll(x, 1, axis=0))
    x_next = jnp.where(row == tm - 1, next_row, pltpu.roll(x, tm - 1, axis=0))
    hs = x + mu_ref[...] * (0.5 * (x_prev + x_next) - x)
    gw = GROUP_WIDTH
    r = hs[:, 0:gw]
    k = hs[:, gw:2 * gw]
    v = hs[:, 2 * gw:3 * gw]
    wd = hs[:, 3 * gw:3 * gw + LANES]
    ad = hs[:, 3 * gw + LANES:3 * gw + 2 * LANES]
    gd = hs[:, 3 * gw + 2 * LANES:RWKV_COLS]
    bd = bd_ref[...]
    lora_w = _dot(jnp.tanh(wd).astype(bf16), wup_ref[...])
    lora_a = _dot(ad.astype(bf16), aup_ref[...])
    kk = k * kk_ref[...]
    kk = kk * lax.rsqrt(_head_sum(kk * kk, bd) + 1e-12)
    shared_ref[:, 0:gw] = r
    shared_ref[:, gw:2 * gw] = v
    shared_ref[:, 2 * gw:3 * gw] = -kk
    g_ref[...] = _dot(_sigmoid(gd).astype(bf16), gup_ref[...])
    bonus_ref[...] = _head_sum(r * k * rk_ref[...], bd) * v
    for e in range(2):
        z = -(w0_ref[e:e + 1, :] + lora_w[:, e * gw:(e + 1) * gw])
        softplus = jnp.maximum(z, 0.0) + jnp.log(1.0 + jnp.exp(-jnp.abs(z)))
        dir_ref[e, :, 0:gw] = -jnp.exp(-softplus - 0.5)
        a = _sigmoid(a0_ref[e:e + 1, :] + lora_a[:, e * gw:(e + 1) * gw])
        dir_ref[e, :, gw:2 * gw] = k * (1.0 + (a - 1.0) * ka_ref[...])
        dir_ref[e, :, 2 * gw:3 * gw] = kk * a


def _bmm(a, b):
    return lax.dot_general(a, b, (((2,), (1,)), ((0,), (0,))), preferred_element_type=f32)


def _bmm_nt(a, b):
    return lax.dot_general(a, b, (((2,), (2,)), ((0,), (0,))), preferred_element_type=f32)


def _bmm_tn(a, b):
    return lax.dot_general(a, b, (((1,), (1,)), ((0,), (0,))), preferred_element_type=f32)


def _rwkv_chunk_kernel(shared_f, shared_b, dir_f, dir_b, o_f, o_b, h_scr, *, batch, ch):
    L = RWKV_CHUNK
    N = HEAD_DIM
    W = RWKV_GROUP_LANES
    hpg = W // N
    n_grp = GROUP_WIDTH // W

    @pl.when(pl.program_id(0) == 0)
    def _():
        h_scr[...] = jnp.zeros_like(h_scr)

    units = [(j, d, b, g) for j in range(ch) for d in range(2) for b in range(batch) for g in range(n_grp)]
    per_step = 2 * batch * n_grp
    n_units = len(units)

    def rows_of(j, d):
        return pl.ds((ch - 1 - j if d else j) * L, L)

    def load(ref_f, ref_b, lead, field):
        return jnp.stack([(ref_b if d else ref_f)[lead + (b, rows_of(j, d),
                                                          pl.ds(field * GROUP_WIDTH + g * W, W))]
                          for (j, d, b, g) in units])

    r = load(shared_f, shared_b, (), 0)
    v = load(shared_f, shared_b, (), 1)
    an = load(shared_f, shared_b, (), 2)
    lw = load(dir_f, dir_b, (0,), 0)
    kd = load(dir_f, dir_b, (0,), 1)
    bdir = load(dir_f, dir_b, (0,), 2)

    shape = (n_units, L, W)
    ui = lax.broadcasted_iota(jnp.int32, shape, 0)
    ti = lax.broadcasted_iota(jnp.int32, shape, 1)
    si = lax.broadcasted_iota(jnp.int32, shape, 2) % N
    sign = 1 - 2 * ((ui // (batch * n_grp)) % 2)
    ahead = (ti - si) * sign
    strict = ahead > 0
    incl = ahead >= 0
    bi = lax.broadcasted_iota(jnp.int32, (W, W), 0) // N
    bj = lax.broadcasted_iota(jnp.int32, (W, W), 1) // N
    diag_blocks = bi == bj

    def bdiag(x):
        return jnp.where(diag_blocks, jnp.concatenate([x] * hpg, axis=1), 0.0).astype(bf16)

    t2 = lax.broadcasted_iota(jnp.int32, (L, L), 0)
    s2 = lax.broadcasted_iota(jnp.int32, (L, L), 1)
    cum_parts = [None] * n_units
    for d in range(2):
        sel = [i for i, u in enumerate(units) if u[1] == d]
        tri = jnp.where((s2 >= t2) if d else (s2 <= t2), 1.0, 0.0).astype(bf16)
        hi, lo = _split_bf16(jnp.concatenate([lw[i] for i in sel], axis=1))
        cum_cat = _dot(tri, hi) + _dot(tri, lo)
        for k, i in enumerate(sel):
            cum_parts[i] = cum_cat[:, k * W:(k + 1) * W]
    cum = jnp.stack(cum_parts)
    p_incl = jnp.exp(cum)
    p_inv = jnp.exp(-cum)
    p_prev = jnp.exp(cum - lw)
    p_last = jnp.exp(jnp.sum(lw, axis=1, keepdims=True))
    a_t = an * p_prev
    b_t = bdir * p_inv
    k_t = kd * p_inv
    r_t = r * p_incl
    b_h = b_t * p_last
    k_h = k_t * p_last

    ar = jnp.concatenate([a_t, r_t], axis=1).astype(bf16)
    prod_b = _bmm_nt(ar, bdiag(b_t))
    prod_k = _bmm_nt(ar, bdiag(k_t))
    a_ab = jnp.where(strict, prod_b[:, :L], 0.0)
    a_rb = jnp.where(incl, prod_b[:, L:], 0.0).astype(bf16)
    a_ak = jnp.where(strict, prod_k[:, :L], 0.0).astype(bf16)
    a_rk = jnp.where(incl, prod_k[:, L:], 0.0).astype(bf16)

    m = 1
    t_inv = None
    while m < L:
        same = (ti // (2 * m)) == (si // (2 * m))
        later_half = ((ti // m) % 2 - (si // m) % 2) * sign == 1
        a_off = jnp.where(same, jnp.where(later_half, a_ab, 0.0), 0.0)
        if m == 1:
            t_inv = jnp.where(ti == si, 1.0, 0.0) + a_off
        else:
            left = _bmm(t_inv.astype(bf16), bdiag(a_off)).astype(bf16)
            t_inv = t_inv + _bmm(left, bdiag(t_inv))
        m *= 2

    av = _bmm(jnp.concatenate([a_ak, a_rk], axis=1), bdiag(v))
    akv = av[:, :L]
    tb = t_inv.astype(bf16)
    w_mat = _bmm(tb, bdiag(a_t))
    z_mat = _bmm(tb, bdiag(akv))
    q_mat = r_t + _bmm(a_rb, bdiag(w_mat))
    y_mat = _bmm(a_rb, bdiag(z_mat)) + av[:, L:]

    for j in range(ch):
        sl = slice(j * per_step, (j + 1) * per_step)
        h_t = h_scr[...]
        qw = jnp.concatenate([q_mat[sl], w_mat[sl]], axis=1).astype(bf16)
        ou = _bmm_nt(qw, h_t.astype(bf16))
        out = ou[:, :L] + y_mat[sl]
        u_mat = ou[:, L:] + z_mat[sl]
        keys = jnp.concatenate([b_h[sl], k_h[sl]], axis=1).astype(bf16)
        vals = jnp.concatenate([u_mat, v[sl]], axis=1).astype(bf16)
        upd = _bmm_tn(vals, keys)
        h_scr[...] = jnp.where(diag_blocks, p_last[sl] * h_t + upd, 0.0)
        for k, (_, d, b, g) in enumerate(units[sl]):
            (o_b if d else o_f)[b, rows_of(j, d), pl.ds(g * W, W)] = out[k]


def _rwkv_post_kernel(of_ref, ob_ref, bonus_ref, g_ref, lnw_ref, lnb_ref, bd_ref, out_ref):
    y = of_ref[...] + ob_ref[...]
    bd = bd_ref[...]
    mean = _head_sum(y, bd) * (1.0 / HEAD_DIM)
    yc = y - mean
    var = _head_sum(yc * yc, bd) * (1.0 / HEAD_DIM)
    yn = yc * lax.rsqrt(var + RWKV_GN_EPS) * lnw_ref[...] + lnb_ref[...]
    out_ref[...] = (yn + bonus_ref[...]) * g_ref[...]


def _rwkv_mixer(h_rwkv, mu, w0, w_up, a0, a_up, g_up, k_k, k_a, r_k, ln_w, ln_b, batch, seq):
    t = h_rwkv.shape[0]
    gw = GROUP_WIDTH
    tm = min(256, seq)
    per_batch = seq // tm
    n_halo = t // 8
    wup = jnp.zeros((2 * RWKV_LORA, 2 * gw), f32)
    aup = jnp.zeros((2 * RWKV_LORA, 2 * gw), f32)
    for e in range(2):
        wup = wup.at[e * RWKV_LORA:(e + 1) * RWKV_LORA, e * gw:(e + 1) * gw].set(w_up[e])
        aup = aup.at[e * RWKV_LORA:(e + 1) * RWKV_LORA, e * gw:(e + 1) * gw].set(a_up[e])
    head_id = jnp.arange(gw) // HEAD_DIM
    bd = (head_id[:, None] == head_id[None, :]).astype(bf16)
    vec = lambda a: a.reshape(1, -1).astype(f32)
    const = lambda a: pl.BlockSpec(a.shape, lambda i: (0,) * a.ndim)
    consts = [vec(mu), w0.astype(f32), wup.astype(bf16), a0.astype(f32), aup.astype(bf16),
              g_up.astype(bf16), vec(k_k), vec(k_a), vec(r_k), bd]
    row = pl.BlockSpec((tm, gw), lambda i: (i, 0))
    one = jax.ShapeDtypeStruct((t, gw), f32)
    n_field = 3
    shared, g, bonus, per_dir = pl.pallas_call(
        functools.partial(_rwkv_prep_kernel, per_batch=per_batch), grid=(t // tm,),
        in_specs=[pl.BlockSpec((tm, RWKV_COLS), lambda i: (i, 0)),
                  pl.BlockSpec((8, RWKV_COLS), lambda i: (jnp.maximum(i * (tm // 8) - 1, 0), 0)),
                  pl.BlockSpec((8, RWKV_COLS), lambda i: (jnp.minimum((i + 1) * (tm // 8), n_halo - 1), 0)),
                  ] + [const(a) for a in consts],
        out_specs=[pl.BlockSpec((tm, n_field * gw), lambda i: (i, 0)), row, row,
                   pl.BlockSpec((2, tm, n_field * gw), lambda i: (0, i, 0))],
        out_shape=[jax.ShapeDtypeStruct((t, n_field * gw), f32), one, one,
                   jax.ShapeDtypeStruct((2, t, n_field * gw), f32)],
        compiler_params=_cparams("parallel"), name="rwkv_prep",
    )(h_rwkv, h_rwkv, h_rwkv, *consts)

    L = RWKV_CHUNK
    ch = RWKV_CHUNKS_PER_STEP
    rows = ch * L
    n_steps = seq // rows
    n_grp = gw // RWKV_GROUP_LANES
    shared3 = shared.reshape(batch, seq, n_field * gw)
    per_dir4 = per_dir.reshape(2, batch, seq, n_field * gw)
    fwd = lambda w: pl.BlockSpec((batch, rows, w), lambda c: (0, c, 0))
    bwd = lambda w: pl.BlockSpec((batch, rows, w), lambda c: (0, n_steps - 1 - c, 0))
    fwd_dir = pl.BlockSpec((1, batch, rows, n_field * gw), lambda c: (0, 0, c, 0))
    bwd_dir = pl.BlockSpec((1, batch, rows, n_field * gw), lambda c: (1, 0, n_steps - 1 - c, 0))
    o3 = jax.ShapeDtypeStruct((batch, seq, gw), f32)
    o_fwd, o_bwd = pl.pallas_call(
        functools.partial(_rwkv_chunk_kernel, batch=batch, ch=ch), grid=(n_steps,),
        in_specs=[fwd(n_field * gw), bwd(n_field * gw), fwd_dir, bwd_dir],
        out_specs=[fwd(gw), bwd(gw)], out_shape=[o3, o3],
        scratch_shapes=[pltpu.VMEM((2 * batch * n_grp, RWKV_GROUP_LANES, RWKV_GROUP_LANES), f32)],
        compiler_params=_cparams("arbitrary"), name="rwkv_chunk",
    )(shared3, shared3, per_dir4, per_dir4)

    tp = min(512, seq)
    rowp = pl.BlockSpec((tp, gw), lambda i: (i, 0))
    cvec = pl.BlockSpec((1, gw), lambda i: (0, 0))
    return pl.pallas_call(
        _rwkv_post_kernel, grid=(t // tp,),
        in_specs=[rowp, rowp, rowp, rowp, cvec, cvec, pl.BlockSpec((gw, gw), lambda i: (0, 0))],
        out_specs=rowp, out_shape=one,
        compiler_params=_cparams("parallel"), name="rwkv_post",
    )(o_fwd.reshape(t, gw), o_bwd.reshape(t, gw), bonus, g, vec(ln_w), vec(ln_b), bd)


def _s5_operators(a_re, a_im, log_dt, b_re, b_im, c_re, c_im):
    L = S5_CHUNK
    lam_r, lam_i = a_re.astype(f32), a_im.astype(f32)
    dt = jnp.exp(log_dt.astype(f32))[..., None]
    mag = jnp.exp(lam_r * dt)
    abar_r, abar_i = mag * jnp.cos(lam_i * dt), mag * jnp.sin(lam_i * dt)
    den = lam_r * lam_r + lam_i * lam_i
    nr, ni = abar_r - 1.0, abar_i
    coef_r = (nr * lam_r + ni * lam_i) / den
    coef_i = (ni * lam_r - nr * lam_i) / den
    br, bi = b_re.astype(f32)[None], b_im.astype(f32)[None]
    bb_r = coef_r[..., None] * br - coef_i[..., None] * bi
    bb_i = coef_r[..., None] * bi + coef_i[..., None] * br
    steps = jnp.arange(L + 1, dtype=f32)[:, None, None, None]
    pmag = jnp.exp(steps * (lam_r * dt)[None])
    pw_r = pmag * jnp.cos(steps * (lam_i * dt)[None])
    pw_i = pmag * jnp.sin(steps * (lam_i * dt)[None])
    cr, ci = c_re.astype(f32)[None, None], c_im.astype(f32)[None, None]
    cp_r = cr * pw_r[:, :, :, None, :] - ci * pw_i[:, :, :, None, :]
    cp_i = cr * pw_i[:, :, :, None, :] + ci * pw_r[:, :, :, None, :]
    kern = (jnp.einsum('kegop,egpi->kegoi', cp_r, bb_r) - jnp.einsum('kegop,egpi->kegoi', cp_i, bb_i))
    kf, kb = kern[:L, 0], kern[:L, 1]
    by_lag = jnp.concatenate([kb[1:][::-1], kf[:1] + kb[:1], kf[1:], jnp.zeros_like(kf[:1])], axis=0)
    lag_tab = jnp.transpose(by_lag, (1, 3, 0, 2)).reshape(S5_GROUPS, S5_GROUP, 2 * L * S5_GROUP)
    g_, h_ = S5_GROUPS, S5_GROUP

    def state_in(pr, pi, e):
        re = pr[..., None] * bb_r[e][None] - pi[..., None] * bb_i[e][None]
        im = pr[..., None] * bb_i[e][None] + pi[..., None] * bb_r[e][None]
        return jnp.transpose(re, (1, 0, 3, 2)), jnp.transpose(im, (1, 0, 3, 2))

    f_re, f_im = state_in(pw_r[:L, 0][::-1], pw_i[:L, 0][::-1], 0)
    b_re_, b_im_ = state_in(pw_r[:L, 1], pw_i[:L, 1], 1)
    b_sum = jnp.concatenate([f_re, f_im, b_re_, b_im_], axis=-1).reshape(g_, L * h_, 4 * S5_STATE)

    def state_out(cpr, cpi):
        return jnp.transpose(cpr, (1, 3, 0, 2)), -jnp.transpose(cpi, (1, 3, 0, 2))

    fo_r, fo_i = state_out(cp_r[1:L + 1, 0], cp_i[1:L + 1, 0])
    bo_r, bo_i = state_out(cp_r[1:L + 1, 1][::-1], cp_i[1:L + 1, 1][::-1])
    c_out = jnp.concatenate([fo_r, fo_i, bo_r, bo_i], axis=1).reshape(g_, 4 * S5_STATE, L * h_)
    a_pow = jnp.stack([jnp.stack([pw_r[L, e], pw_i[L, e]]) for e in range(2)])
    return lag_tab, b_sum.astype(bf16), c_out.astype(bf16), a_pow


def _s5_state_in_kernel(u_ref, bsum_ref, s_ref):
    s_ref[0] = _dot(u_ref[0], bsum_ref[0])


def _s5_carry_kernel(sf_r, sf_i, sb_r, sb_i, af_r, af_i, ab_r, ab_i, xf_r, xf_i, xb_r, xb_i, *, nch):
    rows, p = af_r.shape
    zero = jnp.zeros((rows, p), f32)

    def body(i, carry):
        fr, fi, br, bi = carry
        j = nch - 1 - i
        xf_r[i] = fr
        xf_i[i] = fi
        xb_r[j] = br
        xb_i[j] = bi
        ar, ai = af_r[...], af_i[...]
        nfr = ar * fr - ai * fi + sf_r[i]
        nfi = ar * fi + ai * fr + sf_i[i]
        ar, ai = ab_r[...], ab_i[...]
        nbr = ar * br - ai * bi + sb_r[j]
        nbi = ar * bi + ai * br + sb_i[j]
        return nfr, nfi, nbr, nbi

    lax.fori_loop(0, nch, body, (zero, zero, zero, zero))


def _s5_out_kernel(u_ref, x_ref, lag_ref, cout_ref, y_ref):
    L, h_ = S5_CHUNK, S5_GROUP
    tab = lag_ref[0]
    width = tab.shape[1]
    blocks = []
    for s in range(L):
        start = (L - 1 - s) * h_
        win = pltpu.roll(tab, width - start, axis=1) if start else tab
        blocks.append(win[:, :L * h_])
    m_op = jnp.concatenate(blocks, axis=0).astype(bf16)
    y_ref[0] = _dot(u_ref[0], m_op) + _dot(x_ref[0], cout_ref[0])


def _s5_post_kernel(y_ref, u_ref, d_ref, w_ref, b_ref, o_ref):
    u = u_ref[...]
    z = y_ref[...] + d_ref[...] * u
    c0 = math.sqrt(2.0 / math.pi)
    z = 0.5 * z * (1.0 + jnp.tanh(c0 * (z + 0.044715 * (z * z * z))))
    gate = _dot(z.astype(bf16), w_ref[...]) + b_ref[...]
    o_ref[...] = z * _sigmoid(gate)


def _s5_mixer(u, a_re, a_im, log_dt, b_re, b_im, c_re, c_im, d_skip, glu_w, glu_b, batch, seq):
    t = u.shape[0]
    L, g_, h_, p = S5_CHUNK, S5_GROUPS, S5_GROUP, S5_STATE
    nch = seq // L
    rows = batch * nch
    lag_tab, b_sum, c_out, a_pow = _s5_operators(a_re, a_im, log_dt, b_re, b_im, c_re, c_im)
    ug = jnp.transpose(u.reshape(rows, L, g_, h_), (2, 0, 1, 3)).reshape(g_, rows, L * h_).astype(bf16)
    grp = lambda r, c: pl.BlockSpec((1, r, c), lambda g: (g, 0, 0))
    s_in = pl.pallas_call(
        _s5_state_in_kernel, grid=(g_,),
        in_specs=[grp(rows, L * h_), grp(L * h_, 4 * p)], out_specs=grp(rows, 4 * p),
        out_shape=jax.ShapeDtypeStruct((g_, rows, 4 * p), f32),
        compiler_params=_cparams("parallel"), name="s5_state_in",
    )(ug, b_sum)
    s_in = jnp.transpose(s_in.reshape(g_, batch, nch, 4, p), (3, 2, 1, 0, 4)).reshape(4, nch, batch * g_, p)
    coef = jnp.broadcast_to(a_pow[:, :, None], (2, 2, batch, g_, p)).reshape(2, 2, batch * g_, p)
    st = jax.ShapeDtypeStruct((nch, batch * g_, p), f32)
    rb = 8
    seq_spec = pl.BlockSpec((nch, rb, p), lambda i: (0, i, 0))
    coef_spec = pl.BlockSpec((rb, p), lambda i: (i, 0))
    xs = pl.pallas_call(
        functools.partial(_s5_carry_kernel, nch=nch), grid=(batch * g_ // rb,),
        in_specs=[seq_spec] * 4 + [coef_spec] * 4, out_specs=[seq_spec] * 4, out_shape=[st] * 4,
        compiler_params=_cparams("parallel"), name="s5_carry",
    )(s_in[0], s_in[1], s_in[2], s_in[3], coef[0, 0], coef[0, 1], coef[1, 0], coef[1, 1])
    x_cat = jnp.stack(xs).reshape(4, nch, batch, g_, p)
    x_cat = jnp.transpose(x_cat, (3, 2, 1, 0, 4)).reshape(g_, rows, 4 * p).astype(bf16)
    y = pl.pallas_call(
        _s5_out_kernel, grid=(g_,),
        in_specs=[grp(rows, L * h_), grp(rows, 4 * p), grp(h_, 2 * L * h_), grp(4 * p, L * h_)],
        out_specs=grp(rows, L * h_),
        out_shape=jax.ShapeDtypeStruct((g_, rows, L * h_), f32),
        compiler_params=_cparams("parallel"), name="s5_out",
    )(ug, x_cat, lag_tab, c_out)
    y = jnp.transpose(y.reshape(g_, rows, L, h_), (1, 2, 0, 3)).reshape(t, g_ * h_)
    tm = min(512, seq)
    w = g_ * h_
    row = pl.BlockSpec((tm, w), lambda i: (i, 0))
    cvec = pl.BlockSpec((1, w), lambda i: (0, 0))
    return pl.pallas_call(
        _s5_post_kernel, grid=(t // tm,),
        in_specs=[row, row, cvec, pl.BlockSpec((w, w), lambda i: (0, 0)), cvec],
        out_specs=row, out_shape=jax.ShapeDtypeStruct((t, w), f32),
        compiler_params=_cparams("parallel"), name="s5_post",
    )(y, u, d_skip.reshape(1, w).astype(f32), glu_w.astype(bf16), glu_b.reshape(1, w).astype(f32))


def _outproj_kernel(o0_ref, o1_ref, o2_ref, o3_ref, w_ref, x_ref, g_ref, out_ref):
    acc = None
    for i, o_ref in enumerate((o0_ref, o1_ref, o2_ref, o3_ref)):
        part = _dot(o_ref[...].astype(bf16), w_ref[i * GROUP_WIDTH:(i + 1) * GROUP_WIDTH, :])
        acc = part if acc is None else acc + part
    out_ref[...] = x_ref[...] + g_ref[0] * acc


def _outproj(outs, w_out, x, gate, seq):
    t, d = x.shape
    tm = min(512, seq)
    per_batch = seq // tm
    part = pl.BlockSpec((tm, GROUP_WIDTH), lambda i: (i, 0))
    row = pl.BlockSpec((tm, d), lambda i: (i, 0))
    return pl.pallas_call(
        _outproj_kernel, grid=(t // tm,),
        in_specs=[part, part, part, part, pl.BlockSpec((4 * GROUP_WIDTH, d), lambda i: (0, 0)), row,
                  pl.BlockSpec((1, 1, d), lambda i: (i // per_batch, 0, 0))],
        out_specs=row, out_shape=jax.ShapeDtypeStruct((t, d), f32),
        compiler_params=_cparams("parallel"), name="out_proj",
    )(*outs, w_out.astype(bf16), x, gate)


def _moe_up_kernel(xs_ref, w1_ref, w3_ref, hid_ref):
    b = pl.program_id(2)
    xs = xs_ref[b, 0]
    up = _dot(xs, w1_ref[0, 0].astype(bf16))
    lin = _dot(xs, w3_ref[0, 0].astype(bf16))
    hid_ref[0, 0] = (up * _sigmoid(up) * lin).astype(bf16)


def _moe_up(xs, w1, w3, layer):
    batch, n_exp, cap, d = xs.shape
    ff = w1.shape[3]
    tf = 512
    return pl.pallas_call(
        _moe_up_kernel, grid=(n_exp, ff // tf, batch),
        in_specs=[pl.BlockSpec((batch, 1, cap, d), lambda e, s, b: (0, e, 0, 0)),
                  pl.BlockSpec((1, 1, d, tf), lambda e, s, b: (layer, e, 0, s)),
                  pl.BlockSpec((1, 1, d, tf), lambda e, s, b: (layer, e, 0, s))],
        out_specs=pl.BlockSpec((1, 1, cap, tf), lambda e, s, b: (b, e, 0, s)),
        out_shape=jax.ShapeDtypeStruct((batch, n_exp, cap, ff), bf16),
        compiler_params=_cparams("parallel", "parallel", "arbitrary"), name="moe_up",
    )(xs, w1, w3)


def _moe_down_kernel(idx_ref, hid_ref, gate_ref, g2_ref, w2_ref, x_hbm, out_hbm, slab, ys_scr, sem,
                     *, width, group):
    b, h, e = pl.program_id(0), pl.program_id(1), pl.program_id(2)
    cols = pl.ds(pl.multiple_of(h * width, width), width)

    def slab_copy(src, dst):
        return pltpu.make_async_copy(src, dst, sem)

    @pl.when(e == 0)
    def _():
        cp = slab_copy(x_hbm.at[b, :, cols], slab)
        cp.start()
        cp.wait()

    ys_scr[...] = _dot(hid_ref[0, 0], w2_ref[0, 0].astype(bf16)) * gate_ref[0, 0] * g2_ref[0]
    cap = ys_scr.shape[0]

    def body(g, carry):
        r0 = pl.multiple_of(g * group, group)
        tok = [idx_ref[0, 0, r0 + k] for k in range(group)]
        rows = [slab[pl.ds(tok[k], 1), :] + ys_scr[pl.ds(r0 + k, 1), :] for k in range(group)]
        for k in range(group):
            slab[pl.ds(tok[k], 1), :] = rows[k]
        return carry

    lax.fori_loop(0, cap // group, body, 0)

    @pl.when(e == pl.num_programs(2) - 1)
    def _():
        cp = slab_copy(slab, out_hbm.at[b, :, cols])
        cp.start()
        cp.wait()


def _moe_down_combine(x, hid, gate, g2, w2, idx, layer):
    batch, seq, d = x.shape
    n_exp, cap = idx.shape[1], idx.shape[2]
    ff = hid.shape[3]
    width = d // 2
    return pl.pallas_call(
        functools.partial(_moe_down_kernel, width=width, group=8),
        grid=(batch, d // width, n_exp),
        in_specs=[pl.BlockSpec((1, 1, cap), lambda b, h, e: (b * n_exp + e, 0, 0), memory_space=pltpu.SMEM),
                  pl.BlockSpec((1, 1, cap, ff), lambda b, h, e: (b, e, 0, 0)),
                  pl.BlockSpec((1, 1, cap, 1), lambda b, h, e: (b, e, 0, 0)),
                  pl.BlockSpec((1, 1, width), lambda b, h, e: (b, 0, h)),
                  pl.BlockSpec((1, 1, ff, width), lambda b, h, e: (layer, e, 0, h)),
                  pl.BlockSpec(memory_space=pl.ANY)],
        out_specs=pl.BlockSpec(memory_space=pl.ANY),
        out_shape=jax.ShapeDtypeStruct((batch, seq, d), f32),
        scratch_shapes=[pltpu.VMEM((seq, width), f32), pltpu.VMEM((cap, width), f32),
                        pltpu.SemaphoreType.DMA(())],
        compiler_params=pltpu.CompilerParams(dimension_semantics=("arbitrary",) * 3,
                                             vmem_limit_bytes=MOE_DOWN_VMEM_LIMIT_BYTES),
        name="moe_down_combine",
    )(idx.reshape(batch * n_exp, 1, cap), hid, gate, g2, w2, x)


def _moe_residual(x, xn, logits, g2, w1, w3, w2, layer, batch, seq):
    d = xn.shape[1]
    capacity = EC_CAPACITY_FACTOR * seq // N_EXPERTS
    aff = jax.nn.softmax(logits.reshape(batch, seq, N_EXPERTS), axis=-1)
    gate, idx = lax.top_k(jnp.swapaxes(aff, 1, 2), capacity)
    bidx = jnp.arange(batch)[:, None, None]
    xs = xn.reshape(batch, seq, d)[bidx, idx]
    hid = _moe_up(xs, w1, w3, layer)
    out = _moe_down_combine(x.reshape(batch, seq, d), hid, gate[..., None], g2, w2, idx, layer)
    return out.reshape(batch * seq, d)


def _in_proj_weights(w_in_l):
    o1, o2, o3 = SWA_COLS, SWA_COLS + MLA_COLS, SWA_COLS + MLA_COLS + RWKV_COLS
    w_mla = jnp.zeros((w_in_l.shape[0], MLA_COLS_PAD), f32).at[:, :MLA_COLS].set(w_in_l[:, o1:o2])
    return (w_in_l[:, :o1].astype(bf16), w_mla.astype(bf16),
            w_in_l[:, o2:o3].astype(bf16), w_in_l[:, o3:].astype(bf16))


def kernel(x, c, positions, ada_w, ada_b, norm1_g, norm2_g, w_in, w_out, swa_q_gain, swa_k_gain, swa_sink, mla_q_a_gain, mla_kv_a_gain, mla_w_uq, mla_w_ukv, mla_q_gain, mla_k_gain, rwkv_mu, rwkv_w0, rwkv_w_up, rwkv_a0, rwkv_a_up, rwkv_g_up, rwkv_k_k, rwkv_k_a, rwkv_r_k, rwkv_ln_w, rwkv_ln_b, s5_a_re, s5_a_im, s5_log_dt, s5_b_re, s5_b_im, s5_c_re, s5_c_im, s5_d, s5_glu_w, s5_glu_b, router_w, moe_w1, moe_w3, moe_w2):
    batch, seq, d = x.shape
    depth = ada_w.shape[0]
    t = batch * seq
    cos_h, sin_h, cos_r, sin_r = _rope_tables(positions)
    mod = _ada_mod(c, ada_w, ada_b)
    xf = x.reshape(t, d)
    for l in range(depth):
        sh1, sc1, g1, sh2, sc2, g2 = [m[:, None, :] for m in jnp.split(mod[l], 6, axis=-1)]
        xn = _norm_call(xf, norm1_g[l], sc1, sh1, seq)
        w_swa, w_mla, w_rwkv, w_s5 = _in_proj_weights(w_in[l])
        h_swa = _matmul(xn, w_swa, tm=1024, tn=SWA_COLS, name="in_proj_swa")
        h_mla = _matmul(xn, w_mla, tm=1024, tn=MLA_COLS_PAD, name="in_proj_mla")
        h_rwkv = _matmul(xn, w_rwkv, tm=1024, tn=RWKV_COLS, name="in_proj_rwkv")
        h_s5 = _matmul(xn, w_s5, tm=1024, tn=GROUP_WIDTH, name="in_proj_s5")
        o_swa = _swa_mixer(h_swa, cos_h, sin_h, swa_q_gain[l], swa_k_gain[l], swa_sink[l], batch, seq)
        o_mla = _mla_mixer(h_mla, cos_r, sin_r, mla_q_a_gain[l], mla_kv_a_gain[l], mla_w_uq[l],
                           mla_w_ukv[l], mla_q_gain[l], mla_k_gain[l], batch, seq)
        o_rwkv = _rwkv_mixer(h_rwkv, rwkv_mu[l], rwkv_w0[l], rwkv_w_up[l], rwkv_a0[l], rwkv_a_up[l],
                             rwkv_g_up[l], rwkv_k_k[l], rwkv_k_a[l], rwkv_r_k[l],
                             rwkv_ln_w[l], rwkv_ln_b[l], batch, seq)
        o_s5 = _s5_mixer(h_s5, s5_a_re[l], s5_a_im[l], s5_log_dt[l], s5_b_re[l], s5_b_im[l],
                         s5_c_re[l], s5_c_im[l], s5_d[l], s5_glu_w[l], s5_glu_b[l], batch, seq)
        xf = _outproj((o_swa, o_mla, o_rwkv, o_s5), w_out[l], xf, g1, seq)
        xn2, logits = _norm_call(xf, norm2_g[l], sc2, sh2, seq, router_w=router_w[l])
        xf = _moe_residual(xf, xn2, logits, g2, moe_w1, moe_w3, moe_w2, l, batch, seq)
    return xf.reshape(batch, seq, d)
```
